```python
import math
import jax, jax.numpy as jnp
from jax import lax
import numpy as np

D_MODEL = 1024
BATCH = 8
SEQ = 4096
DEPTH = 2

HEAD_DIM = 64
DIL_GROUPS = ((128, 1), (512, 4), (2048, 16))
HEADS_PER_GROUP = 4
N_GROUPS = len(DIL_GROUPS)
N_HEADS_A = HEADS_PER_GROUP * N_GROUPS
N_HEADS_B = 4
WIDTH_A = N_HEADS_A * HEAD_DIM
WIDTH_A_OUT = HEADS_PER_GROUP * HEAD_DIM
WIDTH_B = N_HEADS_B * HEAD_DIM
N_BRANCH = 2
SPLIT_SIZES = (WIDTH_A, WIDTH_A, WIDTH_A, WIDTH_B, WIDTH_B, WIDTH_B, N_HEADS_B, D_MODEL, D_MODEL)
SPLIT_POINTS = tuple(int(v) for v in np.cumsum(SPLIT_SIZES)[:-1])
D_IN = int(sum(SPLIT_SIZES))
D_FF = -(-8 * D_MODEL // (3 * 256)) * 256
BLOCK = 128
ALIBI_MAX = 8.0
EPS = 1e-6
N_MOD = 6

kernel_name = "hybrid_dilated_fox_adaln_block"


def rms_norm(x, g):
    xf = x.astype(jnp.float32)
    y = xf * lax.rsqrt(jnp.mean(xf * xf, axis=-1, keepdims=True) + EPS)
    return (y * g.astype(jnp.float32)).astype(x.dtype)


def alibi_slopes():
    h = np.arange(1, N_HEADS_A + 1, dtype=np.float32)
    return jnp.asarray(2.0 ** (-ALIBI_MAX * h / N_HEADS_A), dtype=jnp.float32)


def dilated_group_attention(q, k, v, window, dilation, slopes):
    B, S, H, Dh = q.shape
    L = S // dilation
    nb = -(-L // BLOCK)
    Lp = nb * BLOCK
    pad_end = Lp - L

    def split(t):
        return t.reshape(B, L, dilation, H, Dh).transpose(0, 2, 1, 3, 4)

    def band(t):
        t = jnp.pad(split(t), ((0, 0), (0, 0), (BLOCK, pad_end), (0, 0), (0, 0)))
        t = t.reshape(B, dilation, nb + 1, BLOCK, H, Dh)
        return jnp.concatenate([t[:, :, :-1], t[:, :, 1:]], axis=3)

    qb = jnp.pad(split(q), ((0, 0), (0, 0), (0, pad_end), (0, 0), (0, 0)))
    qb = qb.reshape(B, dilation, nb, BLOCK, H, Dh)
    kb, vb = band(k), band(v)

    scores = jnp.einsum('brnqhd,brnkhd->brnhqk', qb, kb).astype(jnp.float32) * (Dh ** -0.5)
    qi = jnp.arange(BLOCK)[:, None]
    kj = jnp.arange(2 * BLOCK)[None, :]
    dist = qi + BLOCK - kj
    kpos = jnp.arange(nb)[:, None, None] * BLOCK + kj[None] - BLOCK
    valid = (dist >= 0) & (dist <= window // dilation) & (kpos >= 0)
    bias = -slopes[:, None, None] * (dist * dilation).astype(jnp.float32)[None]
    logits = jnp.where(valid[None, None, :, None], scores + bias, -jnp.inf)
    lse = jax.nn.logsumexp(logits, axis=-1)
    p = jnp.exp(logits - lse[..., None])
    out = jnp.einsum('brnhqk,brnkhd->brnqhd', p.astype(v.dtype), vb)
    out = out.reshape(B, dilation, Lp, H, Dh)[:, :, :L].transpose(0, 2, 1, 3, 4).reshape(B, S, H, Dh)
    lse = lse.transpose(0, 1, 2, 4, 3).reshape(B, dilation, Lp, H)[:, :, :L]
    lse = lse.transpose(0, 2, 1, 3).reshape(B, S, H)
    return out, lse


def dilated_mixture(qa, ka, va):
    B, S, _ = qa.shape
    shp = (B, S, N_HEADS_A, HEAD_DIM)
    qa, ka, va = qa.reshape(shp), ka.reshape(shp), va.reshape(shp)
    slopes = alibi_slopes()
    outs, lses = [], []
    for g, (w, d) in enumerate(DIL_GROUPS):
        sl = slice(g * HEADS_PER_GROUP, (g + 1) * HEADS_PER_GROUP)
        o, l = dilated_group_attention(qa[:, :, sl], ka[:, :, sl], va[:, :, sl], w, d, slopes[sl])
        outs.append(o)
        lses.append(l)
    outs = jnp.stack(outs, axis=0)
    alpha = jax.nn.softmax(jnp.stack(lses, axis=0), axis=0)
    y = jnp.sum(alpha[..., None].astype(outs.dtype) * outs, axis=0)
    return y.reshape(B, S, WIDTH_A_OUT)


def forgetting_attention(qb, kb, vb, f_logit, b_forget):
    B, S, _ = qb.shape
    H, Dh = N_HEADS_B, HEAD_DIM
    q = qb.reshape(B, S, H, Dh)
    k = kb.reshape(B, S, H, Dh)
    v = vb.reshape(B, S, H, Dh)
    log_f = jax.nn.log_sigmoid(f_logit.astype(jnp.float32) + b_forget.astype(jnp.float32))
    F = jnp.cumsum(log_f, axis=1).transpose(0, 2, 1)
    nb = S // BLOCK
    q_blocks = q.reshape(B, nb, BLOCK, H, Dh).transpose(1, 0, 2, 3, 4)
    F_blocks = F.reshape(B, H, nb, BLOCK).transpose(2, 0, 1, 3)
    kpos = jnp.arange(S)
    scale = Dh ** -0.5

    def one_block(args):
        i, qi, Fi = args
        s = jnp.einsum('bqhd,bkhd->bhqk', qi, k).astype(jnp.float32) * scale
        s = s + (Fi[..., :, None] - F[..., None, :])
        qpos = i * BLOCK + jnp.arange(BLOCK)
        s = jnp.where(kpos[None, :] <= qpos[:, None], s, -jnp.inf)
        p = jax.nn.softmax(s, axis=-1)
        return jnp.einsum('bhqk,bkhd->bqhd', p.astype(v.dtype), v)

    out = lax.map(one_block, (jnp.arange(nb), q_blocks, F_blocks))
    return out.transpose(1, 0, 2, 3, 4).reshape(B, S, H * Dh)


def token_mixer(h, w_in, b_forget, w_up_a, w_up_b, w_out):
    z = h @ w_in
    qa, ka, va, qb, kb, vb, fz, gza, gzb = jnp.split(z, SPLIT_POINTS, axis=-1)
    ya = dilated_mixture(qa, ka, va) @ w_up_a
    yb = forgetting_attention(qb, kb, vb, fz, b_forget) @ w_up_b
    merged = jax.nn.sigmoid(gza) * ya + jax.nn.sigmoid(gzb) * yb
    return merged @ w_out


def swiglu(h, w_ffn_in, w_ffn_out):
    gate, up = jnp.split(h @ w_ffn_in, 2, axis=-1)
    return (jax.nn.silu(gate) * up) @ w_ffn_out


def setup_inputs(seed: int = 0) -> dict:
    key = jax.random.key(seed)
    ks = jax.random.split(key, 16)
    f32 = jnp.float32
    nrm = lambda k, shape, s: jax.random.normal(k, shape, f32) * s
    x = jax.random.normal(ks[0], (BATCH, SEQ, D_MODEL), f32)
    c = jax.random.normal(ks[1], (BATCH, D_MODEL), f32)
    w_ada = nrm(ks[2], (DEPTH, D_MODEL, N_MOD * D_MODEL), 0.5 * D_MODEL ** -0.5)
    b_ada = nrm(ks[3], (DEPTH, N_MOD * D_MODEL), 0.02)
    norm_mix = 1.0 + nrm(ks[4], (DEPTH, D_MODEL), 0.02)
    w_in = nrm(ks[5], (DEPTH, D_MODEL, D_IN), D_MODEL ** -0.5)
    b_forget = jnp.linspace(1.0, 5.0, N_HEADS_B, dtype=f32)[None] + nrm(ks[6], (DEPTH, N_HEADS_B), 0.1)
    w_up_a = nrm(ks[7], (DEPTH, WIDTH_A_OUT, D_MODEL), WIDTH_A_OUT ** -0.5)
    w_up_b = nrm(ks[8], (DEPTH, WIDTH_B, D_MODEL), WIDTH_B ** -0.5)
    w_out = nrm(ks[9], (DEPTH, D_MODEL, D_MODEL), D_MODEL ** -0.5)
    norm_ffn = 1.0 + nrm(ks[10], (DEPTH, D_MODEL), 0.02)
    w_ffn_in = nrm(ks[11], (DEPTH, D_MODEL, 2 * D_FF), D_MODEL ** -0.5)
    w_ffn_out = nrm(ks[12], (DEPTH, D_FF, D_MODEL), D_FF ** -0.5)
    norm_final = 1.0 + nrm(ks[13], (D_MODEL,), 0.02)
    return {"x": x, "c": c, "w_ada": w_ada, "b_ada": b_ada, "norm_mix": norm_mix,
            "w_in": w_in, "b_forget": b_forget, "w_up_a": w_up_a, "w_up_b": w_up_b,
            "w_out": w_out, "norm_ffn": norm_ffn, "w_ffn_in": w_ffn_in,
            "w_ffn_out": w_ffn_out, "norm_final": norm_final}


def reference(x, c, w_ada, b_ada, norm_mix, w_in, b_forget, w_up_a, w_up_b, w_out,
              norm_ffn, w_ffn_in, w_ffn_out, norm_final):
    c_act = jax.nn.silu(c)
    for l in range(DEPTH):
        mod = c_act @ w_ada[l] + b_ada[l]
        sh1, sc1, g1, sh2, sc2, g2 = [m[:, None, :] for m in jnp.split(mod, N_MOD, axis=-1)]
        h = rms_norm(x, norm_mix[l]) * (1.0 + sc1) + sh1
        x = x + g1 * token_mixer(h, w_in[l], b_forget[l], w_up_a[l], w_up_b[l], w_out[l])
        h = rms_norm(x, norm_ffn[l]) * (1.0 + sc2) + sh2
        x = x + g2 * swiglu(h, w_ffn_in[l], w_ffn_out[l])
    return rms_norm(x, norm_final)
```

```python
import functools

import numpy as np
import jax
import jax.numpy as jnp
from jax import lax
from jax.experimental import pallas as pl
from jax.experimental.pallas import tpu as pltpu

F32 = jnp.float32
BF16 = jnp.bfloat16

D_MODEL = 1024
HEAD_DIM = 64
DIL_GROUPS = ((128, 1), (512, 4), (2048, 16))
HEADS_PER_GROUP = 4
N_GROUPS = len(DIL_GROUPS)
N_HEADS_A = HEADS_PER_GROUP * N_GROUPS
N_HEADS_B = 4
WIDTH_G = HEADS_PER_GROUP * HEAD_DIM
WIDTH_A = N_HEADS_A * HEAD_DIM
WIDTH_B = N_HEADS_B * HEAD_DIM
D_FF = -(-8 * D_MODEL // (3 * 256)) * 256
BLOCK = 128
ALIBI_MAX = 8.0
EPS = 1e-6
N_MOD = 6
QK_SCALE = HEAD_DIM ** -0.5

NEG = -1e30
LANES = 128
F_COLS = LANES
F_ROWS = 8
LSE_LANES = LANES // HEADS_PER_GROUP

VMEM_LIMIT = 56 * 1024 * 1024

TM = 512
FF_CHUNK = 256
OUT_CHUNK = 256
DECAY_CHUNK = 256
FOX_TQ = 512
FOX_TKV = 512
DIL_ROWS = 1024


def _params(n_axes):
    return pltpu.CompilerParams(dimension_semantics=("arbitrary",) * n_axes,
                                vmem_limit_bytes=VMEM_LIMIT)


def _resident(shape):
    zeros = (0,) * len(shape)
    return pl.BlockSpec(shape, lambda *_: zeros, pipeline_mode=pl.Buffered(1))


def _alibi_slopes():
    h = np.arange(1, N_HEADS_A + 1, dtype=np.float32)
    return np.asarray(2.0 ** (-ALIBI_MAX * h / N_HEADS_A), dtype=np.float32)


def _modulated_norm(x, g, sc, sh):
    y = x * lax.rsqrt(jnp.mean(x * x, axis=-1, keepdims=True) + EPS)
    return (y * g) * (1.0 + sc) + sh


def _dot(a, b):
    return jnp.dot(a, b, preferred_element_type=F32)


def _dot_nt(a, b):
    return lax.dot_general(a, b, (((1,), (1,)), ((), ())), preferred_element_type=F32)


def _mod_kernel(c_ref, w_ref, b_ref, o_ref):
    c = c_ref[...]
    c_act = (c * jax.nn.sigmoid(c)).astype(BF16)
    o_ref[0] = _dot(c_act, w_ref[0].astype(BF16)) + b_ref[0]


def _modulation(c, w_ada, b_ada):
    depth, d, n = w_ada.shape
    batch = c.shape[0]
    return pl.pallas_call(
        _mod_kernel,
        grid=(depth, n // d),
        in_specs=[pl.BlockSpec((batch, d), lambda l, j: (0, 0)),
                  pl.BlockSpec((1, d, d), lambda l, j: (l, 0, j)),
                  pl.BlockSpec((1, 1, d), lambda l, j: (l, 0, j))],
        out_specs=pl.BlockSpec((1, batch, d), lambda l, j: (l, 0, j)),
        out_shape=jax.ShapeDtypeStruct((depth, batch, n), F32),
        compiler_params=_params(2),
        name="adaln_modulation",
    )(c, w_ada, b_ada.reshape(depth, 1, n))


def _inproj_kernel(x_ref, g_ref, sc_ref, sh_ref, wa_ref, wb_ref,
                   za0_ref, za1_ref, za2_ref, zb_ref, fz_ref):
    h = _modulated_norm(x_ref[0], g_ref[...], sc_ref[0], sh_ref[0]).astype(BF16)
    wg = 3 * WIDTH_G
    for g, za_ref in enumerate((za0_ref, za1_ref, za2_ref)):
        za_ref[0] = _dot(h, wa_ref[:, g * wg:(g + 1) * wg]).astype(BF16)
    zb = _dot(h, wb_ref[...])
    zb_ref[0] = zb[:, :3 * WIDTH_B].astype(BF16)
    fz_ref[0] = zb[:, 3 * WIDTH_B:]


def _in_projection(x, g, sc, sh, wa, wb):
    batch, seq, d = x.shape
    wg = 3 * WIDTH_G
    tok = lambda w: pl.BlockSpec((1, TM, w), lambda b, i: (b, i, 0))
    per_batch = pl.BlockSpec((1, 1, d), lambda b, i: (b, 0, 0))
    return pl.pallas_call(
        _inproj_kernel,
        grid=(batch, seq // TM),
        in_specs=[tok(d), _resident((1, d)), per_batch, per_batch,
                  _resident(wa.shape), _resident(wb.shape)],
        out_specs=[tok(wg), tok(wg), tok(wg), tok(3 * WIDTH_B), tok(F_COLS)],
        out_shape=[jax.ShapeDtypeStruct((batch, seq, wg), BF16)] * N_GROUPS
        + [jax.ShapeDtypeStruct((batch, seq, 3 * WIDTH_B), BF16),
           jax.ShapeDtypeStruct((batch, seq, F_COLS), F32)],
        compiler_params=_params(2),
        name="in_projection",
    )(x, g, sc, sh, wa, wb)


def _log_sigmoid(x):
    return jnp.minimum(x, 0.0) - jnp.log1p(jnp.exp(-jnp.abs(x)))


def _decay_kernel(fz_ref, b_ref, tri_ref, o_ref):
    seq = fz_ref.shape[1]
    tri = tri_ref[...]
    carry = jnp.zeros((F_ROWS, 1), F32)
    for c in range(seq // DECAY_CHUNK):
        cols = slice(c * DECAY_CHUNK, (c + 1) * DECAY_CHUNK)
        logit_t = fz_ref[0, cols, :].T[:F_ROWS]
        lf = _log_sigmoid(logit_t + b_ref[...])
        hi = lf.astype(BF16)
        rest = lf - hi.astype(F32)
        mid = rest.astype(BF16)
        lo = (rest - mid.astype(F32)).astype(BF16)
        cs = _dot(hi, tri) + _dot(mid, tri) + _dot(lo, tri) + carry
        o_ref[0, :, cols] = cs
        carry = cs[:, DECAY_CHUNK - 1:DECAY_CHUNK]


def _decay_cumsum(fz, b_forget):
    batch, seq, _ = fz.shape
    bias = jnp.zeros((F_ROWS,), F32).at[:N_HEADS_B].set(b_forget.astype(F32))
    bias = jnp.broadcast_to(bias[:, None], (F_ROWS, DECAY_CHUNK))
    idx = np.arange(DECAY_CHUNK)
    tri = jnp.asarray(idx[:, None] <= idx[None, :], dtype=BF16)
    return pl.pallas_call(
        _decay_kernel,
        grid=(batch,),
        in_specs=[pl.BlockSpec((1, seq, F_COLS), lambda b: (b, 0, 0)),
                  _resident(bias.shape), _resident(tri.shape)],
        out_specs=pl.BlockSpec((1, F_ROWS, seq), lambda b: (b, 0, 0)),
        out_shape=jax.ShapeDtypeStruct((batch, F_ROWS, seq), F32),
        compiler_params=_params(1),
        name="forget_decay_cumsum",
    )(fz, bias, tri)


def _dilated_kernel(*refs, dilation, slopes, has_prev, is_last):
    refs = list(refs)
    cur_ref, prev_ref = refs[:2]
    refs = refs[2:]
    if has_prev:
        o_in_ref, lse_in_ref = refs[:2]
        refs = refs[2:]
    o_out_ref = refs[0]
    refs = refs[1:]
    if not is_last:
        lse_out_ref = refs[0]
        refs = refs[1:]
    kx_ref, vx_ref = refs

    rows = cur_ref.shape[1]
    nblk = rows // BLOCK
    step = pl.program_id(2)

    kx_ref[:BLOCK] = prev_ref[0, :, WIDTH_G:2 * WIDTH_G]
    kx_ref[BLOCK:] = cur_ref[0, :, WIDTH_G:2 * WIDTH_G]
    vx_ref[:BLOCK] = prev_ref[0, :, 2 * WIDTH_G:]
    vx_ref[BLOCK:] = cur_ref[0, :, 2 * WIDTH_G:]

    qi = lax.broadcasted_iota(jnp.int32, (BLOCK, 2 * BLOCK), 0)
    kj = lax.broadcasted_iota(jnp.int32, (BLOCK, 2 * BLOCK), 1)
    dist = qi + BLOCK - kj
    in_band = (dist >= 0) & (dist <= BLOCK)
    dist_tokens = (dist * dilation).astype(F32)

    def block_body(n, carry):
        row0 = pl.multiple_of(n * BLOCK, BLOCK)
        first_key = jnp.where(step * nblk + n == 0, BLOCK, 0)
        ok = in_band & (kj >= first_key)
        for h in range(HEADS_PER_GROUP):
            hs = slice(h * HEAD_DIM, (h + 1) * HEAD_DIM)
            q = cur_ref[0, pl.ds(row0, BLOCK), hs]
            k = kx_ref[pl.ds(row0, 2 * BLOCK), hs]
            v = vx_ref[pl.ds(row0, 2 * BLOCK), hs]
            logits = jnp.where(ok, _dot_nt(q, k) - slopes[h] * dist_tokens, NEG)
            m = jnp.max(logits, axis=-1, keepdims=True)
            p = jnp.exp(logits - m)
            l = jnp.sum(p, axis=-1, keepdims=True)
            o = _dot(p.astype(BF16), v) / l
            lse = m + jnp.log(l)
            if has_prev:
                ls = slice(h * LSE_LANES, h * LSE_LANES + 1)
                lse_p = lse_in_ref[0, pl.ds(row0, BLOCK), ls]
                o_p = o_in_ref[0, pl.ds(row0, BLOCK), hs]
                top = jnp.maximum(lse_p, lse)
                w_p = jnp.exp(lse_p - top)
                w_g = jnp.exp(lse - top)
                den = w_p + w_g
                o = (o_p * w_p + o * w_g) / den
                lse = top + jnp.log(den)
            o_out_ref[0, pl.ds(row0, BLOCK), hs] = o.astype(o_out_ref.dtype)
            if not is_last:
                lse_out_ref[0, pl.ds(row0, BLOCK), h * LSE_LANES:(h + 1) * LSE_LANES] = (
                    jnp.broadcast_to(lse, (BLOCK, LSE_LANES)))
        return carry

    lax.fori_loop(0, nblk, block_body, 0)


def _dilated_group(zg, state, *, dilation, slopes, is_last):
    batch, seq, wg = zg.shape
    sub = seq // dilation
    assert sub % BLOCK == 0
    rows = min(DIL_ROWS, sub)
    has_prev = state is not None
    blocks_per_step = rows // BLOCK

    def view(a):
        return a.reshape(batch, sub, dilation * a.shape[-1])

    def slab(w):
        return pl.BlockSpec((1, rows, w), lambda b, r, i: (b, i, r))

    prev_spec = pl.BlockSpec(
        (1, BLOCK, wg), lambda b, r, i: (b, jnp.maximum(i * blocks_per_step - 1, 0), r))
    in_specs = [slab(wg), prev_spec]
    operands = [view(zg), view(zg)]
    if has_prev:
        in_specs += [slab(WIDTH_G), slab(LANES)]
        operands += [view(state[0]), view(state[1])]
    out_specs = [slab(WIDTH_G)]
    out_shape = [jax.ShapeDtypeStruct((batch, sub, dilation * WIDTH_G), BF16 if is_last else F32)]
    if not is_last:
        out_specs.append(slab(LANES))
        out_shape.append(jax.ShapeDtypeStruct((batch, sub, dilation * LANES), F32))
    outs = pl.pallas_call(
        functools.partial(_dilated_kernel, dilation=dilation, slopes=slopes,
                          has_prev=has_prev, is_last=is_last),
        grid=(batch, dilation, sub // rows),
        in_specs=in_specs,
        out_specs=out_specs,
        out_shape=out_shape,
        scratch_shapes=[pltpu.VMEM((rows + BLOCK, WIDTH_G), BF16)] * 2,
        compiler_params=_params(3),
        name=f"dilated_attention_d{dilation}",
    )(*operands)
    return tuple(o.reshape(batch, seq, -1) for o in outs)


def _dilated_mixture(zgs):
    slopes = _alibi_slopes()
    state = None
    for g, (window, dilation) in enumerate(DIL_GROUPS):
        assert window // dilation == BLOCK
        hs = [float(s) for s in slopes[g * HEADS_PER_GROUP:(g + 1) * HEADS_PER_GROUP]]
        state = _dilated_group(zgs[g], state, dilation=dilation, slopes=hs,
                               is_last=g == N_GROUPS - 1)
    return state[0]


def _fox_kernel(q_ref, k_ref, v_ref, f_ref, o_ref):
    tq = q_ref.shape[1]
    qb = pl.program_id(1)
    q0 = qb * tq
    n_full = qb * (tq // FOX_TKV)
    row = lax.broadcasted_iota(jnp.int32, (tq, FOX_TKV), 0)
    col = lax.broadcasted_iota(jnp.int32, (tq, FOX_TKV), 1)

    for h in range(N_HEADS_B):
        hs = slice(h * HEAD_DIM, (h + 1) * HEAD_DIM)
        q = q_ref[0, :, hs]
        last = pl.multiple_of(q0 + tq - LANES, LANES)
        f_end = f_ref[0, h:h + 1, pl.ds(last, LANES)][:, LANES - 1:]

        def kv_block(j, carry, masked, q=q, hs=hs, h=h, f_end=f_end):
            m, l, acc = carry
            k0 = pl.multiple_of(j * FOX_TKV, FOX_TKV)
            k = k_ref[0, pl.ds(k0, FOX_TKV), hs]
            v = v_ref[0, pl.ds(k0, FOX_TKV), hs]
            s = _dot_nt(q, k) + (f_end - f_ref[0, h:h + 1, pl.ds(k0, FOX_TKV)])
            if masked:
                s = jnp.where(col + k0 <= row + q0, s, NEG)
            m_new = jnp.maximum(m, jnp.max(s, axis=-1, keepdims=True))
            alpha = jnp.exp(m - m_new)
            p = jnp.exp(s - m_new)
            l = alpha * l + jnp.sum(p, axis=-1, keepdims=True)
            acc = alpha * acc + _dot(p.astype(BF16), v)
            return m_new, l, acc

        carry = (jnp.full((tq, 1), NEG, F32), jnp.zeros((tq, 1), F32),
                 jnp.zeros((tq, HEAD_DIM), F32))
        carry = lax.fori_loop(0, n_full, functools.partial(kv_block, masked=False), carry)
        for d in range(tq // FOX_TKV):
            carry = kv_block(n_full + d, carry, masked=True)
        _, l, acc = carry
        o_ref[0, :, hs] = (acc / l).astype(o_ref.dtype)


def _forgetting_attention(zb, decay):
    batch, seq, _ = zb.shape
    return pl.pallas_call(
        _fox_kernel,
        grid=(batch, seq // FOX_TQ),
        in_specs=[pl.BlockSpec((1, FOX_TQ, WIDTH_B), lambda b, i: (b, i, 0)),
                  pl.BlockSpec((1, seq, WIDTH_B), lambda b, i: (b, 0, 1)),
                  pl.BlockSpec((1, seq, WIDTH_B), lambda b, i: (b, 0, 2)),
                  pl.BlockSpec((1, F_ROWS, seq), lambda b, i: (b, 0, 0))],
        out_specs=pl.BlockSpec((1, FOX_TQ, WIDTH_B), lambda b, i: (b, i, 0)),
        out_shape=jax.ShapeDtypeStruct((batch, seq, WIDTH_B), BF16),
        compiler_params=_params(2),
        name="forgetting_attention",
    )(zb, zb, zb, decay)


def _outproj_kernel(x_ref, g_ref, sc_ref, sh_ref, gate_ref, ya_ref, yb_ref,
                    wg_ref, wua_ref, wub_ref, wo_ref, o_ref, merged_ref):
    x = x_ref[0]
    d = x.shape[-1]
    h = _modulated_norm(x, g_ref[...], sc_ref[0], sh_ref[0]).astype(BF16)
    ya = ya_ref[0]
    yb = yb_ref[0]
    for c in range(d // OUT_CHUNK):
        cs = slice(c * OUT_CHUNK, (c + 1) * OUT_CHUNK)
        cs_b = slice(d + c * OUT_CHUNK, d + (c + 1) * OUT_CHUNK)
        gate_a = jax.nn.sigmoid(_dot(h, wg_ref[:, cs]))
        gate_b = jax.nn.sigmoid(_dot(h, wg_ref[:, cs_b]))
        merged = gate_a * _dot(ya, wua_ref[:, cs]) + gate_b * _dot(yb, wub_ref[:, cs])
        merged_ref[:, cs] = merged.astype(BF16)
    o_ref[0] = x + gate_ref[0] * _dot(merged_ref[...], wo_ref[...])


def _out_projection(x, g, sc, sh, gate, ya, yb, wg, wua, wub, wo):
    batch, seq, d = x.shape
    tok = lambda w: pl.BlockSpec((1, TM, w), lambda b, i: (b, i, 0))
    per_batch = pl.BlockSpec((1, 1, d), lambda b, i: (b, 0, 0))
    return pl.pallas_call(
        _outproj_kernel,
        grid=(batch, seq // TM),
        in_specs=[tok(d), _resident((1, d)), per_batch, per_batch, per_batch,
                  tok(WIDTH_G), tok(WIDTH_B),
                  _resident(wg.shape), _resident(wua.shape), _resident(wub.shape),
                  _resident(wo.shape)],
        out_specs=tok(d),
        out_shape=jax.ShapeDtypeStruct((batch, seq, d), F32),
        scratch_shapes=[pltpu.VMEM((TM, d), BF16)],
        compiler_params=_params(2),
        name="out_projection",
    )(x, g, sc, sh, gate, ya, yb, wg, wua, wub, wo)


def _ffn_kernel(*refs, final_norm):
    x_ref, g_ref, sc_ref, sh_ref, gate_ref, win_ref, wout_ref = refs[:7]
    refs = refs[7:]
    if final_norm:
        gf_ref = refs[0]
        refs = refs[1:]
    o_ref, h_ref, acc_ref = refs
    x = x_ref[0]
    h_ref[...] = _modulated_norm(x, g_ref[...], sc_ref[0], sh_ref[0]).astype(BF16)
    for c in range(D_FF // FF_CHUNK):
        cs = slice(c * FF_CHUNK, (c + 1) * FF_CHUNK)
        cs_up = slice(D_FF + c * FF_CHUNK, D_FF + (c + 1) * FF_CHUNK)
        h = h_ref[...]
        gt = _dot(h, win_ref[:, cs])
        up = _dot(h, win_ref[:, cs_up])
        act = ((gt * jax.nn.sigmoid(gt)) * up).astype(BF16)
        part = _dot(act, wout_ref[cs, :])
        if c == 0:
            acc_ref[...] = part
        else:
            acc_ref[...] += part
    y = x + gate_ref[0] * acc_ref[...]
    if final_norm:
        y = (y * lax.rsqrt(jnp.mean(y * y, axis=-1, keepdims=True) + EPS)) * gf_ref[...]
    o_ref[0] = y


def _ffn(x, g, sc, sh, gate, win, wout, g_final):
    batch, seq, d = x.shape
    final_norm = g_final is not None
    tok = pl.BlockSpec((1, TM, d), lambda b, i: (b, i, 0))
    per_batch = pl.BlockSpec((1, 1, d), lambda b, i: (b, 0, 0))
    in_specs = [tok, _resident((1, d)), per_batch, per_batch, per_batch,
                _resident(win.shape), _resident(wout.shape)]
    operands = [x, g, sc, sh, gate, win, wout]
    if final_norm:
        in_specs.append(_resident((1, d)))
        operands.append(g_final)
    return pl.pallas_call(
        functools.partial(_ffn_kernel, final_norm=final_norm),
        grid=(batch, seq // TM),
        in_specs=in_specs,
        out_specs=tok,
        out_shape=jax.ShapeDtypeStruct((batch, seq, d), F32),
        scratch_shapes=[pltpu.VMEM((TM, d), BF16), pltpu.VMEM((TM, d), F32)],
        compiler_params=_params(2),
        name="swiglu_ffn",
    )(*operands)


def _split_w_in(w_in):
    qa, ka, va, qb, kb, vb, fz, gza, gzb = jnp.split(
        w_in, np.cumsum((WIDTH_A, WIDTH_A, WIDTH_A, WIDTH_B, WIDTH_B, WIDTH_B, N_HEADS_B,
                         D_MODEL))[:].tolist(), axis=-1)
    cols = []
    for g in range(N_GROUPS):
        gs = slice(g * WIDTH_G, (g + 1) * WIDTH_G)
        cols += [qa[:, gs] * QK_SCALE, ka[:, gs], va[:, gs]]
    wa = jnp.concatenate(cols, axis=-1).astype(BF16)
    pad = jnp.zeros((w_in.shape[0], F_COLS - N_HEADS_B), w_in.dtype)
    wb = jnp.concatenate([qb * QK_SCALE, kb, vb, fz, pad], axis=-1).astype(BF16)
    wg = jnp.concatenate([gza, gzb], axis=-1).astype(BF16)
    return wa, wb, wg


def kernel(x, c, w_ada, b_ada, norm_mix, w_in, b_forget, w_up_a, w_up_b, w_out,
           norm_ffn, w_ffn_in, w_ffn_out, norm_final):
    depth = w_ada.shape[0]
    batch, seq, d = x.shape
    assert seq % TM == 0 and seq % FOX_TQ == 0 and FOX_TQ % FOX_TKV == 0
    mod = _modulation(c, w_ada, b_ada).reshape(depth, batch, N_MOD, 1, d)
    for l in range(depth):
        sh1, sc1, g1, sh2, sc2, g2 = (mod[l, :, j] for j in range(N_MOD))
        wa, wb, wg = _split_w_in(w_in[l])
        g_mix = norm_mix[l].reshape(1, d)
        za0, za1, za2, zb, fz = _in_projection(x, g_mix, sc1, sh1, wa, wb)
        decay = _decay_cumsum(fz, b_forget[l])
        ya = _dilated_mixture((za0, za1, za2))
        yb = _forgetting_attention(zb, decay)
        x = _out_projection(x, g_mix, sc1, sh1, g1, ya, yb, wg,
                            w_up_a[l].astype(BF16), w_up_b[l].astype(BF16),
                            w_out[l].astype(BF16))
        g_final = norm_final.reshape(1, d) if l == depth - 1 else None
        x = _ffn(x, norm_ffn[l].reshape(1, d), sc2, sh2, g2,
                 w_ffn_in[l].astype(BF16), w_ffn_out[l].astype(BF16), g_final)
    return x
```

```python
import functools

import numpy as np
import jax
import jax.numpy as jnp
from jax import lax
from jax.experimental import pallas as pl
from jax.experimental.pallas import tpu as pltpu

F32 = jnp.float32
BF16 = jnp.bfloat16

D_MODEL = 1024
HEAD_DIM = 64
DIL_GROUPS = ((128, 1), (512, 4), (2048, 16))
HEADS_PER_GROUP = 4
N_GROUPS = len(DIL_GROUPS)
N_HEADS_A = HEADS_PER_GROUP * N_GROUPS
N_HEADS_B = 4
WIDTH_G = HEADS_PER_GROUP * HEAD_DIM
WIDTH_A = N_HEADS_A * HEAD_DIM
WIDTH_B = N_HEADS_B * HEAD_DIM
D_FF = -(-8 * D_MODEL // (3 * 256)) * 256
BLOCK = 128
ALIBI_MAX = 8.0
EPS = 1e-6
N_MOD = 6
QK_SCALE = HEAD_DIM ** -0.5

NEG = -1e30
LANES = 128
F_COLS = LANES
V_ROWS = 2 * HEAD_DIM
LSE_LANES = LANES // HEADS_PER_GROUP

VMEM_LIMIT = 56 * 1024 * 1024

TM = 512
FF_CHUNK = 256
OUT_CHUNK = 256
DECAY_CHUNK = 256
FOX_TQ = 512
FOX_TKV = 512
DIL_ROWS = 1024


def _params(n_axes):
    return pltpu.CompilerParams(dimension_semantics=("arbitrary",) * n_axes,
                                vmem_limit_bytes=VMEM_LIMIT)


def _resident(shape):
    zeros = (0,) * len(shape)
    return pl.BlockSpec(shape, lambda *_: zeros, pipeline_mode=pl.Buffered(1))


def _alibi_slopes():
    h = np.arange(1, N_HEADS_A + 1, dtype=np.float32)
    return np.asarray(2.0 ** (-ALIBI_MAX * h / N_HEADS_A), dtype=np.float32)


def _modulated_norm(x, g, sc, sh):
    y = x * lax.rsqrt(jnp.mean(x * x, axis=-1, keepdims=True) + EPS)
    return (y * g) * (1.0 + sc) + sh


def _dot(a, b):
    return jnp.dot(a, b, preferred_element_type=F32)


def _dot_nt(a, b):
    return lax.dot_general(a, b, (((1,), (1,)), ((), ())), preferred_element_type=F32)


def _mod_kernel(c_ref, w_ref, b_ref, o_ref):
    c = c_ref[...]
    c_act = (c * jax.nn.sigmoid(c)).astype(BF16)
    o_ref[0] = _dot(c_act, w_ref[0].astype(BF16)) + b_ref[0]


def _modulation(c, w_ada, b_ada):
    depth, d, n = w_ada.shape
    batch = c.shape[0]
    return pl.pallas_call(
        _mod_kernel,
        grid=(depth, n // d),
        in_specs=[pl.BlockSpec((batch, d), lambda l, j: (0, 0)),
                  pl.BlockSpec((1, d, d), lambda l, j: (l, 0, j)),
                  pl.BlockSpec((1, 1, d), lambda l, j: (l, 0, j))],
        out_specs=pl.BlockSpec((1, batch, d), lambda l, j: (l, 0, j)),
        out_shape=jax.ShapeDtypeStruct((depth, batch, n), F32),
        compiler_params=_params(2),
        name="adaln_modulation",
    )(c, w_ada, b_ada.reshape(depth, 1, n))


def _inproj_kernel(x_ref, g_ref, sc_ref, sh_ref, wa_ref, wqt_ref, wvt_ref, wkf_ref,
                   za0_ref, za1_ref, za2_ref, qt_ref, k_ref, vt_ref, fz_ref):
    h = _modulated_norm(x_ref[0], g_ref[...], sc_ref[0], sh_ref[0]).astype(BF16)
    tm = h.shape[0]
    wg = 3 * WIDTH_G
    for g, za_ref in enumerate((za0_ref, za1_ref, za2_ref)):
        za_ref[0] = _dot(h, wa_ref[:, g * wg:(g + 1) * wg]).astype(BF16)
    qt_ref[0] = _dot_nt(wqt_ref[...], h).astype(BF16)
    vt = _dot_nt(wvt_ref[...], h).astype(BF16)
    ones_row = (lax.broadcasted_iota(jnp.int32, (HEAD_DIM, tm), 0) == 0).astype(BF16)
    for hd in range(N_HEADS_B):
        vt_ref[0, hd * V_ROWS:hd * V_ROWS + HEAD_DIM, :] = vt[hd * HEAD_DIM:(hd + 1) * HEAD_DIM]
        vt_ref[0, hd * V_ROWS + HEAD_DIM:(hd + 1) * V_ROWS, :] = ones_row
    kf = _dot(h, wkf_ref[...])
    k_ref[0] = kf[:, :WIDTH_B].astype(BF16)
    fz_ref[0] = kf[:, WIDTH_B:]


def _in_projection(x, g, sc, sh, wa, wqt, wvt, wkf):
    batch, seq, d = x.shape
    wg = 3 * WIDTH_G
    tok = lambda w: pl.BlockSpec((1, TM, w), lambda b, i: (b, i, 0))
    tok_t = lambda rows: pl.BlockSpec((1, rows, TM), lambda b, i: (b, 0, i))
    per_batch = pl.BlockSpec((1, 1, d), lambda b, i: (b, 0, 0))
    return pl.pallas_call(
        _inproj_kernel,
        grid=(batch, seq // TM),
        in_specs=[tok(d), _resident((1, d)), per_batch, per_batch,
                  _resident(wa.shape), _resident(wqt.shape), _resident(wvt.shape),
                  _resident(wkf.shape)],
        out_specs=[tok(wg), tok(wg), tok(wg), tok_t(WIDTH_B), tok(WIDTH_B),
                   tok_t(N_HEADS_B * V_ROWS), tok(F_COLS)],
        out_shape=[jax.ShapeDtypeStruct((batch, seq, wg), BF16)] * N_GROUPS
        + [jax.ShapeDtypeStruct((batch, WIDTH_B, seq), BF16),
           jax.ShapeDtypeStruct((batch, seq, WIDTH_B), BF16),
           jax.ShapeDtypeStruct((batch, N_HEADS_B * V_ROWS, seq), BF16),
           jax.ShapeDtypeStruct((batch, seq, F_COLS), F32)],
        compiler_params=_params(2),
        name="in_projection",
    )(x, g, sc, sh, wa, wqt, wvt, wkf)


def _log_sigmoid(x):
    return jnp.minimum(x, 0.0) - jnp.log1p(jnp.exp(-jnp.abs(x)))


def _decay_kernel(fz_ref, b_ref, tri_ref, o_ref):
    seq = fz_ref.shape[1]
    tri = tri_ref[...]
    carry = jnp.zeros((1, F_COLS), F32)
    for c in range(seq // DECAY_CHUNK):
        rows = slice(c * DECAY_CHUNK, (c + 1) * DECAY_CHUNK)
        lf = _log_sigmoid(fz_ref[0, rows, :] + b_ref[...])
        hi = lf.astype(BF16)
        rest = lf - hi.astype(F32)
        mid = rest.astype(BF16)
        lo = (rest - mid.astype(F32)).astype(BF16)
        cs = _dot(tri, hi) + _dot(tri, mid) + _dot(tri, lo) + carry
        o_ref[0, rows, :] = cs
        carry = cs[DECAY_CHUNK - 1:DECAY_CHUNK, :]


def _decay_cumsum(fz, b_forget):
    batch, seq, _ = fz.shape
    bias = jnp.zeros((1, F_COLS), F32).at[0, :N_HEADS_B].set(b_forget.astype(F32))
    idx = np.arange(DECAY_CHUNK)
    tri = jnp.asarray(idx[None, :] <= idx[:, None], dtype=BF16)
    return pl.pallas_call(
        _decay_kernel,
        grid=(batch,),
        in_specs=[pl.BlockSpec((1, seq, F_COLS), lambda b: (b, 0, 0)),
                  _resident(bias.shape), _resident(tri.shape)],
        out_specs=pl.BlockSpec((1, seq, F_COLS), lambda b: (b, 0, 0)),
        out_shape=jax.ShapeDtypeStruct((batch, seq, F_COLS), F32),
        compiler_params=_params(1),
        name="forget_decay_cumsum",
    )(fz, bias, tri)


def _dilated_kernel(*refs, dilation, slopes, has_prev, is_last):
    refs = list(refs)
    cur_ref, prev_ref = refs[:2]
    refs = refs[2:]
    if has_prev:
        o_in_ref, lse_in_ref = refs[:2]
        refs = refs[2:]
    o_out_ref = refs[0]
    refs = refs[1:]
    if not is_last:
        lse_out_ref = refs[0]
        refs = refs[1:]
    kx_ref, vx_ref = refs

    rows = cur_ref.shape[1]
    nblk = rows // BLOCK
    step = pl.program_id(2)

    kx_ref[:BLOCK] = prev_ref[0, :, WIDTH_G:2 * WIDTH_G]
    kx_ref[BLOCK:] = cur_ref[0, :, WIDTH_G:2 * WIDTH_G]
    vx_ref[:BLOCK] = prev_ref[0, :, 2 * WIDTH_G:]
    vx_ref[BLOCK:] = cur_ref[0, :, 2 * WIDTH_G:]

    qi = lax.broadcasted_iota(jnp.int32, (BLOCK, 2 * BLOCK), 0)
    kj = lax.broadcasted_iota(jnp.int32, (BLOCK, 2 * BLOCK), 1)
    dist = qi + BLOCK - kj
    in_band = (dist >= 0) & (dist <= BLOCK)
    dist_tokens = (dist * dilation).astype(F32)

    def block_body(n, carry):
        row0 = pl.multiple_of(n * BLOCK, BLOCK)
        first_key = jnp.where(step * nblk + n == 0, BLOCK, 0)
        ok = in_band & (kj >= first_key)
        for h in range(HEADS_PER_GROUP):
            hs = slice(h * HEAD_DIM, (h + 1) * HEAD_DIM)
            q = cur_ref[0, pl.ds(row0, BLOCK), hs]
            k = kx_ref[pl.ds(row0, 2 * BLOCK), hs]
            v = vx_ref[pl.ds(row0, 2 * BLOCK), hs]
            logits = jnp.where(ok, _dot_nt(q, k) - slopes[h] * dist_tokens, NEG)
            m = jnp.max(logits, axis=-1, keepdims=True)
            p = jnp.exp(logits - m)
            l = jnp.sum(p, axis=-1, keepdims=True)
            o = _dot(p.astype(BF16), v) / l
            lse = m + jnp.log(l)
            if has_prev:
                ls = slice(h * LSE_LANES, h * LSE_LANES + 1)
                lse_p = lse_in_ref[0, pl.ds(row0, BLOCK), ls]
                o_p = o_in_ref[0, pl.ds(row0, BLOCK), hs]
                top = jnp.maximum(lse_p, lse)
                w_p = jnp.exp(lse_p - top)
                w_g = jnp.exp(lse - top)
                den = w_p + w_g
                o = (o_p * w_p + o * w_g) / den
                lse = top + jnp.log(den)
            o_out_ref[0, pl.ds(row0, BLOCK), hs] = o.astype(o_out_ref.dtype)
            if not is_last:
                lse_out_ref[0, pl.ds(row0, BLOCK), h * LSE_LANES:(h + 1) * LSE_LANES] = (
                    jnp.broadcast_to(lse, (BLOCK, LSE_LANES)))
        return carry

    lax.fori_loop(0, nblk, block_body, 0)


def _dilated_group(zg, state, *, dilation, slopes, is_last):
    batch, seq, wg = zg.shape
    sub = seq // dilation
    assert sub % BLOCK == 0
    rows = min(DIL_ROWS, sub)
    has_prev = state is not None
    blocks_per_step = rows // BLOCK

    def view(a):
        return a.reshape(batch, sub, dilation * a.shape[-1])

    def slab(w):
        return pl.BlockSpec((1, rows, w), lambda b, r, i: (b, i, r))

    prev_spec = pl.BlockSpec(
        (1, BLOCK, wg), lambda b, r, i: (b, jnp.maximum(i * blocks_per_step - 1, 0), r))
    in_specs = [slab(wg), prev_spec]
    operands = [view(zg), view(zg)]
    if has_prev:
        in_specs += [slab(WIDTH_G), slab(LANES)]
        operands += [view(state[0]), view(state[1])]
    out_specs = [slab(WIDTH_G)]
    out_shape = [jax.ShapeDtypeStruct((batch, sub, dilation * WIDTH_G), BF16 if is_last else F32)]
    if not is_last:
        out_specs.append(slab(LANES))
        out_shape.append(jax.ShapeDtypeStruct((batch, sub, dilation * LANES), F32))
    outs = pl.pallas_call(
        functools.partial(_dilated_kernel, dilation=dilation, slopes=slopes,
                          has_prev=has_prev, is_last=is_last),
        grid=(batch, dilation, sub // rows),
        in_specs=in_specs,
        out_specs=out_specs,
        out_shape=out_shape,
        scratch_shapes=[pltpu.VMEM((rows + BLOCK, WIDTH_G), BF16)] * 2,
        compiler_params=_params(3),
        name=f"dilated_attention_d{dilation}",
    )(*operands)
    return tuple(o.reshape(batch, seq, -1) for o in outs)


def _dilated_mixture(zgs):
    slopes = _alibi_slopes()
    state = None
    for g, (window, dilation) in enumerate(DIL_GROUPS):
        assert window // dilation == BLOCK
        hs = [float(s) for s in slopes[g * HEADS_PER_GROUP:(g + 1) * HEADS_PER_GROUP]]
        state = _dilated_group(zgs[g], state, dilation=dilation, slopes=hs,
                               is_last=g == N_GROUPS - 1)
    return state[0]


def _fox_kernel(qt_ref, k_ref, vt_ref, f_ref, o_ref, acc_ref):
    tq = qt_ref.shape[2]
    qb = pl.program_id(1)
    q0 = qb * tq
    n_full = qb * (tq // FOX_TKV)
    key = lax.broadcasted_iota(jnp.int32, (FOX_TKV, tq), 0)
    qry = lax.broadcasted_iota(jnp.int32, (FOX_TKV, tq), 1)
    f_end = f_ref[0, pl.ds(pl.multiple_of(q0 + tq - 8, 8), 8), :][7:8]
    acc_ref[...] = jnp.zeros_like(acc_ref)

    def kv_block(j, ms, masked):
        k0 = pl.multiple_of(j * FOX_TKV, FOX_TKV)
        decay = f_end - f_ref[0, pl.ds(k0, FOX_TKV), :]
        new_ms = []
        for h in range(N_HEADS_B):
            hs = slice(h * HEAD_DIM, (h + 1) * HEAD_DIM)
            s = _dot(k_ref[0, pl.ds(k0, FOX_TKV), hs], qt_ref[0, hs, :]) + decay[:, h:h + 1]
            if masked:
                s = jnp.where(key + k0 <= qry + q0, s, NEG)
            m_new = jnp.maximum(ms[h], jnp.max(s, axis=0, keepdims=True))
            alpha = jnp.exp(ms[h] - m_new)
            p = jnp.exp(s - m_new).astype(BF16)
            vt = vt_ref[0, h * V_ROWS:(h + 1) * V_ROWS, pl.ds(k0, FOX_TKV)]
            acc_ref[h] = acc_ref[h] * alpha + _dot(vt, p)
            new_ms.append(m_new)
        return tuple(new_ms)

    ms = tuple(jnp.full((1, tq), NEG, F32) for _ in range(N_HEADS_B))
    ms = lax.fori_loop(0, n_full, functools.partial(kv_block, masked=False), ms)
    for d in range(tq // FOX_TKV):
        ms = kv_block(n_full + d, ms, masked=True)
    for h in range(N_HEADS_B):
        acc = acc_ref[h]
        out_t = acc / acc[HEAD_DIM:HEAD_DIM + 1]
        o_ref[0, :, h * HEAD_DIM:(h + 1) * HEAD_DIM] = (
            out_t.T[:, :HEAD_DIM].astype(o_ref.dtype))


def _forgetting_attention(qt, k, vt, decay):
    batch, seq, _ = k.shape
    return pl.pallas_call(
        _fox_kernel,
        grid=(batch, seq // FOX_TQ),
        in_specs=[pl.BlockSpec((1, WIDTH_B, FOX_TQ), lambda b, i: (b, 0, i)),
                  pl.BlockSpec((1, seq, WIDTH_B), lambda b, i: (b, 0, 0)),
                  pl.BlockSpec((1, N_HEADS_B * V_ROWS, seq), lambda b, i: (b, 0, 0)),
                  pl.BlockSpec((1, seq, F_COLS), lambda b, i: (b, 0, 0))],
        out_specs=pl.BlockSpec((1, FOX_TQ, WIDTH_B), lambda b, i: (b, i, 0)),
        out_shape=jax.ShapeDtypeStruct((batch, seq, WIDTH_B), BF16),
        scratch_shapes=[pltpu.VMEM((N_HEADS_B, V_ROWS, FOX_TQ), F32)],
        compiler_params=_params(2),
        name="forgetting_attention",
    )(qt, k, vt, decay)


def _outproj_kernel(x_ref, g_ref, sc_ref, sh_ref, gate_ref, ya_ref, yb_ref,
                    wg_ref, wua_ref, wub_ref, wo_ref, o_ref, merged_ref):
    x = x_ref[0]
    d = x.shape[-1]
    h = _modulated_norm(x, g_ref[...], sc_ref[0], sh_ref[0]).astype(BF16)
    ya = ya_ref[0]
    yb = yb_ref[0]
    for c in range(d // OUT_CHUNK):
        cs = slice(c * OUT_CHUNK, (c + 1) * OUT_CHUNK)
        cs_b = slice(d + c * OUT_CHUNK, d + (c + 1) * OUT_CHUNK)
        gate_a = jax.nn.sigmoid(_dot(h, wg_ref[:, cs]))
        gate_b = jax.nn.sigmoid(_dot(h, wg_ref[:, cs_b]))
        merged = gate_a * _dot(ya, wua_ref[:, cs]) + gate_b * _dot(yb, wub_ref[:, cs])
        merged_ref[:, cs] = merged.astype(BF16)
    o_ref[0] = x + gate_ref[0] * _dot(merged_ref[...], wo_ref[...])


def _out_projection(x, g, sc, sh, gate, ya, yb, wg, wua, wub, wo):
    batch, seq, d = x.shape
    tok = lambda w: pl.BlockSpec((1, TM, w), lambda b, i: (b, i, 0))
    per_batch = pl.BlockSpec((1, 1, d), lambda b, i: (b, 0, 0))
    return pl.pallas_call(
        _outproj_kernel,
        grid=(batch, seq // TM),
        in_specs=[tok(d), _resident((1, d)), per_batch, per_batch, per_batch,
                  tok(WIDTH_G), tok(WIDTH_B),
                  _resident(wg.shape), _resident(wua.shape), _resident(wub.shape),
                  _resident(wo.shape)],
        out_specs=tok(d),
        out_shape=jax.ShapeDtypeStruct((batch, seq, d), F32),
        scratch_shapes=[pltpu.VMEM((TM, d), BF16)],
        compiler_params=_params(2),
        name="out_projection",
    )(x, g, sc, sh, gate, ya, yb, wg, wua, wub, wo)


def _ffn_kernel(*refs, final_norm):
    x_ref, g_ref, sc_ref, sh_ref, gate_ref, win_ref, wout_ref = refs[:7]
    refs = refs[7:]
    if final_norm:
        gf_ref = refs[0]
        refs = refs[1:]
    o_ref, h_ref, acc_ref = refs
    x = x_ref[0]
    h_ref[...] = _modulated_norm(x, g_ref[...], sc_ref[0], sh_ref[0]).astype(BF16)
    for c in range(D_FF // FF_CHUNK):
        cs = slice(c * FF_CHUNK, (c + 1) * FF_CHUNK)
        cs_up = slice(D_FF + c * FF_CHUNK, D_FF + (c + 1) * FF_CHUNK)
        h = h_ref[...]
        gt = _dot(h, win_ref[:, cs])
        up = _dot(h, win_ref[:, cs_up])
        act = ((gt * jax.nn.sigmoid(gt)) * up).astype(BF16)
        part = _dot(act, wout_ref[cs, :])
        if c == 0:
            acc_ref[...] = part
        else:
            acc_ref[...] += part
    y = x + gate_ref[0] * acc_ref[...]
    if final_norm:
        y = (y * lax.rsqrt(jnp.mean(y * y, axis=-1, keepdims=True) + EPS)) * gf_ref[...]
    o_ref[0] = y


def _ffn(x, g, sc, sh, gate, win, wout, g_final):
    batch, seq, d = x.shape
    final_norm = g_final is not None
    tok = pl.BlockSpec((1, TM, d), lambda b, i: (b, i, 0))
    per_batch = pl.BlockSpec((1, 1, d), lambda b, i: (b, 0, 0))
    in_specs = [tok, _resident((1, d)), per_batch, per_batch, per_batch,
                _resident(win.shape), _resident(wout.shape)]
    operands = [x, g, sc, sh, gate, win, wout]
    if final_norm:
        in_specs.append(_resident((1, d)))
        operands.append(g_final)
    return pl.pallas_call(
        functools.partial(_ffn_kernel, final_norm=final_norm),
        grid=(batch, seq // TM),
        in_specs=in_specs,
        out_specs=tok,
        out_shape=jax.ShapeDtypeStruct((batch, seq, d), F32),
        scratch_shapes=[pltpu.VMEM((TM, d), BF16), pltpu.VMEM((TM, d), F32)],
        compiler_params=_params(2),
        name="swiglu_ffn",
    )(*operands)


def _split_w_in(w_in):
    qa, ka, va, qb, kb, vb, fz, gza, gzb = jnp.split(
        w_in, np.cumsum((WIDTH_A, WIDTH_A, WIDTH_A, WIDTH_B, WIDTH_B, WIDTH_B, N_HEADS_B,
                         D_MODEL))[:].tolist(), axis=-1)
    cols = []
    for g in range(N_GROUPS):
        gs = slice(g * WIDTH_G, (g + 1) * WIDTH_G)
        cols += [qa[:, gs] * QK_SCALE, ka[:, gs], va[:, gs]]
    wa = jnp.concatenate(cols, axis=-1).astype(BF16)
    pad = jnp.zeros((w_in.shape[0], F_COLS - N_HEADS_B), w_in.dtype)
    wqt = (qb * QK_SCALE).T.astype(BF16)
    wvt = vb.T.astype(BF16)
    wkf = jnp.concatenate([kb, fz, pad], axis=-1).astype(BF16)
    wg = jnp.concatenate([gza, gzb], axis=-1).astype(BF16)
    return wa, wqt, wvt, wkf, wg


def kernel(x, c, w_ada, b_ada, norm_mix, w_in, b_forget, w_up_a, w_up_b, w_out,
           norm_ffn, w_ffn_in, w_ffn_out, norm_final):
    depth = w_ada.shape[0]
    batch, seq, d = x.shape
    assert seq % TM == 0 and seq % FOX_TQ == 0 and FOX_TQ % FOX_TKV == 0
    mod = _modulation(c, w_ada, b_ada).reshape(depth, batch, N_MOD, 1, d)
    for l in range(depth):
        sh1, sc1, g1, sh2, sc2, g2 = (mod[l, :, j] for j in range(N_MOD))
        wa, wqt, wvt, wkf, wg = _split_w_in(w_in[l])
        g_mix = norm_mix[l].reshape(1, d)
        za0, za1, za2, qt, k, vt, fz = _in_projection(x, g_mix, sc1, sh1, wa, wqt, wvt, wkf)
        decay = _decay_cumsum(fz, b_forget[l])
        ya = _dilated_mixture((za0, za1, za2))
        yb = _forgetting_attention(qt, k, vt, decay)
        x = _out_projection(x, g_mix, sc1, sh1, g1, ya, yb, wg,
                            w_up_a[l].astype(BF16), w_up_b[l].astype(BF16),
                            w_out[l].astype(BF16))
        g_final = norm_final.reshape(1, d) if l == depth - 1 else None
        x = _ffn(x, norm_ffn[l].reshape(1, d), sc2, sh2, g2,
                 w_ffn_in[l].astype(BF16), w_ffn_out[l].astype(BF16), g_final)
    return x
```

```python
import functools

import numpy as np
import jax
import jax.numpy as jnp
from jax import lax
from jax.experimental import pallas as pl
from jax.experimental.pallas import tpu as pltpu

F32 = jnp.float32
BF16 = jnp.bfloat16

D_MODEL = 1024
HEAD_DIM = 64
DIL_GROUPS = ((128, 1), (512, 4), (2048, 16))
HEADS_PER_GROUP = 4
N_GROUPS = len(DIL_GROUPS)
N_HEADS_A = HEADS_PER_GROUP * N_GROUPS
N_HEADS_B = 4
WIDTH_G = HEADS_PER_GROUP * HEAD_DIM
WIDTH_A = N_HEADS_A * HEAD_DIM
WIDTH_B = N_HEADS_B * HEAD_DIM
D_FF = -(-8 * D_MODEL // (3 * 256)) * 256
BLOCK = 128
ALIBI_MAX = 8.0
EPS = 1e-6
N_MOD = 6
QK_SCALE = HEAD_DIM ** -0.5

NEG = -1e30
LANES = 128
F_COLS = LANES
V_ROWS = 2 * HEAD_DIM
N_STAT = HEADS_PER_GROUP

VMEM_LIMIT = 56 * 1024 * 1024

TM = 512
FF_CHUNK = 256
OUT_CHUNK = 256
DECAY_CHUNK = 256
FOX_TQ = 512
FOX_TKV = 512
DIL_SPAN = BLOCK * max(d for _, d in DIL_GROUPS)
DIL_BLOCKS_IN_FLIGHT = 4


def _params(n_axes):
    return pltpu.CompilerParams(dimension_semantics=("arbitrary",) * n_axes,
                                vmem_limit_bytes=VMEM_LIMIT)


def _resident(shape):
    zeros = (0,) * len(shape)
    return pl.BlockSpec(shape, lambda *_: zeros, pipeline_mode=pl.Buffered(1))


def _alibi_slopes():
    h = np.arange(1, N_HEADS_A + 1, dtype=np.float32)
    return np.asarray(2.0 ** (-ALIBI_MAX * h / N_HEADS_A), dtype=np.float32)


def _modulated_norm(x, g, sc, sh):
    y = x * lax.rsqrt(jnp.mean(x * x, axis=-1, keepdims=True) + EPS)
    return (y * g) * (1.0 + sc) + sh


def _dot(a, b):
    return jnp.dot(a, b, preferred_element_type=F32)


def _dot_nt(a, b):
    return lax.dot_general(a, b, (((1,), (1,)), ((), ())), preferred_element_type=F32)


def _mod_kernel(c_ref, w_ref, b_ref, o_ref):
    c = c_ref[...]
    c_act = (c * jax.nn.sigmoid(c)).astype(BF16)
    o_ref[0] = _dot(c_act, w_ref[0].astype(BF16)) + b_ref[0]


def _modulation(c, w_ada, b_ada):
    depth, d, n = w_ada.shape
    batch = c.shape[0]
    return pl.pallas_call(
        _mod_kernel,
        grid=(depth, n // d),
        in_specs=[pl.BlockSpec((batch, d), lambda l, j: (0, 0)),
                  pl.BlockSpec((1, d, d), lambda l, j: (l, 0, j)),
                  pl.BlockSpec((1, 1, d), lambda l, j: (l, 0, j))],
        out_specs=pl.BlockSpec((1, batch, d), lambda l, j: (l, 0, j)),
        out_shape=jax.ShapeDtypeStruct((depth, batch, n), F32),
        compiler_params=_params(2),
        name="adaln_modulation",
    )(c, w_ada, b_ada.reshape(depth, 1, n))


def _inproj_kernel(x_ref, g_ref, sc_ref, sh_ref, wa_ref, wqt_ref, wvt_ref, wkf_ref,
                   za0_ref, za1_ref, za2_ref, qt_ref, k_ref, vt_ref, fz_ref, stage_ref):
    h = _modulated_norm(x_ref[0], g_ref[...], sc_ref[0], sh_ref[0]).astype(BF16)
    tm = h.shape[0]
    wg = 3 * WIDTH_G
    for g, za_ref in enumerate((za0_ref, za1_ref, za2_ref)):
        dilation = DIL_GROUPS[g][1]
        za = _dot(h, wa_ref[:, g * wg:(g + 1) * wg])
        if dilation == 1:
            za_ref[0, 0] = za.astype(BF16)
            continue
        for c in range(wg // LANES):
            stage_ref[c] = za[:, c * LANES:(c + 1) * LANES]
        for r in range(dilation):
            rows = pl.ds(r, tm // dilation, stride=dilation)
            for c in range(wg // LANES):
                za_ref[0, r, :, c * LANES:(c + 1) * LANES] = stage_ref[c, rows, :].astype(BF16)
    qt_ref[0] = _dot_nt(wqt_ref[...], h).astype(BF16)
    vt = _dot_nt(wvt_ref[...], h).astype(BF16)
    ones_row = (lax.broadcasted_iota(jnp.int32, (HEAD_DIM, tm), 0) == 0).astype(BF16)
    for hd in range(N_HEADS_B):
        vt_ref[0, hd * V_ROWS:hd * V_ROWS + HEAD_DIM, :] = vt[hd * HEAD_DIM:(hd + 1) * HEAD_DIM]
        vt_ref[0, hd * V_ROWS + HEAD_DIM:(hd + 1) * V_ROWS, :] = ones_row
    kf = _dot(h, wkf_ref[...])
    k_ref[0] = kf[:, :WIDTH_B].astype(BF16)
    fz_ref[0] = kf[:, WIDTH_B:]


def _in_projection(x, g, sc, sh, wa, wqt, wvt, wkf):
    batch, seq, d = x.shape
    wg = 3 * WIDTH_G
    tok = lambda w: pl.BlockSpec((1, TM, w), lambda b, i: (b, i, 0))
    tok_t = lambda rows: pl.BlockSpec((1, rows, TM), lambda b, i: (b, 0, i))
    per_batch = pl.BlockSpec((1, 1, d), lambda b, i: (b, 0, 0))
    dils = [dilation for _, dilation in DIL_GROUPS]
    assert all(TM % (dilation * 16) == 0 for dilation in dils)
    sub_major = [pl.BlockSpec((1, dilation, TM // dilation, wg), lambda b, i: (b, 0, i, 0))
                 for dilation in dils]
    return pl.pallas_call(
        _inproj_kernel,
        grid=(batch, seq // TM),
        in_specs=[tok(d), _resident((1, d)), per_batch, per_batch,
                  _resident(wa.shape), _resident(wqt.shape), _resident(wvt.shape),
                  _resident(wkf.shape)],
        out_specs=sub_major + [tok_t(WIDTH_B), tok(WIDTH_B),
                               tok_t(N_HEADS_B * V_ROWS), tok(F_COLS)],
        out_shape=[jax.ShapeDtypeStruct((batch, dilation, seq // dilation, wg), BF16)
                   for dilation in dils]
        + [jax.ShapeDtypeStruct((batch, WIDTH_B, seq), BF16),
           jax.ShapeDtypeStruct((batch, seq, WIDTH_B), BF16),
           jax.ShapeDtypeStruct((batch, N_HEADS_B * V_ROWS, seq), BF16),
           jax.ShapeDtypeStruct((batch, seq, F_COLS), F32)],
        scratch_shapes=[pltpu.VMEM((wg // LANES, TM, LANES), F32)],
        compiler_params=_params(2),
        name="in_projection",
    )(x, g, sc, sh, wa, wqt, wvt, wkf)


def _log_sigmoid(x):
    return jnp.minimum(x, 0.0) - jnp.log1p(jnp.exp(-jnp.abs(x)))


def _decay_kernel(fz_ref, b_ref, tri_ref, o_ref):
    seq = fz_ref.shape[1]
    tri = tri_ref[...]
    carry = jnp.zeros((1, F_COLS), F32)
    for c in range(seq // DECAY_CHUNK):
        rows = slice(c * DECAY_CHUNK, (c + 1) * DECAY_CHUNK)
        lf = _log_sigmoid(fz_ref[0, rows, :] + b_ref[...])
        hi = lf.astype(BF16)
        rest = lf - hi.astype(F32)
        mid = rest.astype(BF16)
        lo = (rest - mid.astype(F32)).astype(BF16)
        cs = _dot(tri, hi) + _dot(tri, mid) + _dot(tri, lo) + carry
        o_ref[0, rows, :] = cs
        carry = cs[DECAY_CHUNK - 1:DECAY_CHUNK, :]


def _decay_cumsum(fz, b_forget):
    batch, seq, _ = fz.shape
    bias = jnp.zeros((1, F_COLS), F32).at[0, :N_HEADS_B].set(b_forget.astype(F32))
    idx = np.arange(DECAY_CHUNK)
    tri = jnp.asarray(idx[None, :] <= idx[:, None], dtype=BF16)
    return pl.pallas_call(
        _decay_kernel,
        grid=(batch,),
        in_specs=[pl.BlockSpec((1, seq, F_COLS), lambda b: (b, 0, 0)),
                  _resident(bias.shape), _resident(tri.shape)],
        out_specs=pl.BlockSpec((1, seq, F_COLS), lambda b: (b, 0, 0)),
        out_shape=jax.ShapeDtypeStruct((batch, seq, F_COLS), F32),
        compiler_params=_params(1),
        name="forget_decay_cumsum",
    )(fz, bias, tri)


def _band_bias():
    qi = np.arange(BLOCK)[:, None]
    kj = np.arange(2 * BLOCK)[None, :]
    dist = qi + BLOCK - kj
    in_band = (dist >= 0) & (dist <= BLOCK)
    slopes = _alibi_slopes()
    bias = np.empty((N_HEADS_A, BLOCK, 2 * BLOCK), np.float32)
    for g, (window, dilation) in enumerate(DIL_GROUPS):
        assert window // dilation == BLOCK
        for h in range(HEADS_PER_GROUP):
            head = g * HEADS_PER_GROUP + h
            alibi = -slopes[head] * (dist * dilation).astype(np.float32)
            bias[head] = np.where(in_band, alibi, np.float32(NEG))
    return bias


def _unroll(trips):
    return next(u for u in (DIL_BLOCKS_IN_FLIGHT, 3, 2, 1) if trips % u == 0)


def _dilated_kernel(c0_ref, p0_ref, c1_ref, p1_ref, c2_ref, p2_ref, bias_ref, o_ref,
                    state_ref, gather_ref, stage_ref):
    span = pl.program_id(1)
    kj = lax.broadcasted_iota(jnp.int32, (BLOCK, 2 * BLOCK), 1)
    no_prev = jnp.where(kj >= jnp.where(span == 0, BLOCK, 0), 0.0, NEG)
    low_half = lax.broadcasted_iota(jnp.int32, (BLOCK, LANES), 1) < HEAD_DIM

    def own_half(h):
        return low_half if h % 2 == 0 else jnp.logical_not(low_half)

    def attend(g, q_of, k_of, v_of, rows, first):
        for h in range(HEADS_PER_GROUP):
            q = q_of(h // 2)
            q = jnp.where(own_half(h), q, jnp.zeros_like(q))
            z = _dot_nt(q, k_of(h // 2)) + bias_ref[g * HEADS_PER_GROUP + h]
            if first:
                z = z + no_prev
            m = jnp.max(z, axis=-1, keepdims=True)
            p = jnp.exp(z - m)
            l = jnp.sum(p, axis=-1, keepdims=True)
            out = _dot(p.astype(BF16), v_of(h // 2)) / l
            state_ref[g, h, rows, :] = jnp.where(own_half(h), out, m + jnp.log(l))

    for g, (cur_ref, prev_ref) in enumerate(((c0_ref, p0_ref), (c1_ref, p1_ref),
                                             (c2_ref, p2_ref))):
        dilation = DIL_GROUPS[g][1]
        class_rows = cur_ref.shape[2]
        nblk = class_rows // BLOCK

        def cols(part, pair):
            return slice(part * WIDTH_G + pair * LANES, part * WIDTH_G + (pair + 1) * LANES)

        def residue(r, carry, g=g, cur_ref=cur_ref, prev_ref=prev_ref, nblk=nblk,
                    cols=cols, class_rows=class_rows):
            def state_rows(n):
                return pl.ds(pl.multiple_of(r * class_rows + n * BLOCK, BLOCK), BLOCK)

            def with_prev(part):
                return lambda pair: jnp.concatenate(
                    [prev_ref[0, r, :, cols(part, pair)],
                     cur_ref[0, r, :BLOCK, cols(part, pair)]], axis=0)

            attend(g, lambda pair: cur_ref[0, r, :BLOCK, cols(0, pair)], with_prev(1),
                   with_prev(2), state_rows(0), first=True)

            def block(n, c):
                row0 = pl.multiple_of(n * BLOCK, BLOCK)
                keys = pl.ds(row0 - BLOCK, 2 * BLOCK)
                attend(g, lambda pair: cur_ref[0, r, pl.ds(row0, BLOCK), cols(0, pair)],
                       lambda pair: cur_ref[0, r, keys, cols(1, pair)],
                       lambda pair: cur_ref[0, r, keys, cols(2, pair)],
                       state_rows(n), first=False)
                return c

            if nblk > 1:
                lax.fori_loop(1, nblk, block, 0, unroll=_unroll(nblk - 1))
            return carry

        if dilation == 1:
            residue(0, 0)
        else:
            lax.fori_loop(0, dilation, residue, 0, unroll=_unroll(dilation) if nblk == 1 else 1)

    mid = DIL_GROUPS[1][1]
    assert [d for _, d in DIL_GROUPS] == [1, mid, mid * mid] and BLOCK % mid == 0
    mid_rows = DIL_SPAN // mid
    wide_rows = DIL_SPAN // (mid * mid)
    run = BLOCK // mid

    def merge(idx, carry):
        r = idx // (mid_rows // BLOCK)
        n = idx % (mid_rows // BLOCK)
        tokens = pl.ds(mid * BLOCK * n + r, BLOCK, stride=mid)
        halves = []
        for h in range(HEADS_PER_GROUP):
            for a in range(mid):
                gather_ref[h, pl.ds(a, run, stride=mid), :] = state_ref[
                    2, h, pl.ds(pl.multiple_of((mid * a + r) * wide_rows + run * n, run), run), :]
            tiles = (state_ref[0, h, tokens, :],
                     state_ref[1, h, pl.ds(pl.multiple_of(r * mid_rows + n * BLOCK, BLOCK), BLOCK), :],
                     gather_ref[h])
            lses = [pltpu.roll(t, HEAD_DIM, 1) for t in tiles]
            top = jnp.maximum(jnp.maximum(lses[0], lses[1]), lses[2])
            weights = [jnp.exp(lse - top) for lse in lses]
            num = sum(w * t for w, t in zip(weights, tiles))
            halves.append(num / sum(weights))
        for pair in range(HEADS_PER_GROUP // 2):
            stage_ref[pair, tokens, :] = jnp.where(low_half, halves[2 * pair], halves[2 * pair + 1])
        return carry

    lax.fori_loop(0, DIL_SPAN // BLOCK, merge, 0)
    for pair in range(HEADS_PER_GROUP // 2):
        o_ref[0, :, pair * LANES:(pair + 1) * LANES] = stage_ref[pair].astype(o_ref.dtype)


def _dilated_mixture(zs):
    batch, _, seq, wg = zs[0].shape
    assert seq % DIL_SPAN == 0
    in_specs, operands = [], []
    for g, (_, dilation) in enumerate(DIL_GROUPS):
        rows = DIL_SPAN // dilation
        assert rows % BLOCK == 0 and zs[g].shape == (batch, dilation, seq // dilation, wg)
        nblk = rows // BLOCK
        in_specs.append(pl.BlockSpec((1, dilation, rows, wg), lambda b, t: (b, 0, t, 0)))
        in_specs.append(pl.BlockSpec(
            (1, dilation, BLOCK, wg),
            lambda b, t, nblk=nblk: (b, 0, jnp.maximum(t * nblk - 1, 0), 0)))
        operands += [zs[g], zs[g]]
    bias = jnp.asarray(_band_bias())
    return pl.pallas_call(
        _dilated_kernel,
        grid=(batch, seq // DIL_SPAN),
        in_specs=in_specs + [_resident(bias.shape)],
        out_specs=pl.BlockSpec((1, DIL_SPAN, WIDTH_G), lambda b, t: (b, t, 0)),
        out_shape=jax.ShapeDtypeStruct((batch, seq, WIDTH_G), BF16),
        scratch_shapes=[pltpu.VMEM((N_GROUPS, HEADS_PER_GROUP, DIL_SPAN, LANES), F32),
                        pltpu.VMEM((HEADS_PER_GROUP, BLOCK, LANES), F32),
                        pltpu.VMEM((HEADS_PER_GROUP // 2, DIL_SPAN, LANES), F32)],
        compiler_params=_params(2),
        name="dilated_attention",
    )(*operands, bias)


def _fox_kernel(qt_ref, k_ref, vt_ref, f_ref, o_ref, acc_ref):
    tq = qt_ref.shape[2]
    qb = pl.program_id(1)
    q0 = qb * tq
    n_full = qb * (tq // FOX_TKV)
    key = lax.broadcasted_iota(jnp.int32, (FOX_TKV, tq), 0)
    qry = lax.broadcasted_iota(jnp.int32, (FOX_TKV, tq), 1)
    f_end = f_ref[0, pl.ds(pl.multiple_of(q0 + tq - 8, 8), 8), :][7:8]
    acc_ref[...] = jnp.zeros_like(acc_ref)

    def kv_block(j, ms, masked):
        k0 = pl.multiple_of(j * FOX_TKV, FOX_TKV)
        decay = f_end - f_ref[0, pl.ds(k0, FOX_TKV), :]
        new_ms = []
        for h in range(N_HEADS_B):
            hs = slice(h * HEAD_DIM, (h + 1) * HEAD_DIM)
            s = _dot(k_ref[0, pl.ds(k0, FOX_TKV), hs], qt_ref[0, hs, :]) + decay[:, h:h + 1]
            if masked:
                s = jnp.where(key + k0 <= qry + q0, s, NEG)
            m_new = jnp.maximum(ms[h], jnp.max(s, axis=0, keepdims=True))
            alpha = jnp.exp(ms[h] - m_new)
            p = jnp.exp(s - m_new).astype(BF16)
            vt = vt_ref[0, h * V_ROWS:(h + 1) * V_ROWS, pl.ds(k0, FOX_TKV)]
            acc_ref[h] = acc_ref[h] * alpha + _dot(vt, p)
            new_ms.append(m_new)
        return tuple(new_ms)

    ms = tuple(jnp.full((1, tq), NEG, F32) for _ in range(N_HEADS_B))
    ms = lax.fori_loop(0, n_full, functools.partial(kv_block, masked=False), ms)
    for d in range(tq // FOX_TKV):
        ms = kv_block(n_full + d, ms, masked=True)
    for h in range(N_HEADS_B):
        acc = acc_ref[h]
        out_t = acc / acc[HEAD_DIM:HEAD_DIM + 1]
        o_ref[0, :, h * HEAD_DIM:(h + 1) * HEAD_DIM] = (
            out_t.T[:, :HEAD_DIM].astype(o_ref.dtype))


def _forgetting_attention(qt, k, vt, decay):
    batch, seq, _ = k.shape
    return pl.pallas_call(
        _fox_kernel,
        grid=(batch, seq // FOX_TQ),
        in_specs=[pl.BlockSpec((1, WIDTH_B, FOX_TQ), lambda b, i: (b, 0, i)),
                  pl.BlockSpec((1, seq, WIDTH_B), lambda b, i: (b, 0, 0)),
                  pl.BlockSpec((1, N_HEADS_B * V_ROWS, seq), lambda b, i: (b, 0, 0)),
                  pl.BlockSpec((1, seq, F_COLS), lambda b, i: (b, 0, 0))],
        out_specs=pl.BlockSpec((1, FOX_TQ, WIDTH_B), lambda b, i: (b, i, 0)),
        out_shape=jax.ShapeDtypeStruct((batch, seq, WIDTH_B), BF16),
        scratch_shapes=[pltpu.VMEM((N_HEADS_B, V_ROWS, FOX_TQ), F32)],
        compiler_params=_params(2),
        name="forgetting_attention",
    )(qt, k, vt, decay)


def _outproj_kernel(x_ref, g_ref, sc_ref, sh_ref, gate_ref, ya_ref, yb_ref,
                    wg_ref, wua_ref, wub_ref, wo_ref, o_ref, merged_ref):
    x = x_ref[0]
    d = x.shape[-1]
    h = _modulated_norm(x, g_ref[...], sc_ref[0], sh_ref[0]).astype(BF16)
    ya = ya_ref[0]
    yb = yb_ref[0]
    for c in range(d // OUT_CHUNK):
        cs = slice(c * OUT_CHUNK, (c + 1) * OUT_CHUNK)
        cs_b = slice(d + c * OUT_CHUNK, d + (c + 1) * OUT_CHUNK)
        gate_a = jax.nn.sigmoid(_dot(h, wg_ref[:, cs]))
        gate_b = jax.nn.sigmoid(_dot(h, wg_ref[:, cs_b]))
        merged = gate_a * _dot(ya, wua_ref[:, cs]) + gate_b * _dot(yb, wub_ref[:, cs])
        merged_ref[:, cs] = merged.astype(BF16)
    o_ref[0] = x + gate_ref[0] * _dot(merged_ref[...], wo_ref[...])


def _out_projection(x, g, sc, sh, gate, ya, yb, wg, wua, wub, wo):
    batch, seq, d = x.shape
    tok = lambda w: pl.BlockSpec((1, TM, w), lambda b, i: (b, i, 0))
    per_batch = pl.BlockSpec((1, 1, d), lambda b, i: (b, 0, 0))
    return pl.pallas_call(
        _outproj_kernel,
        grid=(batch, seq // TM),
        in_specs=[tok(d), _resident((1, d)), per_batch, per_batch, per_batch,
                  tok(WIDTH_G), tok(WIDTH_B),
                  _resident(wg.shape), _resident(wua.shape), _resident(wub.shape),
                  _resident(wo.shape)],
        out_specs=tok(d),
        out_shape=jax.ShapeDtypeStruct((batch, seq, d), F32),
        scratch_shapes=[pltpu.VMEM((TM, d), BF16)],
        compiler_params=_params(2),
        name="out_projection",
    )(x, g, sc, sh, gate, ya, yb, wg, wua, wub, wo)


def _ffn_kernel(*refs, final_norm):
    x_ref, g_ref, sc_ref, sh_ref, gate_ref, win_ref, wout_ref = refs[:7]
    refs = refs[7:]
    if final_norm:
        gf_ref = refs[0]
        refs = refs[1:]
    o_ref, h_ref, acc_ref = refs
    x = x_ref[0]
    h_ref[...] = _modulated_norm(x, g_ref[...], sc_ref[0], sh_ref[0]).astype(BF16)
    for c in range(D_FF // FF_CHUNK):
        cs = slice(c * FF_CHUNK, (c + 1) * FF_CHUNK)
        cs_up = slice(D_FF + c * FF_CHUNK, D_FF + (c + 1) * FF_CHUNK)
        h = h_ref[...]
        gt = _dot(h, win_ref[:, cs])
        up = _dot(h, win_ref[:, cs_up])
        act = ((gt * jax.nn.sigmoid(gt)) * up).astype(BF16)
        part = _dot(act, wout_ref[cs, :])
        if c == 0:
            acc_ref[...] = part
        else:
            acc_ref[...] += part
    y = x + gate_ref[0] * acc_ref[...]
    if final_norm:
        y = (y * lax.rsqrt(jnp.mean(y * y, axis=-1, keepdims=True) + EPS)) * gf_ref[...]
    o_ref[0] = y


def _ffn(x, g, sc, sh, gate, win, wout, g_final):
    batch, seq, d = x.shape
    final_norm = g_final is not None
    tok = pl.BlockSpec((1, TM, d), lambda b, i: (b, i, 0))
    per_batch = pl.BlockSpec((1, 1, d), lambda b, i: (b, 0, 0))
    in_specs = [tok, _resident((1, d)), per_batch, per_batch, per_batch,
                _resident(win.shape), _resident(wout.shape)]
    operands = [x, g, sc, sh, gate, win, wout]
    if final_norm:
        in_specs.append(_resident((1, d)))
        operands.append(g_final)
    return pl.pallas_call(
        functools.partial(_ffn_kernel, final_norm=final_norm),
        grid=(batch, seq // TM),
        in_specs=in_specs,
        out_specs=tok,
        out_shape=jax.ShapeDtypeStruct((batch, seq, d), F32),
        scratch_shapes=[pltpu.VMEM((TM, d), BF16), pltpu.VMEM((TM, d), F32)],
        compiler_params=_params(2),
        name="swiglu_ffn",
    )(*operands)


def _split_w_in(w_in):
    qa, ka, va, qb, kb, vb, fz, gza, gzb = jnp.split(
        w_in, np.cumsum((WIDTH_A, WIDTH_A, WIDTH_A, WIDTH_B, WIDTH_B, WIDTH_B, N_HEADS_B,
                         D_MODEL))[:].tolist(), axis=-1)
    cols = []
    for g in range(N_GROUPS):
        gs = slice(g * WIDTH_G, (g + 1) * WIDTH_G)
        cols += [qa[:, gs] * QK_SCALE, ka[:, gs], va[:, gs]]
    wa = jnp.concatenate(cols, axis=-1).astype(BF16)
    pad = jnp.zeros((w_in.shape[0], F_COLS - N_HEADS_B), w_in.dtype)
    wqt = (qb * QK_SCALE).T.astype(BF16)
    wvt = vb.T.astype(BF16)
    wkf = jnp.concatenate([kb, fz, pad], axis=-1).astype(BF16)
    wg = jnp.concatenate([gza, gzb], axis=-1).astype(BF16)
    return wa, wqt, wvt, wkf, wg


def kernel(x, c, w_ada, b_ada, norm_mix, w_in, b_forget, w_up_a, w_up_b, w_out,
           norm_ffn, w_ffn_in, w_ffn_out, norm_final):
    depth = w_ada.shape[0]
    batch, seq, d = x.shape
    assert seq % TM == 0 and seq % FOX_TQ == 0 and FOX_TQ % FOX_TKV == 0
    mod = _modulation(c, w_ada, b_ada).reshape(depth, batch, N_MOD, 1, d)
    for l in range(depth):
        sh1, sc1, g1, sh2, sc2, g2 = (mod[l, :, j] for j in range(N_MOD))
        wa, wqt, wvt, wkf, wg = _split_w_in(w_in[l])
        g_mix = norm_mix[l].reshape(1, d)
        za0, za1, za2, qt, k, vt, fz = _in_projection(x, g_mix, sc1, sh1, wa, wqt, wvt, wkf)
        decay = _decay_cumsum(fz, b_forget[l])
        ya = _dilated_mixture((za0, za1, za2))
        yb = _forgetting_attention(qt, k, vt, decay)
        x = _out_projection(x, g_mix, sc1, sh1, g1, ya, yb, wg,
                            w_up_a[l].astype(BF16), w_up_b[l].astype(BF16),
                            w_out[l].astype(BF16))
        g_final = norm_final.reshape(1, d) if l == depth - 1 else None
        x = _ffn(x, norm_ffn[l].reshape(1, d), sc2, sh2, g2,
                 w_ffn_in[l].astype(BF16), w_ffn_out[l].astype(BF16), g_final)
    return x
```

```python
import functools

import numpy as np
import jax
import jax.numpy as jnp
from jax import lax
from jax.experimental import pallas as pl
from jax.experimental.pallas import tpu as pltpu

F32 = jnp.float32
BF16 = jnp.bfloat16

D_MODEL = 1024
HEAD_DIM = 64
DIL_GROUPS = ((128, 1), (512, 4), (2048, 16))
HEADS_PER_GROUP = 4
N_GROUPS = len(DIL_GROUPS)
N_HEADS_A = HEADS_PER_GROUP * N_GROUPS
N_HEADS_B = 4
WIDTH_G = HEADS_PER_GROUP * HEAD_DIM
WIDTH_A = N_HEADS_A * HEAD_DIM
WIDTH_B = N_HEADS_B * HEAD_DIM
D_FF = -(-8 * D_MODEL // (3 * 256)) * 256
BLOCK = 128
ALIBI_MAX = 8.0
EPS = 1e-6
N_MOD = 6
QK_SCALE = HEAD_DIM ** -0.5

NEG = -1e30
LANES = 128
F_COLS = LANES
V_ROWS = 2 * HEAD_DIM
N_STAT = HEADS_PER_GROUP

VMEM_LIMIT = 56 * 1024 * 1024

TM = 512
FF_CHUNK = 256
OUT_CHUNK = 256
DECAY_CHUNK = 256
FOX_TQ = 512
FOX_TKV = 512
DIL_SPAN = BLOCK * max(d for _, d in DIL_GROUPS)
DIL_BLOCKS_IN_FLIGHT = 4


def _params(n_axes, flags=None):
    return pltpu.CompilerParams(dimension_semantics=("arbitrary",) * n_axes,
                                vmem_limit_bytes=VMEM_LIMIT, flags=flags)


def _resident(shape):
    zeros = (0,) * len(shape)
    return pl.BlockSpec(shape, lambda *_: zeros, pipeline_mode=pl.Buffered(1))


def _alibi_slopes():
    h = np.arange(1, N_HEADS_A + 1, dtype=np.float32)
    return np.asarray(2.0 ** (-ALIBI_MAX * h / N_HEADS_A), dtype=np.float32)


def _modulated_norm(x, g, sc, sh):
    y = x * lax.rsqrt(jnp.mean(x * x, axis=-1, keepdims=True) + EPS)
    return (y * g) * (1.0 + sc) + sh


def _dot(a, b):
    return jnp.dot(a, b, preferred_element_type=F32)


def _dot_nt(a, b):
    return lax.dot_general(a, b, (((1,), (1,)), ((), ())), preferred_element_type=F32)


def _mod_kernel(c_ref, w_ref, b_ref, o_ref):
    c = c_ref[...]
    c_act = (c * jax.nn.sigmoid(c)).astype(BF16)
    o_ref[0] = _dot(c_act, w_ref[0].astype(BF16)) + b_ref[0]


def _modulation(c, w_ada, b_ada):
    depth, d, n = w_ada.shape
    batch = c.shape[0]
    return pl.pallas_call(
        _mod_kernel,
        grid=(depth, n // d),
        in_specs=[pl.BlockSpec((batch, d), lambda l, j: (0, 0)),
                  pl.BlockSpec((1, d, d), lambda l, j: (l, 0, j)),
                  pl.BlockSpec((1, 1, d), lambda l, j: (l, 0, j))],
        out_specs=pl.BlockSpec((1, batch, d), lambda l, j: (l, 0, j)),
        out_shape=jax.ShapeDtypeStruct((depth, batch, n), F32),
        compiler_params=_params(2),
        name="adaln_modulation",
    )(c, w_ada, b_ada.reshape(depth, 1, n))


def _inproj_kernel(x_ref, g_ref, sc_ref, sh_ref, wa_ref, wqt_ref, wvt_ref, wkf_ref,
                   za0_ref, za1_ref, za2_ref, qt_ref, k_ref, vt_ref, fz_ref, stage_ref):
    h = _modulated_norm(x_ref[0], g_ref[...], sc_ref[0], sh_ref[0]).astype(BF16)
    tm = h.shape[0]
    wg = 3 * WIDTH_G
    for g, za_ref in enumerate((za0_ref, za1_ref, za2_ref)):
        dilation = DIL_GROUPS[g][1]
        za = _dot(h, wa_ref[:, g * wg:(g + 1) * wg])
        if dilation == 1:
            za_ref[0, 0] = za.astype(BF16)
            continue
        for c in range(wg // LANES):
            stage_ref[c] = za[:, c * LANES:(c + 1) * LANES]
        for r in range(dilation):
            rows = pl.ds(r, tm // dilation, stride=dilation)
            for c in range(wg // LANES):
                za_ref[0, r, :, c * LANES:(c + 1) * LANES] = stage_ref[c, rows, :].astype(BF16)
    qt_ref[0] = _dot_nt(wqt_ref[...], h).astype(BF16)
    vt = _dot_nt(wvt_ref[...], h).astype(BF16)
    ones_row = (lax.broadcasted_iota(jnp.int32, (HEAD_DIM, tm), 0) == 0).astype(BF16)
    for hd in range(N_HEADS_B):
        vt_ref[0, hd * V_ROWS:hd * V_ROWS + HEAD_DIM, :] = vt[hd * HEAD_DIM:(hd + 1) * HEAD_DIM]
        vt_ref[0, hd * V_ROWS + HEAD_DIM:(hd + 1) * V_ROWS, :] = ones_row
    kf = _dot(h, wkf_ref[...])
    k_ref[0] = kf[:, :WIDTH_B].astype(BF16)
    fz_ref[0] = kf[:, WIDTH_B:]


def _in_projection(x, g, sc, sh, wa, wqt, wvt, wkf):
    batch, seq, d = x.shape
    wg = 3 * WIDTH_G
    tok = lambda w: pl.BlockSpec((1, TM, w), lambda b, i: (b, i, 0))
    tok_t = lambda rows: pl.BlockSpec((1, rows, TM), lambda b, i: (b, 0, i))
    per_batch = pl.BlockSpec((1, 1, d), lambda b, i: (b, 0, 0))
    dils = [dilation for _, dilation in DIL_GROUPS]
    assert all(TM % (dilation * 16) == 0 for dilation in dils)
    sub_major = [pl.BlockSpec((1, dilation, TM // dilation, wg), lambda b, i: (b, 0, i, 0))
                 for dilation in dils]
    return pl.pallas_call(
        _inproj_kernel,
        grid=(batch, seq // TM),
        in_specs=[tok(d), _resident((1, d)), per_batch, per_batch,
                  _resident(wa.shape), _resident(wqt.shape), _resident(wvt.shape),
                  _resident(wkf.shape)],
        out_specs=sub_major + [tok_t(WIDTH_B), tok(WIDTH_B),
                               tok_t(N_HEADS_B * V_ROWS), tok(F_COLS)],
        out_shape=[jax.ShapeDtypeStruct((batch, dilation, seq // dilation, wg), BF16)
                   for dilation in dils]
        + [jax.ShapeDtypeStruct((batch, WIDTH_B, seq), BF16),
           jax.ShapeDtypeStruct((batch, seq, WIDTH_B), BF16),
           jax.ShapeDtypeStruct((batch, N_HEADS_B * V_ROWS, seq), BF16),
           jax.ShapeDtypeStruct((batch, seq, F_COLS), F32)],
        scratch_shapes=[pltpu.VMEM((wg // LANES, TM, LANES), F32)],
        compiler_params=_params(2),
        name="in_projection",
    )(x, g, sc, sh, wa, wqt, wvt, wkf)


def _log_sigmoid(x):
    return jnp.minimum(x, 0.0) - jnp.log1p(jnp.exp(-jnp.abs(x)))


def _decay_kernel(fz_ref, b_ref, tri_ref, o_ref):
    seq = fz_ref.shape[1]
    tri = tri_ref[...]
    carry = jnp.zeros((1, F_COLS), F32)
    for c in range(seq // DECAY_CHUNK):
        rows = slice(c * DECAY_CHUNK, (c + 1) * DECAY_CHUNK)
        lf = _log_sigmoid(fz_ref[0, rows, :] + b_ref[...])
        hi = lf.astype(BF16)
        rest = lf - hi.astype(F32)
        mid = rest.astype(BF16)
        lo = (rest - mid.astype(F32)).astype(BF16)
        cs = _dot(tri, hi) + _dot(tri, mid) + _dot(tri, lo) + carry
        o_ref[0, rows, :] = cs
        carry = cs[DECAY_CHUNK - 1:DECAY_CHUNK, :]


def _decay_cumsum(fz, b_forget):
    batch, seq, _ = fz.shape
    bias = jnp.zeros((1, F_COLS), F32).at[0, :N_HEADS_B].set(b_forget.astype(F32))
    idx = np.arange(DECAY_CHUNK)
    tri = jnp.asarray(idx[None, :] <= idx[:, None], dtype=BF16)
    return pl.pallas_call(
        _decay_kernel,
        grid=(batch,),
        in_specs=[pl.BlockSpec((1, seq, F_COLS), lambda b: (b, 0, 0)),
                  _resident(bias.shape), _resident(tri.shape)],
        out_specs=pl.BlockSpec((1, seq, F_COLS), lambda b: (b, 0, 0)),
        out_shape=jax.ShapeDtypeStruct((batch, seq, F_COLS), F32),
        compiler_params=_params(1),
        name="forget_decay_cumsum",
    )(fz, bias, tri)


def _band_bias():
    qi = np.arange(BLOCK)[:, None]
    kj = np.arange(2 * BLOCK)[None, :]
    dist = qi + BLOCK - kj
    in_band = (dist >= 0) & (dist <= BLOCK)
    slopes = _alibi_slopes()
    bias = np.empty((N_HEADS_A, BLOCK, 2 * BLOCK), np.float32)
    for g, (window, dilation) in enumerate(DIL_GROUPS):
        assert window // dilation == BLOCK
        for h in range(HEADS_PER_GROUP):
            head = g * HEADS_PER_GROUP + h
            alibi = -slopes[head] * (dist * dilation).astype(np.float32)
            bias[head] = np.where(in_band, alibi, np.float32(NEG))
    return bias


def _unroll(trips):
    return next(u for u in (DIL_BLOCKS_IN_FLIGHT, 3, 2, 1) if trips % u == 0)


def _dilated_kernel(c0_ref, p0_ref, c1_ref, p1_ref, c2_ref, p2_ref, bias_ref, o_ref,
                    state_ref, gather_ref, stage_ref):
    span = pl.program_id(1)
    kj = lax.broadcasted_iota(jnp.int32, (BLOCK, 2 * BLOCK), 1)
    no_prev = jnp.where(kj >= jnp.where(span == 0, BLOCK, 0), 0.0, NEG)
    low_half = lax.broadcasted_iota(jnp.int32, (BLOCK, LANES), 1) < HEAD_DIM

    def own_half(h):
        return low_half if h % 2 == 0 else jnp.logical_not(low_half)

    def attend(g, q_of, k_of, v_of, rows, first):
        for h in range(HEADS_PER_GROUP):
            q = q_of(h // 2)
            q = jnp.where(own_half(h), q, jnp.zeros_like(q))
            z = _dot_nt(q, k_of(h // 2)) + bias_ref[g * HEADS_PER_GROUP + h]
            if first:
                z = z + no_prev
            m = jnp.max(z, axis=-1, keepdims=True)
            p = jnp.exp(z - m)
            l = jnp.sum(p, axis=-1, keepdims=True)
            out = _dot(p.astype(BF16), v_of(h // 2)) / l
            state_ref[g, h, rows, :] = jnp.where(own_half(h), out, m + jnp.log(l))

    for g, (cur_ref, prev_ref) in enumerate(((c0_ref, p0_ref), (c1_ref, p1_ref),
                                             (c2_ref, p2_ref))):
        dilation = DIL_GROUPS[g][1]
        class_rows = cur_ref.shape[2]
        nblk = class_rows // BLOCK

        def cols(part, pair):
            return slice(part * WIDTH_G + pair * LANES, part * WIDTH_G + (pair + 1) * LANES)

        def residue(r, carry, g=g, cur_ref=cur_ref, prev_ref=prev_ref, nblk=nblk,
                    cols=cols, class_rows=class_rows):
            def state_rows(n):
                return pl.ds(pl.multiple_of(r * class_rows + n * BLOCK, BLOCK), BLOCK)

            def with_prev(part):
                return lambda pair: jnp.concatenate(
                    [prev_ref[0, r, :, cols(part, pair)],
                     cur_ref[0, r, :BLOCK, cols(part, pair)]], axis=0)

            attend(g, lambda pair: cur_ref[0, r, :BLOCK, cols(0, pair)], with_prev(1),
                   with_prev(2), state_rows(0), first=True)

            def block(n, c):
                row0 = pl.multiple_of(n * BLOCK, BLOCK)
                keys = pl.ds(row0 - BLOCK, 2 * BLOCK)
                attend(g, lambda pair: cur_ref[0, r, pl.ds(row0, BLOCK), cols(0, pair)],
                       lambda pair: cur_ref[0, r, keys, cols(1, pair)],
                       lambda pair: cur_ref[0, r, keys, cols(2, pair)],
                       state_rows(n), first=False)
                return c

            if nblk > 1:
                lax.fori_loop(1, nblk, block, 0, unroll=_unroll(nblk - 1))
            return carry

        if dilation == 1:
            residue(0, 0)
        else:
            lax.fori_loop(0, dilation, residue, 0, unroll=_unroll(dilation) if nblk == 1 else 1)

    mid = DIL_GROUPS[1][1]
    assert [d for _, d in DIL_GROUPS] == [1, mid, mid * mid] and BLOCK % mid == 0
    mid_rows = DIL_SPAN // mid
    wide_rows = DIL_SPAN // (mid * mid)
    run = BLOCK // mid

    def merge(idx, carry):
        r = idx // (mid_rows // BLOCK)
        n = idx % (mid_rows // BLOCK)
        tokens = pl.ds(mid * BLOCK * n + r, BLOCK, stride=mid)
        halves = []
        for h in range(HEADS_PER_GROUP):
            for a in range(mid):
                gather_ref[h, pl.ds(a, run, stride=mid), :] = state_ref[
                    2, h, pl.ds(pl.multiple_of((mid * a + r) * wide_rows + run * n, run), run), :]
            tiles = (state_ref[0, h, tokens, :],
                     state_ref[1, h, pl.ds(pl.multiple_of(r * mid_rows + n * BLOCK, BLOCK), BLOCK), :],
                     gather_ref[h])
            lses = [pltpu.roll(t, HEAD_DIM, 1) for t in tiles]
            top = jnp.maximum(jnp.maximum(lses[0], lses[1]), lses[2])
            weights = [jnp.exp(lse - top) for lse in lses]
            num = sum(w * t for w, t in zip(weights, tiles))
            halves.append(num / sum(weights))
        for pair in range(HEADS_PER_GROUP // 2):
            stage_ref[pair, tokens, :] = jnp.where(low_half, halves[2 * pair], halves[2 * pair + 1])
        return carry

    lax.fori_loop(0, DIL_SPAN // BLOCK, merge, 0)
    for pair in range(HEADS_PER_GROUP // 2):
        o_ref[0, :, pair * LANES:(pair + 1) * LANES] = stage_ref[pair].astype(o_ref.dtype)


def _dilated_mixture(zs):
    batch, _, seq, wg = zs[0].shape
    assert seq % DIL_SPAN == 0
    in_specs, operands = [], []
    for g, (_, dilation) in enumerate(DIL_GROUPS):
        rows = DIL_SPAN // dilation
        assert rows % BLOCK == 0 and zs[g].shape == (batch, dilation, seq // dilation, wg)
        nblk = rows // BLOCK
        in_specs.append(pl.BlockSpec((1, dilation, rows, wg), lambda b, t: (b, 0, t, 0)))
        in_specs.append(pl.BlockSpec(
            (1, dilation, BLOCK, wg),
            lambda b, t, nblk=nblk: (b, 0, jnp.maximum(t * nblk - 1, 0), 0)))
        operands += [zs[g], zs[g]]
    bias = jnp.asarray(_band_bias())
    return pl.pallas_call(
        _dilated_kernel,
        grid=(batch, seq // DIL_SPAN),
        in_specs=in_specs + [_resident(bias.shape)],
        out_specs=pl.BlockSpec((1, DIL_SPAN, WIDTH_G), lambda b, t: (b, t, 0)),
        out_shape=jax.ShapeDtypeStruct((batch, seq, WIDTH_G), BF16),
        scratch_shapes=[pltpu.VMEM((N_GROUPS, HEADS_PER_GROUP, DIL_SPAN, LANES), F32),
                        pltpu.VMEM((HEADS_PER_GROUP, BLOCK, LANES), F32),
                        pltpu.VMEM((HEADS_PER_GROUP // 2, DIL_SPAN, LANES), F32)],
        compiler_params=_params(2),
        name="dilated_attention",
    )(*operands, bias)


def _fox_kernel(qt_ref, k_ref, vt_ref, f_ref, o_ref, acc_ref, s_ref):
    tq = qt_ref.shape[2]
    qb = pl.program_id(1)
    q0 = qb * tq
    n_full = qb * (tq // FOX_TKV)
    key = lax.broadcasted_iota(jnp.int32, (FOX_TKV, tq), 0)
    qry = lax.broadcasted_iota(jnp.int32, (FOX_TKV, tq), 1)
    f_end = f_ref[0, pl.ds(pl.multiple_of(q0 + tq - 8, 8), 8), :][7:8]
    acc_ref[...] = jnp.zeros_like(acc_ref)

    def kv_block(j, ms, masked):
        k0 = pl.multiple_of(j * FOX_TKV, FOX_TKV)
        decay = f_end - f_ref[0, pl.ds(k0, FOX_TKV), :]

        def scores(h):
            hs = slice(h * HEAD_DIM, (h + 1) * HEAD_DIM)
            s = _dot(k_ref[0, pl.ds(k0, FOX_TKV), hs], qt_ref[0, hs, :]) + decay[:, h:h + 1]
            if masked:
                s = jnp.where(key + k0 <= qry + q0, s, NEG)
            s_ref[h % 2] = s
            return jnp.max(s, axis=0, keepdims=True)

        new_ms = []
        top_next = scores(0)
        for h in range(N_HEADS_B):
            m_new = jnp.maximum(ms[h], top_next)
            if h + 1 < N_HEADS_B:
                top_next = scores(h + 1)
            alpha = jnp.exp(ms[h] - m_new)
            p = jnp.exp(s_ref[h % 2] - m_new).astype(BF16)
            vt = vt_ref[0, h * V_ROWS:(h + 1) * V_ROWS, pl.ds(k0, FOX_TKV)]
            acc_ref[h] = acc_ref[h] * alpha + _dot(vt, p)
            new_ms.append(m_new)
        return tuple(new_ms)

    ms = tuple(jnp.full((1, tq), NEG, F32) for _ in range(N_HEADS_B))
    ms = lax.fori_loop(0, n_full, functools.partial(kv_block, masked=False), ms)
    for d in range(tq // FOX_TKV):
        ms = kv_block(n_full + d, ms, masked=True)
    for h in range(N_HEADS_B):
        acc = acc_ref[h]
        out_t = acc / acc[HEAD_DIM:HEAD_DIM + 1]
        o_ref[0, :, h * HEAD_DIM:(h + 1) * HEAD_DIM] = (
            out_t.T[:, :HEAD_DIM].astype(o_ref.dtype))


def _forgetting_attention(qt, k, vt, decay):
    batch, seq, _ = k.shape
    return pl.pallas_call(
        _fox_kernel,
        grid=(batch, seq // FOX_TQ),
        in_specs=[pl.BlockSpec((1, WIDTH_B, FOX_TQ), lambda b, i: (b, 0, i)),
                  pl.BlockSpec((1, seq, WIDTH_B), lambda b, i: (b, 0, 0)),
                  pl.BlockSpec((1, N_HEADS_B * V_ROWS, seq), lambda b, i: (b, 0, 0)),
                  pl.BlockSpec((1, seq, F_COLS), lambda b, i: (b, 0, 0))],
        out_specs=pl.BlockSpec((1, FOX_TQ, WIDTH_B), lambda b, i: (b, i, 0)),
        out_shape=jax.ShapeDtypeStruct((batch, seq, WIDTH_B), BF16),
        scratch_shapes=[pltpu.VMEM((N_HEADS_B, V_ROWS, FOX_TQ), F32),
                        pltpu.VMEM((2, FOX_TKV, FOX_TQ), F32)],
        compiler_params=_params(2),
        name="forgetting_attention",
    )(qt, k, vt, decay)


def _outproj_kernel(x_ref, g_ref, sc_ref, sh_ref, gate_ref, ya_ref, yb_ref,
                    wg_ref, wua_ref, wub_ref, wo_ref, o_ref, merged_ref):
    x = x_ref[0]
    d = x.shape[-1]
    h = _modulated_norm(x, g_ref[...], sc_ref[0], sh_ref[0]).astype(BF16)
    ya = ya_ref[0]
    yb = yb_ref[0]
    for c in range(d // OUT_CHUNK):
        cs = slice(c * OUT_CHUNK, (c + 1) * OUT_CHUNK)
        cs_b = slice(d + c * OUT_CHUNK, d + (c + 1) * OUT_CHUNK)
        gate_a = jax.nn.sigmoid(_dot(h, wg_ref[:, cs]))
        gate_b = jax.nn.sigmoid(_dot(h, wg_ref[:, cs_b]))
        merged = gate_a * _dot(ya, wua_ref[:, cs]) + gate_b * _dot(yb, wub_ref[:, cs])
        merged_ref[:, cs] = merged.astype(BF16)
    o_ref[0] = x + gate_ref[0] * _dot(merged_ref[...], wo_ref[...])


def _out_projection(x, g, sc, sh, gate, ya, yb, wg, wua, wub, wo):
    batch, seq, d = x.shape
    tok = lambda w: pl.BlockSpec((1, TM, w), lambda b, i: (b, i, 0))
    per_batch = pl.BlockSpec((1, 1, d), lambda b, i: (b, 0, 0))
    return pl.pallas_call(
        _outproj_kernel,
        grid=(batch, seq // TM),
        in_specs=[tok(d), _resident((1, d)), per_batch, per_batch, per_batch,
                  tok(WIDTH_G), tok(WIDTH_B),
                  _resident(wg.shape), _resident(wua.shape), _resident(wub.shape),
                  _resident(wo.shape)],
        out_specs=tok(d),
        out_shape=jax.ShapeDtypeStruct((batch, seq, d), F32),
        scratch_shapes=[pltpu.VMEM((TM, d), BF16)],
        compiler_params=_params(2),
        name="out_projection",
    )(x, g, sc, sh, gate, ya, yb, wg, wua, wub, wo)


def _ffn_kernel(*refs, final_norm):
    x_ref, g_ref, sc_ref, sh_ref, gate_ref, win_ref, wout_ref = refs[:7]
    refs = refs[7:]
    if final_norm:
        gf_ref = refs[0]
        refs = refs[1:]
    o_ref, h_ref, acc_ref = refs
    x = x_ref[0]
    h_ref[...] = _modulated_norm(x, g_ref[...], sc_ref[0], sh_ref[0]).astype(BF16)
    for c in range(D_FF // FF_CHUNK):
        cs = slice(c * FF_CHUNK, (c + 1) * FF_CHUNK)
        cs_up = slice(D_FF + c * FF_CHUNK, D_FF + (c + 1) * FF_CHUNK)
        h = h_ref[...]
        gt = _dot(h, win_ref[:, cs])
        up = _dot(h, win_ref[:, cs_up])
        act = ((gt * jax.nn.sigmoid(gt)) * up).astype(BF16)
        part = _dot(act, wout_ref[cs, :])
        if c == 0:
            acc_ref[...] = part
        else:
            acc_ref[...] += part
    y = x + gate_ref[0] * acc_ref[...]
    if final_norm:
        y = (y * lax.rsqrt(jnp.mean(y * y, axis=-1, keepdims=True) + EPS)) * gf_ref[...]
    o_ref[0] = y


def _ffn(x, g, sc, sh, gate, win, wout, g_final):
    batch, seq, d = x.shape
    final_norm = g_final is not None
    tok = pl.BlockSpec((1, TM, d), lambda b, i: (b, i, 0))
    per_batch = pl.BlockSpec((1, 1, d), lambda b, i: (b, 0, 0))
    in_specs = [tok, _resident((1, d)), per_batch, per_batch, per_batch,
                _resident(win.shape), _resident(wout.shape)]
    operands = [x, g, sc, sh, gate, win, wout]
    if final_norm:
        in_specs.append(_resident((1, d)))
        operands.append(g_final)
    return pl.pallas_call(
        functools.partial(_ffn_kernel, final_norm=final_norm),
        grid=(batch, seq // TM),
        in_specs=in_specs,
        out_specs=tok,
        out_shape=jax.ShapeDtypeStruct((batch, seq, d), F32),
        scratch_shapes=[pltpu.VMEM((TM, d), BF16), pltpu.VMEM((TM, d), F32)],
        compiler_params=_params(2),
        name="swiglu_ffn",
    )(*operands)


def _split_w_in(w_in):
    qa, ka, va, qb, kb, vb, fz, gza, gzb = jnp.split(
        w_in, np.cumsum((WIDTH_A, WIDTH_A, WIDTH_A, WIDTH_B, WIDTH_B, WIDTH_B, N_HEADS_B,
                         D_MODEL))[:].tolist(), axis=-1)
    cols = []
    for g in range(N_GROUPS):
        gs = slice(g * WIDTH_G, (g + 1) * WIDTH_G)
        cols += [qa[:, gs] * QK_SCALE, ka[:, gs], va[:, gs]]
    wa = jnp.concatenate(cols, axis=-1).astype(BF16)
    pad = jnp.zeros((w_in.shape[0], F_COLS - N_HEADS_B), w_in.dtype)
    wqt = (qb * QK_SCALE).T.astype(BF16)
    wvt = vb.T.astype(BF16)
    wkf = jnp.concatenate([kb, fz, pad], axis=-1).astype(BF16)
    wg = jnp.concatenate([gza, gzb], axis=-1).astype(BF16)
    return wa, wqt, wvt, wkf, wg


def kernel(x, c, w_ada, b_ada, norm_mix, w_in, b_forget, w_up_a, w_up_b, w_out,
           norm_ffn, w_ffn_in, w_ffn_out, norm_final):
    depth = w_ada.shape[0]
    batch, seq, d = x.shape
    assert seq % TM == 0 and seq % FOX_TQ == 0 and FOX_TQ % FOX_TKV == 0
    mod = _modulation(c, w_ada, b_ada).reshape(depth, batch, N_MOD, 1, d)
    for l in range(depth):
        sh1, sc1, g1, sh2, sc2, g2 = (mod[l, :, j] for j in range(N_MOD))
        wa, wqt, wvt, wkf, wg = _split_w_in(w_in[l])
        g_mix = norm_mix[l].reshape(1, d)
        za0, za1, za2, qt, k, vt, fz = _in_projection(x, g_mix, sc1, sh1, wa, wqt, wvt, wkf)
        decay = _decay_cumsum(fz, b_forget[l])
        ya = _dilated_mixture((za0, za1, za2))
        yb = _forgetting_attention(qt, k, vt, decay)
        x = _out_projection(x, g_mix, sc1, sh1, g1, ya, yb, wg,
                            w_up_a[l].astype(BF16), w_up_b[l].astype(BF16),
                            w_out[l].astype(BF16))
        g_final = norm_final.reshape(1, d) if l == depth - 1 else None
        x = _ffn(x, norm_ffn[l].reshape(1, d), sc2, sh2, g2,
                 w_ffn_in[l].astype(BF16), w_ffn_out[l].astype(BF16), g_final)
    return x
```

```python
import functools

import numpy as np
import jax
import jax.numpy as jnp
from jax import lax
from jax.experimental import pallas as pl
from jax.experimental.pallas import tpu as pltpu

F32 = jnp.float32
BF16 = jnp.bfloat16

D_MODEL = 1024
HEAD_DIM = 64
DIL_GROUPS = ((128, 1), (512, 4), (2048, 16))
HEADS_PER_GROUP = 4
N_GROUPS = len(DIL_GROUPS)
N_HEADS_A = HEADS_PER_GROUP * N_GROUPS
N_HEADS_B = 4
WIDTH_G = HEADS_PER_GROUP * HEAD_DIM
WIDTH_A = N_HEADS_A * HEAD_DIM
WIDTH_B = N_HEADS_B * HEAD_DIM
D_FF = -(-8 * D_MODEL // (3 * 256)) * 256
BLOCK = 128
ALIBI_MAX = 8.0
EPS = 1e-6
N_MOD = 6
QK_SCALE = HEAD_DIM ** -0.5

NEG = -1e30
LANES = 128
F_COLS = LANES
BF16_ROWS = 16
V_ROWS = HEAD_DIM + BF16_ROWS
Q_ROWS = LANES
DECAY_PIECES = 3
LOG2E = 1.4426950408889634

VMEM_LIMIT = 56 * 1024 * 1024

TM = 512
FF_CHUNK = 256
OUT_CHUNK = 256
DECAY_CHUNK = 256
FOX_TQ = 512
FOX_TKV = 512
DIL_SPAN = BLOCK * max(d for _, d in DIL_GROUPS)
DIL_BLOCKS_IN_FLIGHT = 4


def _params(n_axes, flags=None):
    return pltpu.CompilerParams(dimension_semantics=("arbitrary",) * n_axes,
                                vmem_limit_bytes=VMEM_LIMIT, flags=flags)


def _resident(shape):
    zeros = (0,) * len(shape)
    return pl.BlockSpec(shape, lambda *_: zeros, pipeline_mode=pl.Buffered(1))


def _alibi_slopes():
    h = np.arange(1, N_HEADS_A + 1, dtype=np.float32)
    return np.asarray(2.0 ** (-ALIBI_MAX * h / N_HEADS_A), dtype=np.float32)


def _modulated_norm(x, g, sc, sh):
    y = x * lax.rsqrt(jnp.mean(x * x, axis=-1, keepdims=True) + EPS)
    return (y * g) * (1.0 + sc) + sh


def _dot(a, b):
    return jnp.dot(a, b, preferred_element_type=F32)


def _dot_nt(a, b):
    return lax.dot_general(a, b, (((1,), (1,)), ((), ())), preferred_element_type=F32)


def _split_bf16(x):
    pieces = []
    for _ in range(DECAY_PIECES):
        piece = x.astype(BF16)
        pieces.append(piece)
        x = x - piece.astype(F32)
    return pieces


def _mod_kernel(c_ref, w_ref, b_ref, o_ref):
    c = c_ref[...]
    c_act = (c * jax.nn.sigmoid(c)).astype(BF16)
    o_ref[0] = _dot(c_act, w_ref[0].astype(BF16)) + b_ref[0]


def _modulation(c, w_ada, b_ada):
    depth, d, n = w_ada.shape
    batch = c.shape[0]
    return pl.pallas_call(
        _mod_kernel,
        grid=(depth, n // d),
        in_specs=[pl.BlockSpec((batch, d), lambda l, j: (0, 0)),
                  pl.BlockSpec((1, d, d), lambda l, j: (l, 0, j)),
                  pl.BlockSpec((1, 1, d), lambda l, j: (l, 0, j))],
        out_specs=pl.BlockSpec((1, batch, d), lambda l, j: (l, 0, j)),
        out_shape=jax.ShapeDtypeStruct((depth, batch, n), F32),
        compiler_params=_params(2),
        name="adaln_modulation",
    )(c, w_ada, b_ada.reshape(depth, 1, n))


def _inproj_kernel(x_ref, g_ref, sc_ref, sh_ref, wa_ref, wqt_ref, wvt_ref, wkf_ref,
                   za0_ref, za1_ref, za2_ref, qt_ref, k_ref, vt_ref, fz_ref, stage_ref):
    h = _modulated_norm(x_ref[0], g_ref[...], sc_ref[0], sh_ref[0]).astype(BF16)
    tm = h.shape[0]
    wg = 3 * WIDTH_G
    for g, za_ref in enumerate((za0_ref, za1_ref, za2_ref)):
        dilation = DIL_GROUPS[g][1]
        za = _dot(h, wa_ref[:, g * wg:(g + 1) * wg])
        if dilation == 1:
            za_ref[0, 0] = za.astype(BF16)
            continue
        for c in range(wg // LANES):
            stage_ref[c] = za[:, c * LANES:(c + 1) * LANES]
        for r in range(dilation):
            rows = pl.ds(r, tm // dilation, stride=dilation)
            for c in range(wg // LANES):
                za_ref[0, r, :, c * LANES:(c + 1) * LANES] = stage_ref[c, rows, :].astype(BF16)
    qt = _dot_nt(wqt_ref[...], h).astype(BF16)
    vt = _dot_nt(wvt_ref[...], h).astype(BF16)
    q_tail = (lax.broadcasted_iota(jnp.int32, (Q_ROWS - HEAD_DIM, tm), 0)
              < DECAY_PIECES).astype(BF16)
    v_tail = (lax.broadcasted_iota(jnp.int32, (V_ROWS - HEAD_DIM, tm), 0) == 0).astype(BF16)
    for hd in range(N_HEADS_B):
        heads = slice(hd * HEAD_DIM, (hd + 1) * HEAD_DIM)
        qt_ref[0, hd * Q_ROWS:hd * Q_ROWS + HEAD_DIM, :] = qt[heads]
        qt_ref[0, hd * Q_ROWS + HEAD_DIM:(hd + 1) * Q_ROWS, :] = q_tail
        vt_ref[0, hd * V_ROWS:hd * V_ROWS + HEAD_DIM, :] = vt[heads]
        vt_ref[0, hd * V_ROWS + HEAD_DIM:(hd + 1) * V_ROWS, :] = v_tail
    kf = _dot(h, wkf_ref[...])
    k_ref[0] = kf[:, :WIDTH_B].astype(BF16)
    fz_ref[0] = kf[:, WIDTH_B:]


def _in_projection(x, g, sc, sh, wa, wqt, wvt, wkf):
    batch, seq, d = x.shape
    wg = 3 * WIDTH_G
    tok = lambda w: pl.BlockSpec((1, TM, w), lambda b, i: (b, i, 0))
    tok_t = lambda rows: pl.BlockSpec((1, rows, TM), lambda b, i: (b, 0, i))
    per_batch = pl.BlockSpec((1, 1, d), lambda b, i: (b, 0, 0))
    dils = [dilation for _, dilation in DIL_GROUPS]
    assert all(TM % (dilation * 16) == 0 for dilation in dils)
    sub_major = [pl.BlockSpec((1, dilation, TM // dilation, wg), lambda b, i: (b, 0, i, 0))
                 for dilation in dils]
    return pl.pallas_call(
        _inproj_kernel,
        grid=(batch, seq // TM),
        in_specs=[tok(d), _resident((1, d)), per_batch, per_batch,
                  _resident(wa.shape), _resident(wqt.shape), _resident(wvt.shape),
                  _resident(wkf.shape)],
        out_specs=sub_major + [tok_t(N_HEADS_B * Q_ROWS), tok(WIDTH_B),
                               tok_t(N_HEADS_B * V_ROWS), tok(F_COLS)],
        out_shape=[jax.ShapeDtypeStruct((batch, dilation, seq // dilation, wg), BF16)
                   for dilation in dils]
        + [jax.ShapeDtypeStruct((batch, N_HEADS_B * Q_ROWS, seq), BF16),
           jax.ShapeDtypeStruct((batch, seq, WIDTH_B), BF16),
           jax.ShapeDtypeStruct((batch, N_HEADS_B * V_ROWS, seq), BF16),
           jax.ShapeDtypeStruct((batch, seq, F_COLS), F32)],
        scratch_shapes=[pltpu.VMEM((wg // LANES, TM, LANES), F32)],
        compiler_params=_params(2),
        name="in_projection",
    )(x, g, sc, sh, wa, wqt, wvt, wkf)


def _log_sigmoid(x):
    return jnp.minimum(x, 0.0) - jnp.log1p(jnp.exp(-jnp.abs(x)))


def _decay_kernel(fz_ref, b_ref, tri_ref, o_ref):
    seq = fz_ref.shape[1]
    tri = tri_ref[...]
    carry = jnp.zeros((1, F_COLS), F32)
    for c in range(seq // DECAY_CHUNK):
        rows = slice(c * DECAY_CHUNK, (c + 1) * DECAY_CHUNK)
        lf = _log_sigmoid(fz_ref[0, rows, :] + b_ref[...])
        cs = sum(_dot(tri, piece) for piece in _split_bf16(lf)) + carry
        o_ref[0, rows, :] = cs
        carry = cs[DECAY_CHUNK - 1:DECAY_CHUNK, :]


def _decay_cumsum(fz, b_forget):
    batch, seq, _ = fz.shape
    bias = jnp.zeros((1, F_COLS), F32).at[0, :N_HEADS_B].set(b_forget.astype(F32))
    idx = np.arange(DECAY_CHUNK)
    tri = jnp.asarray(idx[None, :] <= idx[:, None], dtype=BF16)
    return pl.pallas_call(
        _decay_kernel,
        grid=(batch,),
        in_specs=[pl.BlockSpec((1, seq, F_COLS), lambda b: (b, 0, 0)),
                  _resident(bias.shape), _resident(tri.shape)],
        out_specs=pl.BlockSpec((1, seq, F_COLS), lambda b: (b, 0, 0)),
        out_shape=jax.ShapeDtypeStruct((batch, seq, F_COLS), F32),
        compiler_params=_params(1),
        name="forget_decay_cumsum",
    )(fz, bias, tri)


def _band_bias():
    qi = np.arange(BLOCK)[:, None]
    kj = np.arange(2 * BLOCK)[None, :]
    dist = qi + BLOCK - kj
    in_band = (dist >= 0) & (dist <= BLOCK)
    slopes = _alibi_slopes()
    bias = np.empty((N_HEADS_A, BLOCK, 2 * BLOCK), np.float32)
    for g, (window, dilation) in enumerate(DIL_GROUPS):
        assert window // dilation == BLOCK
        for h in range(HEADS_PER_GROUP):
            head = g * HEADS_PER_GROUP + h
            alibi = -slopes[head] * (dist * dilation).astype(np.float32)
            bias[head] = np.where(in_band, alibi, np.float32(NEG))
    return bias


def _unroll(trips):
    return next(u for u in (DIL_BLOCKS_IN_FLIGHT, 3, 2, 1) if trips % u == 0)


def _dilated_kernel(c0_ref, p0_ref, c1_ref, p1_ref, c2_ref, p2_ref, bias_ref, o_ref,
                    state_ref, gather_ref, stage_ref):
    span = pl.program_id(1)
    kj = lax.broadcasted_iota(jnp.int32, (BLOCK, 2 * BLOCK), 1)
    no_prev = jnp.where(kj >= jnp.where(span == 0, BLOCK, 0), 0.0, NEG)
    low_half = lax.broadcasted_iota(jnp.int32, (BLOCK, LANES), 1) < HEAD_DIM

    def own_half(h):
        return low_half if h % 2 == 0 else jnp.logical_not(low_half)

    def attend(g, q_of, k_of, v_of, rows, first):
        for h in range(HEADS_PER_GROUP):
            q = q_of(h // 2)
            q = jnp.where(own_half(h), q, jnp.zeros_like(q))
            z = _dot_nt(q, k_of(h // 2)) + bias_ref[g * HEADS_PER_GROUP + h]
            if first:
                z = z + no_prev
            m = jnp.max(z, axis=-1, keepdims=True)
            p = jnp.exp(z - m)
            l = jnp.sum(p, axis=-1, keepdims=True)
            out = _dot(p.astype(BF16), v_of(h // 2)) / l
            state_ref[g, h, rows, :] = jnp.where(own_half(h), out, m + jnp.log(l))

    for g, (cur_ref, prev_ref) in enumerate(((c0_ref, p0_ref), (c1_ref, p1_ref),
                                             (c2_ref, p2_ref))):
        dilation = DIL_GROUPS[g][1]
        class_rows = cur_ref.shape[2]
        nblk = class_rows // BLOCK

        def cols(part, pair):
            return slice(part * WIDTH_G + pair * LANES, part * WIDTH_G + (pair + 1) * LANES)

        def residue(r, carry, g=g, cur_ref=cur_ref, prev_ref=prev_ref, nblk=nblk,
                    cols=cols, class_rows=class_rows):
            def state_rows(n):
                return pl.ds(pl.multiple_of(r * class_rows + n * BLOCK, BLOCK), BLOCK)

            def with_prev(part):
                return lambda pair: jnp.concatenate(
                    [prev_ref[0, r, :, cols(part, pair)],
                     cur_ref[0, r, :BLOCK, cols(part, pair)]], axis=0)

            attend(g, lambda pair: cur_ref[0, r, :BLOCK, cols(0, pair)], with_prev(1),
                   with_prev(2), state_rows(0), first=True)

            def block(n, c):
                row0 = pl.multiple_of(n * BLOCK, BLOCK)
                keys = pl.ds(row0 - BLOCK, 2 * BLOCK)
                attend(g, lambda pair: cur_ref[0, r, pl.ds(row0, BLOCK), cols(0, pair)],
                       lambda pair: cur_ref[0, r, keys, cols(1, pair)],
                       lambda pair: cur_ref[0, r, keys, cols(2, pair)],
                       state_rows(n), first=False)
                return c

            if nblk > 1:
                lax.fori_loop(1, nblk, block, 0, unroll=_unroll(nblk - 1))
            return carry

        if dilation == 1:
            residue(0, 0)
        else:
            lax.fori_loop(0, dilation, residue, 0, unroll=_unroll(dilation) if nblk == 1 else 1)

    mid = DIL_GROUPS[1][1]
    assert [d for _, d in DIL_GROUPS] == [1, mid, mid * mid] and BLOCK % mid == 0
    mid_rows = DIL_SPAN // mid
    wide_rows = DIL_SPAN // (mid * mid)
    run = BLOCK // mid

    def merge(idx, carry):
        r = idx // (mid_rows // BLOCK)
        n = idx % (mid_rows // BLOCK)
        tokens = pl.ds(mid * BLOCK * n + r, BLOCK, stride=mid)
        halves = []
        for h in range(HEADS_PER_GROUP):
            for a in range(mid):
                gather_ref[h, pl.ds(a, run, stride=mid), :] = state_ref[
                    2, h, pl.ds(pl.multiple_of((mid * a + r) * wide_rows + run * n, run), run), :]
            tiles = (state_ref[0, h, tokens, :],
                     state_ref[1, h, pl.ds(pl.multiple_of(r * mid_rows + n * BLOCK, BLOCK), BLOCK), :],
                     gather_ref[h])
            lses = [pltpu.roll(t, HEAD_DIM, 1) for t in tiles]
            top = jnp.maximum(jnp.maximum(lses[0], lses[1]), lses[2])
            weights = [jnp.exp(lse - top) for lse in lses]
            num = sum(w * t for w, t in zip(weights, tiles))
            halves.append(num / sum(weights))
        for pair in range(HEADS_PER_GROUP // 2):
            stage_ref[pair, tokens, :] = jnp.where(low_half, halves[2 * pair], halves[2 * pair + 1])
        return carry

    lax.fori_loop(0, DIL_SPAN // BLOCK, merge, 0)
    for pair in range(HEADS_PER_GROUP // 2):
        o_ref[0, :, pair * LANES:(pair + 1) * LANES] = stage_ref[pair].astype(o_ref.dtype)


def _dilated_mixture(zs):
    batch, _, seq, wg = zs[0].shape
    assert seq % DIL_SPAN == 0
    in_specs, operands = [], []
    for g, (_, dilation) in enumerate(DIL_GROUPS):
        rows = DIL_SPAN // dilation
        assert rows % BLOCK == 0 and zs[g].shape == (batch, dilation, seq // dilation, wg)
        nblk = rows // BLOCK
        in_specs.append(pl.BlockSpec((1, dilation, rows, wg), lambda b, t: (b, 0, t, 0)))
        in_specs.append(pl.BlockSpec(
            (1, dilation, BLOCK, wg),
            lambda b, t, nblk=nblk: (b, 0, jnp.maximum(t * nblk - 1, 0), 0)))
        operands += [zs[g], zs[g]]
    bias = jnp.asarray(_band_bias())
    return pl.pallas_call(
        _dilated_kernel,
        grid=(batch, seq // DIL_SPAN),
        in_specs=in_specs + [_resident(bias.shape)],
        out_specs=pl.BlockSpec((1, DIL_SPAN, WIDTH_G), lambda b, t: (b, t, 0)),
        out_shape=jax.ShapeDtypeStruct((batch, seq, WIDTH_G), BF16),
        scratch_shapes=[pltpu.VMEM((N_GROUPS, HEADS_PER_GROUP, DIL_SPAN, LANES), F32),
                        pltpu.VMEM((HEADS_PER_GROUP, BLOCK, LANES), F32),
                        pltpu.VMEM((HEADS_PER_GROUP // 2, DIL_SPAN, LANES), F32)],
        compiler_params=_params(2),
        name="dilated_attention",
    )(*operands, bias)


def _fox_place():
    place = np.zeros((WIDTH_B + DECAY_PIECES * F_COLS, N_HEADS_B * LANES), np.float32)
    for c in range(WIDTH_B):
        place[c, (c // HEAD_DIM) * LANES + c % HEAD_DIM] = 1.0
    for i in range(DECAY_PIECES):
        for h in range(N_HEADS_B):
            place[WIDTH_B + i * F_COLS + h, h * LANES + HEAD_DIM + i] = 1.0
    return place


def _fox_kernel(qt_ref, k_ref, vt_ref, f_ref, place_ref, o_ref, kaug_ref, acc_ref, s_ref):
    tq = qt_ref.shape[2]
    seq = k_ref.shape[1]
    qb = pl.program_id(1)
    q0 = qb * tq
    n_full = qb
    key_minus_query = (lax.broadcasted_iota(jnp.int32, (FOX_TKV, tq), 0)
                       - lax.broadcasted_iota(jnp.int32, (FOX_TKV, tq), 1))

    @pl.when(qb == 0)
    def _():
        for c in range(seq // FOX_TKV):
            rows = slice(c * FOX_TKV, (c + 1) * FOX_TKV)
            spread = _dot(k_ref[0, rows, :], place_ref[:WIDTH_B])
            for i, piece in enumerate(_split_bf16(f_ref[0, rows, :] * (-LOG2E))):
                at = WIDTH_B + i * F_COLS
                spread = spread + _dot(piece, place_ref[at:at + F_COLS])
            kaug_ref[rows, :] = spread.astype(BF16)

    acc_ref[...] = jnp.zeros_like(acc_ref)

    def scores(h, k0, masked):
        keys = pl.ds(pl.multiple_of(k0, FOX_TKV), FOX_TKV)
        s = _dot(kaug_ref[keys, h * LANES:(h + 1) * LANES],
                 qt_ref[0, h * Q_ROWS:(h + 1) * Q_ROWS, :])
        if masked:
            s = jnp.where(key_minus_query <= q0 - k0, s, NEG)
        s_ref[h % 2] = s
        return jnp.max(s, axis=0, keepdims=True)

    def kv_block(j, carry, diagonal):
        ms, top_next = carry
        k0 = pl.multiple_of(j * FOX_TKV, FOX_TKV)
        new_ms = []
        for h in range(N_HEADS_B):
            m_new = jnp.maximum(ms[h], top_next)
            if h + 1 < N_HEADS_B:
                top_next = scores(h + 1, k0, masked=diagonal)
            elif not diagonal:
                top_next = scores(0, k0 + FOX_TKV, masked=True)
            alpha = jnp.exp2(ms[h] - m_new)
            p = jnp.exp2(s_ref[h % 2] - m_new).astype(BF16)
            vt = vt_ref[0, h * V_ROWS:(h + 1) * V_ROWS, pl.ds(k0, FOX_TKV)]
            acc_ref[h] = acc_ref[h] * alpha + _dot(vt, p)
            new_ms.append(m_new)
        return tuple(new_ms), top_next

    ms = tuple(jnp.full((1, tq), NEG, F32) for _ in range(N_HEADS_B))
    carry = (ms, scores(0, 0, masked=True))
    carry = lax.fori_loop(0, n_full, functools.partial(kv_block, diagonal=False), carry)
    kv_block(n_full, carry, diagonal=True)
    pad = jnp.zeros((LANES - HEAD_DIM, tq), F32)
    for h in range(N_HEADS_B):
        acc = acc_ref[h]
        out_t = jnp.concatenate([acc[:HEAD_DIM] / acc[HEAD_DIM:HEAD_DIM + 1], pad], axis=0)
        o_ref[0, :, h * HEAD_DIM:(h + 1) * HEAD_DIM] = (
            out_t.T[:, :HEAD_DIM].astype(o_ref.dtype))


def _forgetting_attention(qt, k, vt, decay):
    batch, seq, _ = k.shape
    place = jnp.asarray(_fox_place(), dtype=BF16)
    return pl.pallas_call(
        _fox_kernel,
        grid=(batch, seq // FOX_TQ),
        in_specs=[pl.BlockSpec((1, N_HEADS_B * Q_ROWS, FOX_TQ), lambda b, i: (b, 0, i)),
                  pl.BlockSpec((1, seq, WIDTH_B), lambda b, i: (b, 0, 0)),
                  pl.BlockSpec((1, N_HEADS_B * V_ROWS, seq), lambda b, i: (b, 0, 0)),
                  pl.BlockSpec((1, seq, F_COLS), lambda b, i: (b, 0, 0)),
                  _resident(place.shape)],
        out_specs=pl.BlockSpec((1, FOX_TQ, WIDTH_B), lambda b, i: (b, i, 0)),
        out_shape=jax.ShapeDtypeStruct((batch, seq, WIDTH_B), BF16),
        scratch_shapes=[pltpu.VMEM((seq, N_HEADS_B * LANES), BF16),
                        pltpu.VMEM((N_HEADS_B, V_ROWS, FOX_TQ), F32),
                        pltpu.VMEM((2, FOX_TKV, FOX_TQ), F32)],
        compiler_params=_params(2),
        name="forgetting_attention",
    )(qt, k, vt, decay, place)


def _outproj_kernel(x_ref, g_ref, sc_ref, sh_ref, gate_ref, ya_ref, yb_ref,
                    wg_ref, wua_ref, wub_ref, wo_ref, o_ref, merged_ref):
    x = x_ref[0]
    d = x.shape[-1]
    h = _modulated_norm(x, g_ref[...], sc_ref[0], sh_ref[0]).astype(BF16)
    ya = ya_ref[0]
    yb = yb_ref[0]
    for c in range(d // OUT_CHUNK):
        cs = slice(c * OUT_CHUNK, (c + 1) * OUT_CHUNK)
        cs_b = slice(d + c * OUT_CHUNK, d + (c + 1) * OUT_CHUNK)
        gate_a = jax.nn.sigmoid(_dot(h, wg_ref[:, cs]))
        gate_b = jax.nn.sigmoid(_dot(h, wg_ref[:, cs_b]))
        merged = gate_a * _dot(ya, wua_ref[:, cs]) + gate_b * _dot(yb, wub_ref[:, cs])
        merged_ref[:, cs] = merged.astype(BF16)
    o_ref[0] = x + gate_ref[0] * _dot(merged_ref[...], wo_ref[...])


def _out_projection(x, g, sc, sh, gate, ya, yb, wg, wua, wub, wo):
    batch, seq, d = x.shape
    tok = lambda w: pl.BlockSpec((1, TM, w), lambda b, i: (b, i, 0))
    per_batch = pl.BlockSpec((1, 1, d), lambda b, i: (b, 0, 0))
    return pl.pallas_call(
        _outproj_kernel,
        grid=(batch, seq // TM),
        in_specs=[tok(d), _resident((1, d)), per_batch, per_batch, per_batch,
                  tok(WIDTH_G), tok(WIDTH_B),
                  _resident(wg.shape), _resident(wua.shape), _resident(wub.shape),
                  _resident(wo.shape)],
        out_specs=tok(d),
        out_shape=jax.ShapeDtypeStruct((batch, seq, d), F32),
        scratch_shapes=[pltpu.VMEM((TM, d), BF16)],
        compiler_params=_params(2),
        name="out_projection",
    )(x, g, sc, sh, gate, ya, yb, wg, wua, wub, wo)


def _ffn_kernel(*refs, final_norm):
    x_ref, g_ref, sc_ref, sh_ref, gate_ref, win_ref, wout_ref = refs[:7]
    refs = refs[7:]
    if final_norm:
        gf_ref = refs[0]
        refs = refs[1:]
    o_ref, h_ref, acc_ref = refs
    x = x_ref[0]
    h_ref[...] = _modulated_norm(x, g_ref[...], sc_ref[0], sh_ref[0]).astype(BF16)
    for c in range(D_FF // FF_CHUNK):
        cs = slice(c * FF_CHUNK, (c + 1) * FF_CHUNK)
        cs_up = slice(D_FF + c * FF_CHUNK, D_FF + (c + 1) * FF_CHUNK)
        h = h_ref[...]
        gt = _dot(h, win_ref[:, cs])
        up = _dot(h, win_ref[:, cs_up])
        act = ((gt * jax.nn.sigmoid(gt)) * up).astype(BF16)
        part = _dot(act, wout_ref[cs, :])
        if c == 0:
            acc_ref[...] = part
        else:
            acc_ref[...] += part
    y = x + gate_ref[0] * acc_ref[...]
    if final_norm:
        y = (y * lax.rsqrt(jnp.mean(y * y, axis=-1, keepdims=True) + EPS)) * gf_ref[...]
    o_ref[0] = y


def _ffn(x, g, sc, sh, gate, win, wout, g_final):
    batch, seq, d = x.shape
    final_norm = g_final is not None
    tok = pl.BlockSpec((1, TM, d), lambda b, i: (b, i, 0))
    per_batch = pl.BlockSpec((1, 1, d), lambda b, i: (b, 0, 0))
    in_specs = [tok, _resident((1, d)), per_batch, per_batch, per_batch,
                _resident(win.shape), _resident(wout.shape)]
    operands = [x, g, sc, sh, gate, win, wout]
    if final_norm:
        in_specs.append(_resident((1, d)))
        operands.append(g_final)
    return pl.pallas_call(
        functools.partial(_ffn_kernel, final_norm=final_norm),
        grid=(batch, seq // TM),
        in_specs=in_specs,
        out_specs=tok,
        out_shape=jax.ShapeDtypeStruct((batch, seq, d), F32),
        scratch_shapes=[pltpu.VMEM((TM, d), BF16), pltpu.VMEM((TM, d), F32)],
        compiler_params=_params(2),
        name="swiglu_ffn",
    )(*operands)


def _split_w_in(w_in):
    qa, ka, va, qb, kb, vb, fz, gza, gzb = jnp.split(
        w_in, np.cumsum((WIDTH_A, WIDTH_A, WIDTH_A, WIDTH_B, WIDTH_B, WIDTH_B, N_HEADS_B,
                         D_MODEL))[:].tolist(), axis=-1)
    cols = []
    for g in range(N_GROUPS):
        gs = slice(g * WIDTH_G, (g + 1) * WIDTH_G)
        cols += [qa[:, gs] * QK_SCALE, ka[:, gs], va[:, gs]]
    wa = jnp.concatenate(cols, axis=-1).astype(BF16)
    pad = jnp.zeros((w_in.shape[0], F_COLS - N_HEADS_B), w_in.dtype)
    wqt = (qb * (QK_SCALE * LOG2E)).T.astype(BF16)
    wvt = vb.T.astype(BF16)
    wkf = jnp.concatenate([kb, fz, pad], axis=-1).astype(BF16)
    wg = jnp.concatenate([gza, gzb], axis=-1).astype(BF16)
    return wa, wqt, wvt, wkf, wg


def kernel(x, c, w_ada, b_ada, norm_mix, w_in, b_forget, w_up_a, w_up_b, w_out,
           norm_ffn, w_ffn_in, w_ffn_out, norm_final):
    depth = w_ada.shape[0]
    batch, seq, d = x.shape
    assert seq % TM == 0 and seq % FOX_TQ == 0 and FOX_TQ == FOX_TKV
    mod = _modulation(c, w_ada, b_ada).reshape(depth, batch, N_MOD, 1, d)
    for l in range(depth):
        sh1, sc1, g1, sh2, sc2, g2 = (mod[l, :, j] for j in range(N_MOD))
        wa, wqt, wvt, wkf, wg = _split_w_in(w_in[l])
        g_mix = norm_mix[l].reshape(1, d)
        za0, za1, za2, qt, k, vt, fz = _in_projection(x, g_mix, sc1, sh1, wa, wqt, wvt, wkf)
        decay = _decay_cumsum(fz, b_forget[l])
        ya = _dilated_mixture((za0, za1, za2))
        yb = _forgetting_attention(qt, k, vt, decay)
        x = _out_projection(x, g_mix, sc1, sh1, g1, ya, yb, wg,
                            w_up_a[l].astype(BF16), w_up_b[l].astype(BF16),
                            w_out[l].astype(BF16))
        g_final = norm_final.reshape(1, d) if l == depth - 1 else None
        x = _ffn(x, norm_ffn[l].reshape(1, d), sc2, sh2, g2,
                 w_ffn_in[l].astype(BF16), w_ffn_out[l].astype(BF16), g_final)
    return x
```

```python
import functools

import numpy as np
import jax
import jax.numpy as jnp
from jax import lax
from jax.experimental import pallas as pl
from jax.experimental.pallas import tpu as pltpu

F32 = jnp.float32
BF16 = jnp.bfloat16

D_MODEL = 1024
HEAD_DIM = 64
DIL_GROUPS = ((128, 1), (512, 4), (2048, 16))
HEADS_PER_GROUP = 4
N_GROUPS = len(DIL_GROUPS)
N_HEADS_A = HEADS_PER_GROUP * N_GROUPS
N_HEADS_B = 4
WIDTH_G = HEADS_PER_GROUP * HEAD_DIM
WIDTH_A = N_HEADS_A * HEAD_DIM
WIDTH_B = N_HEADS_B * HEAD_DIM
D_FF = -(-8 * D_MODEL // (3 * 256)) * 256
BLOCK = 128
ALIBI_MAX = 8.0
EPS = 1e-6
N_MOD = 6
QK_SCALE = HEAD_DIM ** -0.5

NEG = -1e30
LANES = 128
F_COLS = LANES
BF16_ROWS = 16
V_ROWS = HEAD_DIM + BF16_ROWS
Q_ROWS = LANES
DECAY_PIECES = 3
LOG2E = 1.4426950408889634

VMEM_LIMIT = 56 * 1024 * 1024

TM = 512
FF_CHUNK = 256
OUT_CHUNK = 256
DECAY_CHUNK = 256
FOX_TQ = 512
FOX_TKV = 512
DIL_SPAN = BLOCK * max(d for _, d in DIL_GROUPS)
DIL_BLOCKS_IN_FLIGHT = 8


def _params(n_axes, flags=None):
    return pltpu.CompilerParams(dimension_semantics=("arbitrary",) * n_axes,
                                vmem_limit_bytes=VMEM_LIMIT, flags=flags)


def _resident(shape):
    zeros = (0,) * len(shape)
    return pl.BlockSpec(shape, lambda *_: zeros, pipeline_mode=pl.Buffered(1))


def _alibi_slopes():
    h = np.arange(1, N_HEADS_A + 1, dtype=np.float32)
    return np.asarray(2.0 ** (-ALIBI_MAX * h / N_HEADS_A), dtype=np.float32)


def _modulated_norm(x, g, sc, sh):
    y = x * lax.rsqrt(jnp.mean(x * x, axis=-1, keepdims=True) + EPS)
    return (y * g) * (1.0 + sc) + sh


def _dot(a, b):
    return jnp.dot(a, b, preferred_element_type=F32)


def _dot_nt(a, b):
    return lax.dot_general(a, b, (((1,), (1,)), ((), ())), preferred_element_type=F32)


def _split_bf16(x):
    pieces = []
    for _ in range(DECAY_PIECES):
        piece = x.astype(BF16)
        pieces.append(piece)
        x = x - piece.astype(F32)
    return pieces


def _mod_kernel(c_ref, w_ref, b_ref, o_ref):
    c = c_ref[...]
    c_act = (c * jax.nn.sigmoid(c)).astype(BF16)
    o_ref[0] = _dot(c_act, w_ref[0].astype(BF16)) + b_ref[0]


def _modulation(c, w_ada, b_ada):
    depth, d, n = w_ada.shape
    batch = c.shape[0]
    return pl.pallas_call(
        _mod_kernel,
        grid=(depth, n // d),
        in_specs=[pl.BlockSpec((batch, d), lambda l, j: (0, 0)),
                  pl.BlockSpec((1, d, d), lambda l, j: (l, 0, j)),
                  pl.BlockSpec((1, 1, d), lambda l, j: (l, 0, j))],
        out_specs=pl.BlockSpec((1, batch, d), lambda l, j: (l, 0, j)),
        out_shape=jax.ShapeDtypeStruct((depth, batch, n), F32),
        compiler_params=_params(2),
        name="adaln_modulation",
    )(c, w_ada, b_ada.reshape(depth, 1, n))


def _inproj_kernel(x_ref, g_ref, sc_ref, sh_ref, wa_ref, wqt_ref, wvt_ref, wkf_ref,
                   za0_ref, za1_ref, za2_ref, qt_ref, k_ref, vt_ref, fz_ref, stage_ref):
    h = _modulated_norm(x_ref[0], g_ref[...], sc_ref[0], sh_ref[0]).astype(BF16)
    tm = h.shape[0]
    wg = 3 * WIDTH_G
    for g, za_ref in enumerate((za0_ref, za1_ref, za2_ref)):
        dilation = DIL_GROUPS[g][1]
        za = _dot(h, wa_ref[:, g * wg:(g + 1) * wg])
        if dilation == 1:
            za_ref[0, 0] = za.astype(BF16)
            continue
        for c in range(wg // LANES):
            stage_ref[c] = za[:, c * LANES:(c + 1) * LANES]
        for r in range(dilation):
            rows = pl.ds(r, tm // dilation, stride=dilation)
            for c in range(wg // LANES):
                za_ref[0, r, :, c * LANES:(c + 1) * LANES] = stage_ref[c, rows, :].astype(BF16)
    qt = _dot_nt(wqt_ref[...], h).astype(BF16)
    vt = _dot_nt(wvt_ref[...], h).astype(BF16)
    q_tail = (lax.broadcasted_iota(jnp.int32, (Q_ROWS - HEAD_DIM, tm), 0)
              < DECAY_PIECES).astype(BF16)
    v_tail = (lax.broadcasted_iota(jnp.int32, (V_ROWS - HEAD_DIM, tm), 0) == 0).astype(BF16)
    for hd in range(N_HEADS_B):
        heads = slice(hd * HEAD_DIM, (hd + 1) * HEAD_DIM)
        qt_ref[0, hd * Q_ROWS:hd * Q_ROWS + HEAD_DIM, :] = qt[heads]
        qt_ref[0, hd * Q_ROWS + HEAD_DIM:(hd + 1) * Q_ROWS, :] = q_tail
        vt_ref[0, hd * V_ROWS:hd * V_ROWS + HEAD_DIM, :] = vt[heads]
        vt_ref[0, hd * V_ROWS + HEAD_DIM:(hd + 1) * V_ROWS, :] = v_tail
    kf = _dot(h, wkf_ref[...])
    k_ref[0] = kf[:, :WIDTH_B].astype(BF16)
    fz_ref[0] = kf[:, WIDTH_B:]


def _in_projection(x, g, sc, sh, wa, wqt, wvt, wkf):
    batch, seq, d = x.shape
    wg = 3 * WIDTH_G
    tok = lambda w: pl.BlockSpec((1, TM, w), lambda b, i: (b, i, 0))
    tok_t = lambda rows: pl.BlockSpec((1, rows, TM), lambda b, i: (b, 0, i))
    per_batch = pl.BlockSpec((1, 1, d), lambda b, i: (b, 0, 0))
    dils = [dilation for _, dilation in DIL_GROUPS]
    assert all(TM % (dilation * BF16_ROWS) == 0 for dilation in dils)
    sub_major = [pl.BlockSpec((1, dilation, TM // dilation, wg), lambda b, i: (b, 0, i, 0))
                 for dilation in dils]
    return pl.pallas_call(
        _inproj_kernel,
        grid=(batch, seq // TM),
        in_specs=[tok(d), _resident((1, d)), per_batch, per_batch,
                  _resident(wa.shape), _resident(wqt.shape), _resident(wvt.shape),
                  _resident(wkf.shape)],
        out_specs=sub_major + [tok_t(N_HEADS_B * Q_ROWS), tok(WIDTH_B),
                               tok_t(N_HEADS_B * V_ROWS), tok(F_COLS)],
        out_shape=[jax.ShapeDtypeStruct((batch, dilation, seq // dilation, wg), BF16)
                   for dilation in dils]
        + [jax.ShapeDtypeStruct((batch, N_HEADS_B * Q_ROWS, seq), BF16),
           jax.ShapeDtypeStruct((batch, seq, WIDTH_B), BF16),
           jax.ShapeDtypeStruct((batch, N_HEADS_B * V_ROWS, seq), BF16),
           jax.ShapeDtypeStruct((batch, seq, F_COLS), F32)],
        scratch_shapes=[pltpu.VMEM((wg // LANES, TM, LANES), F32)],
        compiler_params=_params(2),
        name="in_projection",
    )(x, g, sc, sh, wa, wqt, wvt, wkf)


def _log_sigmoid(x):
    return jnp.minimum(x, 0.0) - jnp.log1p(jnp.exp(-jnp.abs(x)))


def _decay_kernel(fz_ref, b_ref, tri_ref, o_ref):
    seq = fz_ref.shape[1]
    tri = tri_ref[...]
    carry = jnp.zeros((1, F_COLS), F32)
    for c in range(seq // DECAY_CHUNK):
        rows = slice(c * DECAY_CHUNK, (c + 1) * DECAY_CHUNK)
        lf = _log_sigmoid(fz_ref[0, rows, :] + b_ref[...])
        cs = sum(_dot(tri, piece) for piece in _split_bf16(lf)) + carry
        o_ref[0, rows, :] = cs
        carry = cs[DECAY_CHUNK - 1:DECAY_CHUNK, :]


def _decay_cumsum(fz, b_forget):
    batch, seq, _ = fz.shape
    bias = jnp.zeros((1, F_COLS), F32).at[0, :N_HEADS_B].set(b_forget.astype(F32))
    idx = np.arange(DECAY_CHUNK)
    tri = jnp.asarray(idx[None, :] <= idx[:, None], dtype=BF16)
    return pl.pallas_call(
        _decay_kernel,
        grid=(batch,),
        in_specs=[pl.BlockSpec((1, seq, F_COLS), lambda b: (b, 0, 0)),
                  _resident(bias.shape), _resident(tri.shape)],
        out_specs=pl.BlockSpec((1, seq, F_COLS), lambda b: (b, 0, 0)),
        out_shape=jax.ShapeDtypeStruct((batch, seq, F_COLS), F32),
        compiler_params=_params(1),
        name="forget_decay_cumsum",
    )(fz, bias, tri)


def _band_bias():
    qi = np.arange(BLOCK)[:, None]
    kj = np.arange(2 * BLOCK)[None, :]
    dist = qi + BLOCK - kj
    in_band = (dist >= 0) & (dist <= BLOCK)
    slopes = _alibi_slopes()
    bias = np.empty((N_HEADS_A, BLOCK, 2 * BLOCK), np.float32)
    for g, (window, dilation) in enumerate(DIL_GROUPS):
        assert window // dilation == BLOCK
        for h in range(HEADS_PER_GROUP):
            head = g * HEADS_PER_GROUP + h
            alibi = -slopes[head] * (dist * dilation).astype(np.float32)
            bias[head] = np.where(in_band, alibi * np.float32(LOG2E), np.float32(NEG))
    return bias.reshape(N_HEADS_A // 2, 2 * BLOCK, 2 * BLOCK)


def _unroll(trips, blocks_per_trip=1):
    fits = [u for u in range(1, trips + 1)
            if trips % u == 0 and u * blocks_per_trip <= DIL_BLOCKS_IN_FLIGHT]
    return max(fits, default=1)


def _dilated_kernel(c0_ref, p0_ref, c1_ref, p1_ref, c2_ref, p2_ref, bias_ref, o_ref,
                    state_ref, gather_ref, stage_ref, first_bias_ref):
    span = pl.program_id(1)
    pairs = HEADS_PER_GROUP // 2
    kj = lax.broadcasted_iota(jnp.int32, (2 * BLOCK, 2 * BLOCK), 1)
    no_prev = jnp.where(kj >= jnp.where(span == 0, BLOCK, 0), 0.0, NEG)
    for slab in range(N_GROUPS * pairs):
        first_bias_ref[slab] = bias_ref[slab] + no_prev
    low_half = lax.broadcasted_iota(jnp.int32, (BLOCK, LANES), 1) < HEAD_DIM
    own_half = jnp.concatenate([low_half, jnp.logical_not(low_half)], axis=0)

    def attend(g, q_of, k_of, v_of, rows, first):
        for pair in range(pairs):
            q = q_of(pair)
            q = jnp.where(own_half, jnp.concatenate([q, q], axis=0), jnp.zeros((), q.dtype))
            bias = (first_bias_ref if first else bias_ref)[g * pairs + pair]
            z = _dot_nt(q, k_of(pair)) + bias
            m = jnp.max(z, axis=-1, keepdims=True)
            p = jnp.exp2(z - m)
            l = jnp.sum(p, axis=-1, keepdims=True)
            out = _dot(p.astype(BF16), v_of(pair)) / l
            lse = m + jnp.log2(l)
            state_ref[g, pair, 0, rows, :] = jnp.where(low_half, out[:BLOCK], out[BLOCK:])
            state_ref[g, pair, 1, rows, :] = jnp.where(low_half, lse[:BLOCK], lse[BLOCK:])

    for g, (cur_ref, prev_ref) in enumerate(((c0_ref, p0_ref), (c1_ref, p1_ref),
                                             (c2_ref, p2_ref))):
        dilation = DIL_GROUPS[g][1]
        class_rows = cur_ref.shape[2]
        nblk = class_rows // BLOCK

        def cols(part, pair):
            return slice(part * WIDTH_G + pair * LANES, part * WIDTH_G + (pair + 1) * LANES)

        def residue(r, carry, g=g, cur_ref=cur_ref, prev_ref=prev_ref, nblk=nblk,
                    cols=cols, class_rows=class_rows):
            def state_rows(n):
                return pl.ds(pl.multiple_of(r * class_rows + n * BLOCK, BLOCK), BLOCK)

            def with_prev(part):
                return lambda pair: jnp.concatenate(
                    [prev_ref[0, r, :, cols(part, pair)],
                     cur_ref[0, r, :BLOCK, cols(part, pair)]], axis=0)

            attend(g, lambda pair: cur_ref[0, r, :BLOCK, cols(0, pair)], with_prev(1),
                   with_prev(2), state_rows(0), first=True)

            def block(n, c):
                row0 = pl.multiple_of(n * BLOCK, BLOCK)
                keys = pl.ds(row0 - BLOCK, 2 * BLOCK)
                attend(g, lambda pair: cur_ref[0, r, pl.ds(row0, BLOCK), cols(0, pair)],
                       lambda pair: cur_ref[0, r, keys, cols(1, pair)],
                       lambda pair: cur_ref[0, r, keys, cols(2, pair)],
                       state_rows(n), first=False)
                return c

            if nblk > 1:
                lax.fori_loop(1, nblk, block, 0, unroll=_unroll(nblk - 1))
            return carry

        if dilation == 1:
            residue(0, 0)
        else:
            lax.fori_loop(0, dilation, residue, 0, unroll=_unroll(dilation, nblk))

    mid = DIL_GROUPS[1][1]
    assert [d for _, d in DIL_GROUPS] == [1, mid, mid * mid] and BLOCK % mid == 0
    mid_rows = DIL_SPAN // mid
    wide_rows = DIL_SPAN // (mid * mid)
    run = BLOCK // mid

    def merge(idx, carry):
        r = idx // (mid_rows // BLOCK)
        n = idx % (mid_rows // BLOCK)
        tokens = pl.ds(mid * BLOCK * n + r, BLOCK, stride=mid)
        mid_tokens = pl.ds(pl.multiple_of(r * mid_rows + n * BLOCK, BLOCK), BLOCK)
        for pair in range(pairs):
            def tiles(kind):
                for a in range(mid):
                    wide = pl.ds(pl.multiple_of((mid * a + r) * wide_rows + run * n, run), run)
                    gather_ref[pair, kind, pl.ds(a, run, stride=mid), :] = (
                        state_ref[2, pair, kind, wide, :])
                return (state_ref[0, pair, kind, tokens, :],
                        state_ref[1, pair, kind, mid_tokens, :], gather_ref[pair, kind])

            lses = tiles(1)
            top = jnp.maximum(jnp.maximum(lses[0], lses[1]), lses[2])
            weights = [jnp.exp2(lse - top) for lse in lses]
            num = sum(w * out for w, out in zip(weights, tiles(0)))
            stage_ref[pair, tokens, :] = num / sum(weights)
        return carry

    lax.fori_loop(0, DIL_SPAN // BLOCK, merge, 0)
    for pair in range(HEADS_PER_GROUP // 2):
        o_ref[0, :, pair * LANES:(pair + 1) * LANES] = stage_ref[pair].astype(o_ref.dtype)


def _dilated_mixture(zs):
    batch, _, seq, wg = zs[0].shape
    assert seq % DIL_SPAN == 0
    in_specs, operands = [], []
    for g, (_, dilation) in enumerate(DIL_GROUPS):
        rows = DIL_SPAN // dilation
        assert rows % BLOCK == 0 and zs[g].shape == (batch, dilation, seq // dilation, wg)
        nblk = rows // BLOCK
        in_specs.append(pl.BlockSpec((1, dilation, rows, wg), lambda b, t: (b, 0, t, 0)))
        in_specs.append(pl.BlockSpec(
            (1, dilation, BLOCK, wg),
            lambda b, t, nblk=nblk: (b, 0, jnp.maximum(t * nblk - 1, 0), 0)))
        operands += [zs[g], zs[g]]
    bias = jnp.asarray(_band_bias())
    return pl.pallas_call(
        _dilated_kernel,
        grid=(batch, seq // DIL_SPAN),
        in_specs=in_specs + [_resident(bias.shape)],
        out_specs=pl.BlockSpec((1, DIL_SPAN, WIDTH_G), lambda b, t: (b, t, 0)),
        out_shape=jax.ShapeDtypeStruct((batch, seq, WIDTH_G), BF16),
        scratch_shapes=[pltpu.VMEM((N_GROUPS, HEADS_PER_GROUP // 2, 2, DIL_SPAN, LANES), F32),
                        pltpu.VMEM((HEADS_PER_GROUP // 2, 2, BLOCK, LANES), F32),
                        pltpu.VMEM((HEADS_PER_GROUP // 2, DIL_SPAN, LANES), F32),
                        pltpu.VMEM(bias.shape, F32)],
        compiler_params=_params(2),
        name="dilated_attention",
    )(*operands, bias)


def _fox_place():
    place = np.zeros((WIDTH_B + DECAY_PIECES * F_COLS, N_HEADS_B * LANES), np.float32)
    for c in range(WIDTH_B):
        place[c, (c // HEAD_DIM) * LANES + c % HEAD_DIM] = 1.0
    for i in range(DECAY_PIECES):
        for h in range(N_HEADS_B):
            place[WIDTH_B + i * F_COLS + h, h * LANES + HEAD_DIM + i] = 1.0
    return place


def _fox_kernel(qt_ref, k_ref, vt_ref, f_ref, place_ref, o_ref, kaug_ref, acc_ref, s_ref):
    tq = qt_ref.shape[2]
    seq = k_ref.shape[1]
    qb = pl.program_id(1)
    q0 = qb * tq
    n_full = qb
    key_minus_query = (lax.broadcasted_iota(jnp.int32, (FOX_TKV, tq), 0)
                       - lax.broadcasted_iota(jnp.int32, (FOX_TKV, tq), 1))

    @pl.when(qb == 0)
    def _():
        for c in range(seq // FOX_TKV):
            rows = slice(c * FOX_TKV, (c + 1) * FOX_TKV)
            spread = _dot(k_ref[0, rows, :], place_ref[:WIDTH_B])
            for i, piece in enumerate(_split_bf16(f_ref[0, rows, :] * (-LOG2E))):
                at = WIDTH_B + i * F_COLS
                spread = spread + _dot(piece, place_ref[at:at + F_COLS])
            kaug_ref[rows, :] = spread.astype(BF16)

    acc_ref[...] = jnp.zeros_like(acc_ref)

    def scores(h, k0, masked):
        keys = pl.ds(pl.multiple_of(k0, FOX_TKV), FOX_TKV)
        s = _dot(kaug_ref[keys, h * LANES:(h + 1) * LANES],
                 qt_ref[0, h * Q_ROWS:(h + 1) * Q_ROWS, :])
        if masked:
            s = jnp.where(key_minus_query <= q0 - k0, s, NEG)
        s_ref[h % 2] = s
        return jnp.max(s, axis=0, keepdims=True)

    def kv_block(j, carry, diagonal):
        ms, top_next = carry
        k0 = pl.multiple_of(j * FOX_TKV, FOX_TKV)
        new_ms = []
        for h in range(N_HEADS_B):
            m_new = jnp.maximum(ms[h], top_next)
            if h + 1 < N_HEADS_B:
                top_next = scores(h + 1, k0, masked=diagonal)
            elif not diagonal:
                top_next = scores(0, k0 + FOX_TKV, masked=True)
            alpha = jnp.exp2(ms[h] - m_new)
            p = jnp.exp2(s_ref[h % 2] - m_new).astype(BF16)
            vt = vt_ref[0, h * V_ROWS:(h + 1) * V_ROWS, pl.ds(k0, FOX_TKV)]
            acc_ref[h] = acc_ref[h] * alpha + _dot(vt, p)
            new_ms.append(m_new)
        return tuple(new_ms), top_next

    ms = tuple(jnp.full((1, tq), NEG, F32) for _ in range(N_HEADS_B))
    carry = (ms, scores(0, 0, masked=True))
    carry = lax.fori_loop(0, n_full, functools.partial(kv_block, diagonal=False), carry)
    kv_block(n_full, carry, diagonal=True)
    pad = jnp.zeros((LANES - HEAD_DIM, tq), F32)
    for h in range(N_HEADS_B):
        acc = acc_ref[h]
        out_t = jnp.concatenate([acc[:HEAD_DIM] / acc[HEAD_DIM:HEAD_DIM + 1], pad], axis=0)
        o_ref[0, :, h * HEAD_DIM:(h + 1) * HEAD_DIM] = (
            out_t.T[:, :HEAD_DIM].astype(o_ref.dtype))


def _forgetting_attention(qt, k, vt, decay):
    batch, seq, _ = k.shape
    place = jnp.asarray(_fox_place(), dtype=BF16)
    return pl.pallas_call(
        _fox_kernel,
        grid=(batch, seq // FOX_TQ),
        in_specs=[pl.BlockSpec((1, N_HEADS_B * Q_ROWS, FOX_TQ), lambda b, i: (b, 0, i)),
                  pl.BlockSpec((1, seq, WIDTH_B), lambda b, i: (b, 0, 0)),
                  pl.BlockSpec((1, N_HEADS_B * V_ROWS, seq), lambda b, i: (b, 0, 0)),
                  pl.BlockSpec((1, seq, F_COLS), lambda b, i: (b, 0, 0)),
                  _resident(place.shape)],
        out_specs=pl.BlockSpec((1, FOX_TQ, WIDTH_B), lambda b, i: (b, i, 0)),
        out_shape=jax.ShapeDtypeStruct((batch, seq, WIDTH_B), BF16),
        scratch_shapes=[pltpu.VMEM((seq, N_HEADS_B * LANES), BF16),
                        pltpu.VMEM((N_HEADS_B, V_ROWS, FOX_TQ), F32),
                        pltpu.VMEM((2, FOX_TKV, FOX_TQ), F32)],
        compiler_params=_params(2),
        name="forgetting_attention",
    )(qt, k, vt, decay, place)


def _outproj_kernel(x_ref, g_ref, sc_ref, sh_ref, gate_ref, ya_ref, yb_ref,
                    wg_ref, wua_ref, wub_ref, wo_ref, o_ref, merged_ref):
    x = x_ref[0]
    d = x.shape[-1]
    h = _modulated_norm(x, g_ref[...], sc_ref[0], sh_ref[0]).astype(BF16)
    ya = ya_ref[0]
    yb = yb_ref[0]
    for c in range(d // OUT_CHUNK):
        cs = slice(c * OUT_CHUNK, (c + 1) * OUT_CHUNK)
        cs_b = slice(d + c * OUT_CHUNK, d + (c + 1) * OUT_CHUNK)
        gate_a = jax.nn.sigmoid(_dot(h, wg_ref[:, cs]))
        gate_b = jax.nn.sigmoid(_dot(h, wg_ref[:, cs_b]))
        merged = gate_a * _dot(ya, wua_ref[:, cs]) + gate_b * _dot(yb, wub_ref[:, cs])
        merged_ref[:, cs] = merged.astype(BF16)
    o_ref[0] = x + gate_ref[0] * _dot(merged_ref[...], wo_ref[...])


def _out_projection(x, g, sc, sh, gate, ya, yb, wg, wua, wub, wo):
    batch, seq, d = x.shape
    tok = lambda w: pl.BlockSpec((1, TM, w), lambda b, i: (b, i, 0))
    per_batch = pl.BlockSpec((1, 1, d), lambda b, i: (b, 0, 0))
    return pl.pallas_call(
        _outproj_kernel,
        grid=(batch, seq // TM),
        in_specs=[tok(d), _resident((1, d)), per_batch, per_batch, per_batch,
                  tok(WIDTH_G), tok(WIDTH_B),
                  _resident(wg.shape), _resident(wua.shape), _resident(wub.shape),
                  _resident(wo.shape)],
        out_specs=tok(d),
        out_shape=jax.ShapeDtypeStruct((batch, seq, d), F32),
        scratch_shapes=[pltpu.VMEM((TM, d), BF16)],
        compiler_params=_params(2),
        name="out_projection",
    )(x, g, sc, sh, gate, ya, yb, wg, wua, wub, wo)


def _ffn_kernel(*refs, final_norm):
    x_ref, g_ref, sc_ref, sh_ref, gate_ref, win_ref, wout_ref = refs[:7]
    refs = refs[7:]
    if final_norm:
        gf_ref = refs[0]
        refs = refs[1:]
    o_ref, h_ref, acc_ref = refs
    x = x_ref[0]
    h_ref[...] = _modulated_norm(x, g_ref[...], sc_ref[0], sh_ref[0]).astype(BF16)
    for c in range(D_FF // FF_CHUNK):
        cs = slice(c * FF_CHUNK, (c + 1) * FF_CHUNK)
        cs_up = slice(D_FF + c * FF_CHUNK, D_FF + (c + 1) * FF_CHUNK)
        h = h_ref[...]
        gt = _dot(h, win_ref[:, cs])
        up = _dot(h, win_ref[:, cs_up])
        act = ((gt * jax.nn.sigmoid(gt)) * up).astype(BF16)
        part = _dot(act, wout_ref[cs, :])
        if c == 0:
            acc_ref[...] = part
        else:
            acc_ref[...] += part
    y = x + gate_ref[0] * acc_ref[...]
    if final_norm:
        y = (y * lax.rsqrt(jnp.mean(y * y, axis=-1, keepdims=True) + EPS)) * gf_ref[...]
    o_ref[0] = y


def _ffn(x, g, sc, sh, gate, win, wout, g_final):
    batch, seq, d = x.shape
    final_norm = g_final is not None
    tok = pl.BlockSpec((1, TM, d), lambda b, i: (b, i, 0))
    per_batch = pl.BlockSpec((1, 1, d), lambda b, i: (b, 0, 0))
    in_specs = [tok, _resident((1, d)), per_batch, per_batch, per_batch,
                _resident(win.shape), _resident(wout.shape)]
    operands = [x, g, sc, sh, gate, win, wout]
    if final_norm:
        in_specs.append(_resident((1, d)))
        operands.append(g_final)
    return pl.pallas_call(
        functools.partial(_ffn_kernel, final_norm=final_norm),
        grid=(batch, seq // TM),
        in_specs=in_specs,
        out_specs=tok,
        out_shape=jax.ShapeDtypeStruct((batch, seq, d), F32),
        scratch_shapes=[pltpu.VMEM((TM, d), BF16), pltpu.VMEM((TM, d), F32)],
        compiler_params=_params(2),
        name="swiglu_ffn",
    )(*operands)


def _split_w_in(w_in):
    qa, ka, va, qb, kb, vb, fz, gza, gzb = jnp.split(
        w_in, np.cumsum((WIDTH_A, WIDTH_A, WIDTH_A, WIDTH_B, WIDTH_B, WIDTH_B, N_HEADS_B,
                         D_MODEL))[:].tolist(), axis=-1)
    cols = []
    for g in range(N_GROUPS):
        gs = slice(g * WIDTH_G, (g + 1) * WIDTH_G)
        cols += [qa[:, gs] * (QK_SCALE * LOG2E), ka[:, gs], va[:, gs]]
    wa = jnp.concatenate(cols, axis=-1).astype(BF16)
    pad = jnp.zeros((w_in.shape[0], F_COLS - N_HEADS_B), w_in.dtype)
    wqt = (qb * (QK_SCALE * LOG2E)).T.astype(BF16)
    wvt = vb.T.astype(BF16)
    wkf = jnp.concatenate([kb, fz, pad], axis=-1).astype(BF16)
    wg = jnp.concatenate([gza, gzb], axis=-1).astype(BF16)
    return wa, wqt, wvt, wkf, wg


def kernel(x, c, w_ada, b_ada, norm_mix, w_in, b_forget, w_up_a, w_up_b, w_out,
           norm_ffn, w_ffn_in, w_ffn_out, norm_final):
    depth = w_ada.shape[0]
    batch, seq, d = x.shape
    assert seq % TM == 0 and seq % FOX_TQ == 0 and FOX_TQ == FOX_TKV
    mod = _modulation(c, w_ada, b_ada).reshape(depth, batch, N_MOD, 1, d)
    for l in range(depth):
        sh1, sc1, g1, sh2, sc2, g2 = (mod[l, :, j] for j in range(N_MOD))
        wa, wqt, wvt, wkf, wg = _split_w_in(w_in[l])
        g_mix = norm_mix[l].reshape(1, d)
        za0, za1, za2, qt, k, vt, fz = _in_projection(x, g_mix, sc1, sh1, wa, wqt, wvt, wkf)
        decay = _decay_cumsum(fz, b_forget[l])
        ya = _dilated_mixture((za0, za1, za2))
        yb = _forgetting_attention(qt, k, vt, decay)
        x = _out_projection(x, g_mix, sc1, sh1, g1, ya, yb, wg,
                            w_up_a[l].astype(BF16), w_up_b[l].astype(BF16),
                            w_out[l].astype(BF16))
        g_final = norm_final.reshape(1, d) if l == depth - 1 else None
        x = _ffn(x, norm_ffn[l].reshape(1, d), sc2, sh2, g2,
                 w_ffn_in[l].astype(BF16), w_ffn_out[l].astype(BF16), g_final)
    return x
```

```python
import functools

import numpy as np
import jax
import jax.numpy as jnp
from jax import lax
from jax.experimental import pallas as pl
from jax.experimental.pallas import tpu as pltpu

F32 = jnp.float32
BF16 = jnp.bfloat16

D_MODEL = 1024
HEAD_DIM = 64
DIL_GROUPS = ((128, 1), (512, 4), (2048, 16))
HEADS_PER_GROUP = 4
N_GROUPS = len(DIL_GROUPS)
N_HEADS_A = HEADS_PER_GROUP * N_GROUPS
N_HEADS_B = 4
WIDTH_G = HEADS_PER_GROUP * HEAD_DIM
WIDTH_A = N_HEADS_A * HEAD_DIM
WIDTH_B = N_HEADS_B * HEAD_DIM
D_FF = -(-8 * D_MODEL // (3 * 256)) * 256
BLOCK = 128
ALIBI_MAX = 8.0
EPS = 1e-6
N_MOD = 6
QK_SCALE = HEAD_DIM ** -0.5

NEG = -1e30
LANES = 128
F_COLS = LANES
BF16_ROWS = 16
V_ROWS = HEAD_DIM + BF16_ROWS
Q_ROWS = LANES
DECAY_PIECES = 3
LOG2E = 1.4426950408889634

VMEM_LIMIT = 56 * 1024 * 1024

TM = 1024
FF_CHUNK = 256
OUT_CHUNK = 256
DECAY_CHUNK = 256
FOX_TQ = 512
FOX_TKV = 512
DIL_SPAN = BLOCK * max(d for _, d in DIL_GROUPS)
DIL_BLOCKS_IN_FLIGHT = 8


def _params(n_axes, flags=None):
    return pltpu.CompilerParams(dimension_semantics=("arbitrary",) * n_axes,
                                vmem_limit_bytes=VMEM_LIMIT, flags=flags)


def _resident(shape):
    zeros = (0,) * len(shape)
    return pl.BlockSpec(shape, lambda *_: zeros, pipeline_mode=pl.Buffered(1))


def _alibi_slopes():
    h = np.arange(1, N_HEADS_A + 1, dtype=np.float32)
    return np.asarray(2.0 ** (-ALIBI_MAX * h / N_HEADS_A), dtype=np.float32)


def _modulated_norm(x, g, sc, sh):
    y = x * lax.rsqrt(jnp.mean(x * x, axis=-1, keepdims=True) + EPS)
    return (y * g) * (1.0 + sc) + sh


def _dot(a, b):
    return jnp.dot(a, b, preferred_element_type=F32)


def _dot_nt(a, b):
    return lax.dot_general(a, b, (((1,), (1,)), ((), ())), preferred_element_type=F32)


def _split_bf16(x):
    pieces = []
    for _ in range(DECAY_PIECES):
        piece = x.astype(BF16)
        pieces.append(piece)
        x = x - piece.astype(F32)
    return pieces


def _mod_kernel(c_ref, w_ref, b_ref, o_ref):
    c = c_ref[...]
    c_act = (c * jax.nn.sigmoid(c)).astype(BF16)
    o_ref[0] = _dot(c_act, w_ref[0].astype(BF16)) + b_ref[0]


def _modulation(c, w_ada, b_ada):
    depth, d, n = w_ada.shape
    batch = c.shape[0]
    cols = 2 * d
    return pl.pallas_call(
        _mod_kernel,
        grid=(depth, n // cols),
        in_specs=[pl.BlockSpec((batch, d), lambda l, j: (0, 0)),
                  pl.BlockSpec((1, d, cols), lambda l, j: (l, 0, j)),
                  pl.BlockSpec((1, 1, cols), lambda l, j: (l, 0, j))],
        out_specs=pl.BlockSpec((1, batch, cols), lambda l, j: (l, 0, j)),
        out_shape=jax.ShapeDtypeStruct((depth, batch, n), F32),
        compiler_params=_params(2),
        name="adaln_modulation",
    )(c, w_ada, b_ada.reshape(depth, 1, n))


def _inproj_kernel(x_ref, g_ref, sc_ref, sh_ref, wa_ref, wqt_ref, wvt_ref, wkf_ref,
                   za0_ref, za1_ref, za2_ref, qt_ref, k_ref, vt_ref, fz_ref, stage_ref):
    h = _modulated_norm(x_ref[0], g_ref[...], sc_ref[0], sh_ref[0]).astype(BF16)
    tm = h.shape[0]
    wg = 3 * WIDTH_G
    for g, za_ref in enumerate((za0_ref, za1_ref, za2_ref)):
        dilation = DIL_GROUPS[g][1]
        za = _dot(h, wa_ref[:, g * wg:(g + 1) * wg])
        if dilation == 1:
            za_ref[0, 0] = za.astype(BF16)
            continue
        for c in range(wg // LANES):
            stage_ref[c] = za[:, c * LANES:(c + 1) * LANES]
        for r in range(dilation):
            rows = pl.ds(r, tm // dilation, stride=dilation)
            for c in range(wg // LANES):
                za_ref[0, r, :, c * LANES:(c + 1) * LANES] = stage_ref[c, rows, :].astype(BF16)
    qt = _dot_nt(wqt_ref[...], h).astype(BF16)
    vt = _dot_nt(wvt_ref[...], h).astype(BF16)
    q_tail = (lax.broadcasted_iota(jnp.int32, (Q_ROWS - HEAD_DIM, tm), 0)
              < DECAY_PIECES).astype(BF16)
    v_tail = (lax.broadcasted_iota(jnp.int32, (V_ROWS - HEAD_DIM, tm), 0) == 0).astype(BF16)
    for hd in range(N_HEADS_B):
        heads = slice(hd * HEAD_DIM, (hd + 1) * HEAD_DIM)
        qt_ref[0, hd * Q_ROWS:hd * Q_ROWS + HEAD_DIM, :] = qt[heads]
        qt_ref[0, hd * Q_ROWS + HEAD_DIM:(hd + 1) * Q_ROWS, :] = q_tail
        vt_ref[0, hd * V_ROWS:hd * V_ROWS + HEAD_DIM, :] = vt[heads]
        vt_ref[0, hd * V_ROWS + HEAD_DIM:(hd + 1) * V_ROWS, :] = v_tail
    kf = _dot(h, wkf_ref[...])
    k_ref[0] = kf[:, :WIDTH_B].astype(BF16)
    fz_ref[0] = kf[:, WIDTH_B:]


def _in_projection(x, g, sc, sh, wa, wqt, wvt, wkf):
    batch, seq, d = x.shape
    wg = 3 * WIDTH_G
    tok = lambda w: pl.BlockSpec((1, TM, w), lambda b, i: (b, i, 0))
    tok_t = lambda rows: pl.BlockSpec((1, rows, TM), lambda b, i: (b, 0, i))
    per_batch = pl.BlockSpec((1, 1, d), lambda b, i: (b, 0, 0))
    dils = [dilation for _, dilation in DIL_GROUPS]
    assert all(TM % (dilation * BF16_ROWS) == 0 for dilation in dils)
    sub_major = [pl.BlockSpec((1, dilation, TM // dilation, wg), lambda b, i: (b, 0, i, 0))
                 for dilation in dils]
    return pl.pallas_call(
        _inproj_kernel,
        grid=(batch, seq // TM),
        in_specs=[tok(d), _resident((1, d)), per_batch, per_batch,
                  _resident(wa.shape), _resident(wqt.shape), _resident(wvt.shape),
                  _resident(wkf.shape)],
        out_specs=sub_major + [tok_t(N_HEADS_B * Q_ROWS), tok(WIDTH_B),
                               tok_t(N_HEADS_B * V_ROWS), tok(F_COLS)],
        out_shape=[jax.ShapeDtypeStruct((batch, dilation, seq // dilation, wg), BF16)
                   for dilation in dils]
        + [jax.ShapeDtypeStruct((batch, N_HEADS_B * Q_ROWS, seq), BF16),
           jax.ShapeDtypeStruct((batch, seq, WIDTH_B), BF16),
           jax.ShapeDtypeStruct((batch, N_HEADS_B * V_ROWS, seq), BF16),
           jax.ShapeDtypeStruct((batch, seq, F_COLS), F32)],
        scratch_shapes=[pltpu.VMEM((wg // LANES, TM, LANES), F32)],
        compiler_params=_params(2),
        name="in_projection",
    )(x, g, sc, sh, wa, wqt, wvt, wkf)


def _log_sigmoid(x):
    return jnp.minimum(x, 0.0) - jnp.log1p(jnp.exp(-jnp.abs(x)))


def _decay_kernel(fz_ref, b_ref, tri_ref, o_ref):
    seq = fz_ref.shape[1]
    tri = tri_ref[...]
    carry = jnp.zeros((1, F_COLS), F32)
    for c in range(seq // DECAY_CHUNK):
        rows = slice(c * DECAY_CHUNK, (c + 1) * DECAY_CHUNK)
        lf = _log_sigmoid(fz_ref[0, rows, :] + b_ref[...])
        cs = sum(_dot(tri, piece) for piece in _split_bf16(lf)) + carry
        o_ref[0, rows, :] = cs
        carry = cs[DECAY_CHUNK - 1:DECAY_CHUNK, :]


def _decay_cumsum(fz, b_forget):
    batch, seq, _ = fz.shape
    bias = jnp.zeros((1, F_COLS), F32).at[0, :N_HEADS_B].set(b_forget.astype(F32))
    idx = np.arange(DECAY_CHUNK)
    tri = jnp.asarray(idx[None, :] <= idx[:, None], dtype=BF16)
    return pl.pallas_call(
        _decay_kernel,
        grid=(batch,),
        in_specs=[pl.BlockSpec((1, seq, F_COLS), lambda b: (b, 0, 0)),
                  _resident(bias.shape), _resident(tri.shape)],
        out_specs=pl.BlockSpec((1, seq, F_COLS), lambda b: (b, 0, 0)),
        out_shape=jax.ShapeDtypeStruct((batch, seq, F_COLS), F32),
        compiler_params=_params(1),
        name="forget_decay_cumsum",
    )(fz, bias, tri)


def _band_bias():
    qi = np.arange(BLOCK)[:, None]
    kj = np.arange(2 * BLOCK)[None, :]
    dist = qi + BLOCK - kj
    in_band = (dist >= 0) & (dist <= BLOCK)
    slopes = _alibi_slopes()
    bias = np.empty((N_HEADS_A, BLOCK, 2 * BLOCK), np.float32)
    for g, (window, dilation) in enumerate(DIL_GROUPS):
        assert window // dilation == BLOCK
        for h in range(HEADS_PER_GROUP):
            head = g * HEADS_PER_GROUP + h
            alibi = -slopes[head] * (dist * dilation).astype(np.float32)
            bias[head] = np.where(in_band, alibi * np.float32(LOG2E), np.float32(NEG))
    return bias.reshape(N_HEADS_A // 2, 2 * BLOCK, 2 * BLOCK)


def _unroll(trips, blocks_per_trip=1):
    fits = [u for u in range(1, trips + 1)
            if trips % u == 0 and u * blocks_per_trip <= DIL_BLOCKS_IN_FLIGHT]
    return max(fits, default=1)


def _dilated_kernel(c0_ref, p0_ref, c1_ref, p1_ref, c2_ref, p2_ref, bias_ref, o_ref,
                    state_ref, gather_ref, stage_ref, first_bias_ref):
    span = pl.program_id(1)
    pairs = HEADS_PER_GROUP // 2
    kj = lax.broadcasted_iota(jnp.int32, (2 * BLOCK, 2 * BLOCK), 1)
    no_prev = jnp.where(kj >= jnp.where(span == 0, BLOCK, 0), 0.0, NEG)
    for slab in range(N_GROUPS * pairs):
        first_bias_ref[slab] = bias_ref[slab] + no_prev
    low_half = lax.broadcasted_iota(jnp.int32, (BLOCK, LANES), 1) < HEAD_DIM
    own_half = jnp.concatenate([low_half, jnp.logical_not(low_half)], axis=0)

    def attend(g, q_of, k_of, v_of, rows, first):
        for pair in range(pairs):
            q = q_of(pair)
            q = jnp.where(own_half, jnp.concatenate([q, q], axis=0), jnp.zeros((), q.dtype))
            bias = (first_bias_ref if first else bias_ref)[g * pairs + pair]
            z = _dot_nt(q, k_of(pair)) + bias
            m = jnp.max(z, axis=-1, keepdims=True)
            p = jnp.exp2(z - m)
            l = jnp.sum(p, axis=-1, keepdims=True)
            out = _dot(p.astype(BF16), v_of(pair)) / l
            lse = m + jnp.log2(l)
            state_ref[g, pair, 0, rows, :] = jnp.where(low_half, out[:BLOCK], out[BLOCK:])
            state_ref[g, pair, 1, rows, :] = jnp.where(low_half, lse[:BLOCK], lse[BLOCK:])

    for g, (cur_ref, prev_ref) in enumerate(((c0_ref, p0_ref), (c1_ref, p1_ref),
                                             (c2_ref, p2_ref))):
        dilation = DIL_GROUPS[g][1]
        class_rows = cur_ref.shape[2]
        nblk = class_rows // BLOCK

        def cols(part, pair):
            return slice(part * WIDTH_G + pair * LANES, part * WIDTH_G + (pair + 1) * LANES)

        def residue(r, carry, g=g, cur_ref=cur_ref, prev_ref=prev_ref, nblk=nblk,
                    cols=cols, class_rows=class_rows):
            def state_rows(n):
                return pl.ds(pl.multiple_of(r * class_rows + n * BLOCK, BLOCK), BLOCK)

            def with_prev(part):
                return lambda pair: jnp.concatenate(
                    [prev_ref[0, r, :, cols(part, pair)],
                     cur_ref[0, r, :BLOCK, cols(part, pair)]], axis=0)

            attend(g, lambda pair: cur_ref[0, r, :BLOCK, cols(0, pair)], with_prev(1),
                   with_prev(2), state_rows(0), first=True)

            def block(n, c):
                row0 = pl.multiple_of(n * BLOCK, BLOCK)
                keys = pl.ds(row0 - BLOCK, 2 * BLOCK)
                attend(g, lambda pair: cur_ref[0, r, pl.ds(row0, BLOCK), cols(0, pair)],
                       lambda pair: cur_ref[0, r, keys, cols(1, pair)],
                       lambda pair: cur_ref[0, r, keys, cols(2, pair)],
                       state_rows(n), first=False)
                return c

            if nblk > 1:
                lax.fori_loop(1, nblk, block, 0, unroll=_unroll(nblk - 1))
            return carry

        if dilation == 1:
            residue(0, 0)
        else:
            lax.fori_loop(0, dilation, residue, 0, unroll=_unroll(dilation, nblk))

    mid = DIL_GROUPS[1][1]
    assert [d for _, d in DIL_GROUPS] == [1, mid, mid * mid] and BLOCK % mid == 0
    mid_rows = DIL_SPAN // mid
    wide_rows = DIL_SPAN // (mid * mid)
    run = BLOCK // mid

    def merge(idx, carry):
        r = idx // (mid_rows // BLOCK)
        n = idx % (mid_rows // BLOCK)
        tokens = pl.ds(mid * BLOCK * n + r, BLOCK, stride=mid)
        mid_tokens = pl.ds(pl.multiple_of(r * mid_rows + n * BLOCK, BLOCK), BLOCK)
        for pair in range(pairs):
            def tiles(kind):
                for a in range(mid):
                    wide = pl.ds(pl.multiple_of((mid * a + r) * wide_rows + run * n, run), run)
                    gather_ref[pair, kind, pl.ds(a, run, stride=mid), :] = (
                        state_ref[2, pair, kind, wide, :])
                return (state_ref[0, pair, kind, tokens, :],
                        state_ref[1, pair, kind, mid_tokens, :], gather_ref[pair, kind])

            lses = tiles(1)
            top = jnp.maximum(jnp.maximum(lses[0], lses[1]), lses[2])
            weights = [jnp.exp2(lse - top) for lse in lses]
            num = sum(w * out for w, out in zip(weights, tiles(0)))
            stage_ref[pair, tokens, :] = num / sum(weights)
        return carry

    lax.fori_loop(0, DIL_SPAN // BLOCK, merge, 0)
    for pair in range(HEADS_PER_GROUP // 2):
        o_ref[0, :, pair * LANES:(pair + 1) * LANES] = stage_ref[pair].astype(o_ref.dtype)


def _dilated_mixture(zs):
    batch, _, seq, wg = zs[0].shape
    assert seq % DIL_SPAN == 0
    in_specs, operands = [], []
    for g, (_, dilation) in enumerate(DIL_GROUPS):
        rows = DIL_SPAN // dilation
        assert rows % BLOCK == 0 and zs[g].shape == (batch, dilation, seq // dilation, wg)
        nblk = rows // BLOCK
        in_specs.append(pl.BlockSpec((1, dilation, rows, wg), lambda b, t: (b, 0, t, 0)))
        in_specs.append(pl.BlockSpec(
            (1, dilation, BLOCK, wg),
            lambda b, t, nblk=nblk: (b, 0, jnp.maximum(t * nblk - 1, 0), 0)))
        operands += [zs[g], zs[g]]
    bias = jnp.asarray(_band_bias())
    return pl.pallas_call(
        _dilated_kernel,
        grid=(batch, seq // DIL_SPAN),
        in_specs=in_specs + [_resident(bias.shape)],
        out_specs=pl.BlockSpec((1, DIL_SPAN, WIDTH_G), lambda b, t: (b, t, 0)),
        out_shape=jax.ShapeDtypeStruct((batch, seq, WIDTH_G), BF16),
        scratch_shapes=[pltpu.VMEM((N_GROUPS, HEADS_PER_GROUP // 2, 2, DIL_SPAN, LANES), F32),
                        pltpu.VMEM((HEADS_PER_GROUP // 2, 2, BLOCK, LANES), F32),
                        pltpu.VMEM((HEADS_PER_GROUP // 2, DIL_SPAN, LANES), F32),
                        pltpu.VMEM(bias.shape, F32)],
        compiler_params=_params(2),
        name="dilated_attention",
    )(*operands, bias)


def _fox_place():
    place = np.zeros((WIDTH_B + DECAY_PIECES * F_COLS, N_HEADS_B * LANES), np.float32)
    for c in range(WIDTH_B):
        place[c, (c // HEAD_DIM) * LANES + c % HEAD_DIM] = 1.0
    for i in range(DECAY_PIECES):
        for h in range(N_HEADS_B):
            place[WIDTH_B + i * F_COLS + h, h * LANES + HEAD_DIM + i] = 1.0
    return place


def _fox_kernel(qt_ref, k_ref, vt_ref, f_ref, place_ref, o_ref, kaug_ref, acc_ref, s_ref):
    tq = qt_ref.shape[2]
    seq = k_ref.shape[1]
    qb = pl.program_id(1)
    q0 = qb * tq
    n_full = qb
    key_minus_query = (lax.broadcasted_iota(jnp.int32, (FOX_TKV, tq), 0)
                       - lax.broadcasted_iota(jnp.int32, (FOX_TKV, tq), 1))

    @pl.when(qb == 0)
    def _():
        for c in range(seq // FOX_TKV):
            rows = slice(c * FOX_TKV, (c + 1) * FOX_TKV)
            pieces = _split_bf16(f_ref[0, rows, :] * (-LOG2E))
            packed = jnp.concatenate([k_ref[0, rows, :]] + pieces, axis=1)
            kaug_ref[rows, :] = _dot(packed, place_ref[...]).astype(BF16)

    acc_ref[...] = jnp.zeros_like(acc_ref)

    def scores(h, k0, masked):
        keys = pl.ds(pl.multiple_of(k0, FOX_TKV), FOX_TKV)
        s = _dot(kaug_ref[keys, h * LANES:(h + 1) * LANES],
                 qt_ref[0, h * Q_ROWS:(h + 1) * Q_ROWS, :])
        if masked:
            s = jnp.where(key_minus_query <= q0 - k0, s, NEG)
        s_ref[h % 2] = s
        return jnp.max(s, axis=0, keepdims=True)

    def kv_block(j, carry, diagonal):
        ms, top_next = carry
        k0 = pl.multiple_of(j * FOX_TKV, FOX_TKV)
        new_ms = []
        for h in range(N_HEADS_B):
            m_new = jnp.maximum(ms[h], top_next)
            if h + 1 < N_HEADS_B:
                top_next = scores(h + 1, k0, masked=diagonal)
            elif not diagonal:
                top_next = scores(0, k0 + FOX_TKV, masked=True)
            alpha = jnp.exp2(ms[h] - m_new)
            p = jnp.exp2(s_ref[h % 2] - m_new).astype(BF16)
            vt = vt_ref[0, h * V_ROWS:(h + 1) * V_ROWS, pl.ds(k0, FOX_TKV)]
            acc_ref[h] = acc_ref[h] * alpha + _dot(vt, p)
            new_ms.append(m_new)
        return tuple(new_ms), top_next

    ms = tuple(jnp.full((1, tq), NEG, F32) for _ in range(N_HEADS_B))
    carry = (ms, scores(0, 0, masked=True))
    carry = lax.fori_loop(0, n_full, functools.partial(kv_block, diagonal=False), carry)
    kv_block(n_full, carry, diagonal=True)
    pad = jnp.zeros((LANES - HEAD_DIM, tq), F32)
    for h in range(N_HEADS_B):
        acc = acc_ref[h]
        out_t = jnp.concatenate([acc[:HEAD_DIM] / acc[HEAD_DIM:HEAD_DIM + 1], pad], axis=0)
        o_ref[0, :, h * HEAD_DIM:(h + 1) * HEAD_DIM] = (
            out_t.T[:, :HEAD_DIM].astype(o_ref.dtype))


def _forgetting_attention(qt, k, vt, decay):
    batch, seq, _ = k.shape
    place = jnp.asarray(_fox_place(), dtype=BF16)
    return pl.pallas_call(
        _fox_kernel,
        grid=(batch, seq // FOX_TQ),
        in_specs=[pl.BlockSpec((1, N_HEADS_B * Q_ROWS, FOX_TQ), lambda b, i: (b, 0, i)),
                  pl.BlockSpec((1, seq, WIDTH_B), lambda b, i: (b, 0, 0)),
                  pl.BlockSpec((1, N_HEADS_B * V_ROWS, seq), lambda b, i: (b, 0, 0)),
                  pl.BlockSpec((1, seq, F_COLS), lambda b, i: (b, 0, 0)),
                  _resident(place.shape)],
        out_specs=pl.BlockSpec((1, FOX_TQ, WIDTH_B), lambda b, i: (b, i, 0)),
        out_shape=jax.ShapeDtypeStruct((batch, seq, WIDTH_B), BF16),
        scratch_shapes=[pltpu.VMEM((seq, N_HEADS_B * LANES), BF16),
                        pltpu.VMEM((N_HEADS_B, V_ROWS, FOX_TQ), F32),
                        pltpu.VMEM((2, FOX_TKV, FOX_TQ), F32)],
        compiler_params=_params(2),
        name="forgetting_attention",
    )(qt, k, vt, decay, place)


def _outproj_kernel(x_ref, g_ref, sc_ref, sh_ref, gate_ref, ya_ref, yb_ref,
                    wg_ref, wua_ref, wub_ref, wo_ref, o_ref, merged_ref):
    x = x_ref[0]
    d = x.shape[-1]
    h = _modulated_norm(x, g_ref[...], sc_ref[0], sh_ref[0]).astype(BF16)
    ya = ya_ref[0]
    yb = yb_ref[0]
    for c in range(d // OUT_CHUNK):
        cs = slice(c * OUT_CHUNK, (c + 1) * OUT_CHUNK)
        cs_b = slice(d + c * OUT_CHUNK, d + (c + 1) * OUT_CHUNK)
        gate_a = jax.nn.sigmoid(_dot(h, wg_ref[:, cs]))
        gate_b = jax.nn.sigmoid(_dot(h, wg_ref[:, cs_b]))
        merged = gate_a * _dot(ya, wua_ref[:, cs]) + gate_b * _dot(yb, wub_ref[:, cs])
        merged_ref[:, cs] = merged.astype(BF16)
    o_ref[0] = x + gate_ref[0] * _dot(merged_ref[...], wo_ref[...])


def _out_projection(x, g, sc, sh, gate, ya, yb, wg, wua, wub, wo):
    batch, seq, d = x.shape
    tok = lambda w: pl.BlockSpec((1, TM, w), lambda b, i: (b, i, 0))
    per_batch = pl.BlockSpec((1, 1, d), lambda b, i: (b, 0, 0))
    return pl.pallas_call(
        _outproj_kernel,
        grid=(batch, seq // TM),
        in_specs=[tok(d), _resident((1, d)), per_batch, per_batch, per_batch,
                  tok(WIDTH_G), tok(WIDTH_B),
                  _resident(wg.shape), _resident(wua.shape), _resident(wub.shape),
                  _resident(wo.shape)],
        out_specs=tok(d),
        out_shape=jax.ShapeDtypeStruct((batch, seq, d), F32),
        scratch_shapes=[pltpu.VMEM((TM, d), BF16)],
        compiler_params=_params(2),
        name="out_projection",
    )(x, g, sc, sh, gate, ya, yb, wg, wua, wub, wo)


def _ffn_kernel(*refs, final_norm):
    x_ref, g_ref, sc_ref, sh_ref, gate_ref, win_ref, wout_ref = refs[:7]
    refs = refs[7:]
    if final_norm:
        gf_ref = refs[0]
        refs = refs[1:]
    o_ref, h_ref, acc_ref = refs
    x = x_ref[0]
    h_ref[...] = _modulated_norm(x, g_ref[...], sc_ref[0], sh_ref[0]).astype(BF16)
    for c in range(D_FF // FF_CHUNK):
        cs = slice(c * FF_CHUNK, (c + 1) * FF_CHUNK)
        cs_up = slice(D_FF + c * FF_CHUNK, D_FF + (c + 1) * FF_CHUNK)
        h = h_ref[...]
        gt = _dot(h, win_ref[:, cs])
        up = _dot(h, win_ref[:, cs_up])
        act = ((gt * jax.nn.sigmoid(gt)) * up).astype(BF16)
        part = _dot(act, wout_ref[cs, :])
        if c == 0:
            acc_ref[...] = part
        else:
            acc_ref[...] += part
    y = x + gate_ref[0] * acc_ref[...]
    if final_norm:
        y = (y * lax.rsqrt(jnp.mean(y * y, axis=-1, keepdims=True) + EPS)) * gf_ref[...]
    o_ref[0] = y


def _ffn(x, g, sc, sh, gate, win, wout, g_final):
    batch, seq, d = x.shape
    final_norm = g_final is not None
    tok = pl.BlockSpec((1, TM, d), lambda b, i: (b, i, 0))
    per_batch = pl.BlockSpec((1, 1, d), lambda b, i: (b, 0, 0))
    in_specs = [tok, _resident((1, d)), per_batch, per_batch, per_batch,
                _resident(win.shape), _resident(wout.shape)]
    operands = [x, g, sc, sh, gate, win, wout]
    if final_norm:
        in_specs.append(_resident((1, d)))
        operands.append(g_final)
    return pl.pallas_call(
        functools.partial(_ffn_kernel, final_norm=final_norm),
        grid=(batch, seq // TM),
        in_specs=in_specs,
        out_specs=tok,
        out_shape=jax.ShapeDtypeStruct((batch, seq, d), F32),
        scratch_shapes=[pltpu.VMEM((TM, d), BF16), pltpu.VMEM((TM, d), F32)],
        compiler_params=_params(2),
        name="swiglu_ffn",
    )(*operands)


def _split_w_in(w_in):
    qa, ka, va, qb, kb, vb, fz, gza, gzb = jnp.split(
        w_in, np.cumsum((WIDTH_A, WIDTH_A, WIDTH_A, WIDTH_B, WIDTH_B, WIDTH_B, N_HEADS_B,
                         D_MODEL))[:].tolist(), axis=-1)
    cols = []
    for g in range(N_GROUPS):
        gs = slice(g * WIDTH_G, (g + 1) * WIDTH_G)
        cols += [qa[:, gs] * (QK_SCALE * LOG2E), ka[:, gs], va[:, gs]]
    wa = jnp.concatenate(cols, axis=-1).astype(BF16)
    pad = jnp.zeros((w_in.shape[0], F_COLS - N_HEADS_B), w_in.dtype)
    wqt = (qb * (QK_SCALE * LOG2E)).T.astype(BF16)
    wvt = vb.T.astype(BF16)
    wkf = jnp.concatenate([kb, fz, pad], axis=-1).astype(BF16)
    wg = jnp.concatenate([gza, gzb], axis=-1).astype(BF16)
    return wa, wqt, wvt, wkf, wg


def kernel(x, c, w_ada, b_ada, norm_mix, w_in, b_forget, w_up_a, w_up_b, w_out,
           norm_ffn, w_ffn_in, w_ffn_out, norm_final):
    depth = w_ada.shape[0]
    batch, seq, d = x.shape
    assert seq % TM == 0 and seq % FOX_TQ == 0 and FOX_TQ == FOX_TKV
    mod = _modulation(c, w_ada, b_ada).reshape(depth, batch, N_MOD, 1, d)
    for l in range(depth):
        sh1, sc1, g1, sh2, sc2, g2 = (mod[l, :, j] for j in range(N_MOD))
        wa, wqt, wvt, wkf, wg = _split_w_in(w_in[l])
        g_mix = norm_mix[l].reshape(1, d)
        za0, za1, za2, qt, k, vt, fz = _in_projection(x, g_mix, sc1, sh1, wa, wqt, wvt, wkf)
        decay = _decay_cumsum(fz, b_forget[l])
        ya = _dilated_mixture((za0, za1, za2))
        yb = _forgetting_attention(qt, k, vt, decay)
        x = _out_projection(x, g_mix, sc1, sh1, g1, ya, yb, wg,
                            w_up_a[l].astype(BF16), w_up_b[l].astype(BF16),
                            w_out[l].astype(BF16))
        g_final = norm_final.reshape(1, d) if l == depth - 1 else None
        x = _ffn(x, norm_ffn[l].reshape(1, d), sc2, sh2, g2,
                 w_ffn_in[l].astype(BF16), w_ffn_out[l].astype(BF16), g_final)
    return x
```

```python
import functools

import numpy as np
import jax
import jax.numpy as jnp
from jax import lax
from jax.experimental import pallas as pl
from jax.experimental.pallas import tpu as pltpu

F32 = jnp.float32
BF16 = jnp.bfloat16

D_MODEL = 1024
HEAD_DIM = 64
DIL_GROUPS = ((128, 1), (512, 4), (2048, 16))
HEADS_PER_GROUP = 4
N_GROUPS = len(DIL_GROUPS)
N_HEADS_A = HEADS_PER_GROUP * N_GROUPS
N_HEADS_B = 4
WIDTH_G = HEADS_PER_GROUP * HEAD_DIM
WIDTH_A = N_HEADS_A * HEAD_DIM
WIDTH_B = N_HEADS_B * HEAD_DIM
D_FF = -(-8 * D_MODEL // (3 * 256)) * 256
BLOCK = 128
ALIBI_MAX = 8.0
EPS = 1e-6
N_MOD = 6
MOD_SHIFT1, MOD_SCALE1, MOD_GATE1, MOD_SHIFT2, MOD_SCALE2, MOD_GATE2 = range(N_MOD)
QK_SCALE = HEAD_DIM ** -0.5

NEG = -1e30
LANES = 128
F_COLS = LANES
BF16_ROWS = 16
V_ROWS = HEAD_DIM + BF16_ROWS
Q_ROWS = LANES
DECAY_PIECES = 3
LOG2E = 1.4426950408889634

VMEM_LIMIT = 56 * 1024 * 1024

TM = 1024
FF_CHUNK = 256
OUT_CHUNK = 256
DECAY_CHUNK = 256
FOX_TQ = 512
FOX_TKV = 512
DIL_SPAN = BLOCK * max(d for _, d in DIL_GROUPS)
DIL_BLOCKS_IN_FLIGHT = 8


def _params(n_axes, flags=None):
    return pltpu.CompilerParams(dimension_semantics=("arbitrary",) * n_axes,
                                vmem_limit_bytes=VMEM_LIMIT, flags=flags)


def _resident(shape):
    zeros = (0,) * len(shape)
    return pl.BlockSpec(shape, lambda *_: zeros, pipeline_mode=pl.Buffered(1))


def _of_layer(stacked, l):
    tail = stacked.shape[1:]
    index = (l,) + (0,) * len(tail)
    return pl.BlockSpec((None,) + tail, lambda *_: index, pipeline_mode=pl.Buffered(1))


def _mod_vector(mod, l, j):
    return pl.BlockSpec((None, None, None) + mod.shape[3:], lambda b, i: (l, b, j, 0, 0))


def _alibi_slopes():
    h = np.arange(1, N_HEADS_A + 1, dtype=np.float32)
    return np.asarray(2.0 ** (-ALIBI_MAX * h / N_HEADS_A), dtype=np.float32)


def _modulated_norm(x, g, sc, sh):
    y = x * lax.rsqrt(jnp.mean(x * x, axis=-1, keepdims=True) + EPS)
    return (y * g) * (1.0 + sc) + sh


def _dot(a, b):
    return jnp.dot(a, b, preferred_element_type=F32)


def _dot_nt(a, b):
    return lax.dot_general(a, b, (((1,), (1,)), ((), ())), preferred_element_type=F32)


def _split_bf16(x):
    pieces = []
    for _ in range(DECAY_PIECES):
        piece = x.astype(BF16)
        pieces.append(piece)
        x = x - piece.astype(F32)
    return pieces


def _mod_kernel(c_ref, w_ref, b_ref, o_ref):
    c = c_ref[...]
    c_act = (c * jax.nn.sigmoid(c)).astype(BF16)
    o_ref[0] = _dot(c_act, w_ref[0].astype(BF16)) + b_ref[0]


def _modulation(c, w_ada, b_ada):
    depth, d, n = w_ada.shape
    batch = c.shape[0]
    cols = 2 * d
    return pl.pallas_call(
        _mod_kernel,
        grid=(depth, n // cols),
        in_specs=[pl.BlockSpec((batch, d), lambda l, j: (0, 0)),
                  pl.BlockSpec((1, d, cols), lambda l, j: (l, 0, j)),
                  pl.BlockSpec((1, 1, cols), lambda l, j: (l, 0, j))],
        out_specs=pl.BlockSpec((1, batch, cols), lambda l, j: (l, 0, j)),
        out_shape=jax.ShapeDtypeStruct((depth, batch, n), F32),
        compiler_params=_params(2),
        name="adaln_modulation",
    )(c, w_ada, b_ada.reshape(depth, 1, n))


def _inproj_kernel(x_ref, sc_ref, sh_ref, g_ref, wa_ref, wqt_ref, wvt_ref, wkf_ref,
                   za0_ref, za1_ref, za2_ref, qt_ref, k_ref, vt_ref, fz_ref, stage_ref):
    h = _modulated_norm(x_ref[0], g_ref[...], sc_ref[...], sh_ref[...]).astype(BF16)
    tm = h.shape[0]
    wg = 3 * WIDTH_G
    for g, za_ref in enumerate((za0_ref, za1_ref, za2_ref)):
        dilation = DIL_GROUPS[g][1]
        za = _dot(h, wa_ref[:, g * wg:(g + 1) * wg])
        if dilation == 1:
            za_ref[0, 0] = za.astype(BF16)
            continue
        for c in range(wg // LANES):
            stage_ref[c] = za[:, c * LANES:(c + 1) * LANES]
        for r in range(dilation):
            rows = pl.ds(r, tm // dilation, stride=dilation)
            for c in range(wg // LANES):
                za_ref[0, r, :, c * LANES:(c + 1) * LANES] = stage_ref[c, rows, :].astype(BF16)
    qt = _dot_nt(wqt_ref[...], h).astype(BF16)
    vt = _dot_nt(wvt_ref[...], h).astype(BF16)
    q_tail = (lax.broadcasted_iota(jnp.int32, (Q_ROWS - HEAD_DIM, tm), 0)
              < DECAY_PIECES).astype(BF16)
    v_tail = (lax.broadcasted_iota(jnp.int32, (V_ROWS - HEAD_DIM, tm), 0) == 0).astype(BF16)
    for hd in range(N_HEADS_B):
        heads = slice(hd * HEAD_DIM, (hd + 1) * HEAD_DIM)
        qt_ref[0, hd * Q_ROWS:hd * Q_ROWS + HEAD_DIM, :] = qt[heads]
        qt_ref[0, hd * Q_ROWS + HEAD_DIM:(hd + 1) * Q_ROWS, :] = q_tail
        vt_ref[0, hd * V_ROWS:hd * V_ROWS + HEAD_DIM, :] = vt[heads]
        vt_ref[0, hd * V_ROWS + HEAD_DIM:(hd + 1) * V_ROWS, :] = v_tail
    kf = _dot(h, wkf_ref[...])
    k_ref[0] = kf[:, :WIDTH_B].astype(BF16)
    fz_ref[0] = kf[:, WIDTH_B:]


def _in_projection(x, mod, params, l):
    batch, seq, d = x.shape
    wg = 3 * WIDTH_G
    tok = lambda w: pl.BlockSpec((1, TM, w), lambda b, i: (b, i, 0))
    tok_t = lambda rows: pl.BlockSpec((1, rows, TM), lambda b, i: (b, 0, i))
    dils = [dilation for _, dilation in DIL_GROUPS]
    assert all(TM % (dilation * BF16_ROWS) == 0 for dilation in dils)
    sub_major = [pl.BlockSpec((1, dilation, TM // dilation, wg), lambda b, i: (b, 0, i, 0))
                 for dilation in dils]
    return pl.pallas_call(
        _inproj_kernel,
        grid=(batch, seq // TM),
        in_specs=[tok(d), _mod_vector(mod, l, MOD_SCALE1), _mod_vector(mod, l, MOD_SHIFT1)]
        + [_of_layer(p, l) for p in params],
        out_specs=sub_major + [tok_t(N_HEADS_B * Q_ROWS), tok(WIDTH_B),
                               tok_t(N_HEADS_B * V_ROWS), tok(F_COLS)],
        out_shape=[jax.ShapeDtypeStruct((batch, dilation, seq // dilation, wg), BF16)
                   for dilation in dils]
        + [jax.ShapeDtypeStruct((batch, N_HEADS_B * Q_ROWS, seq), BF16),
           jax.ShapeDtypeStruct((batch, seq, WIDTH_B), BF16),
           jax.ShapeDtypeStruct((batch, N_HEADS_B * V_ROWS, seq), BF16),
           jax.ShapeDtypeStruct((batch, seq, F_COLS), F32)],
        scratch_shapes=[pltpu.VMEM((wg // LANES, TM, LANES), F32)],
        compiler_params=_params(2),
        name="in_projection",
    )(x, mod, mod, *params)


def _band_bias():
    qi = np.arange(BLOCK)[:, None]
    kj = np.arange(2 * BLOCK)[None, :]
    dist = qi + BLOCK - kj
    in_band = (dist >= 0) & (dist <= BLOCK)
    slopes = _alibi_slopes()
    bias = np.empty((N_HEADS_A, BLOCK, 2 * BLOCK), np.float32)
    for g, (window, dilation) in enumerate(DIL_GROUPS):
        assert window // dilation == BLOCK
        for h in range(HEADS_PER_GROUP):
            head = g * HEADS_PER_GROUP + h
            alibi = -slopes[head] * (dist * dilation).astype(np.float32)
            bias[head] = np.where(in_band, alibi * np.float32(LOG2E), np.float32(NEG))
    return bias.reshape(N_HEADS_A // 2, 2 * BLOCK, 2 * BLOCK)


def _unroll(trips, blocks_per_trip=1):
    fits = [u for u in range(1, trips + 1)
            if trips % u == 0 and u * blocks_per_trip <= DIL_BLOCKS_IN_FLIGHT]
    return max(fits, default=1)


def _dilated_kernel(c0_ref, p0_ref, c1_ref, p1_ref, c2_ref, p2_ref, bias_ref, o_ref,
                    state_ref, gather_ref, stage_ref, first_bias_ref):
    span = pl.program_id(1)
    pairs = HEADS_PER_GROUP // 2
    kj = lax.broadcasted_iota(jnp.int32, (2 * BLOCK, 2 * BLOCK), 1)
    no_prev = jnp.where(kj >= jnp.where(span == 0, BLOCK, 0), 0.0, NEG)
    for slab in range(N_GROUPS * pairs):
        first_bias_ref[slab] = bias_ref[slab] + no_prev
    low_half = lax.broadcasted_iota(jnp.int32, (BLOCK, LANES), 1) < HEAD_DIM
    own_half = jnp.concatenate([low_half, jnp.logical_not(low_half)], axis=0)

    def attend(g, q_of, k_of, v_of, rows, first):
        for pair in range(pairs):
            q = q_of(pair)
            q = jnp.where(own_half, jnp.concatenate([q, q], axis=0), jnp.zeros((), q.dtype))
            bias = (first_bias_ref if first else bias_ref)[g * pairs + pair]
            z = _dot_nt(q, k_of(pair)) + bias
            m = jnp.max(z, axis=-1, keepdims=True)
            p = jnp.exp2(z - m)
            l = jnp.sum(p, axis=-1, keepdims=True)
            out = _dot(p.astype(BF16), v_of(pair)) / l
            lse = m + jnp.log2(l)
            state_ref[g, pair, 0, rows, :] = jnp.where(low_half, out[:BLOCK], out[BLOCK:])
            state_ref[g, pair, 1, rows, :] = jnp.where(low_half, lse[:BLOCK], lse[BLOCK:])

    for g, (cur_ref, prev_ref) in enumerate(((c0_ref, p0_ref), (c1_ref, p1_ref),
                                             (c2_ref, p2_ref))):
        dilation = DIL_GROUPS[g][1]
        class_rows = cur_ref.shape[2]
        nblk = class_rows // BLOCK

        def cols(part, pair):
            return slice(part * WIDTH_G + pair * LANES, part * WIDTH_G + (pair + 1) * LANES)

        def residue(r, carry, g=g, cur_ref=cur_ref, prev_ref=prev_ref, nblk=nblk,
                    cols=cols, class_rows=class_rows):
            def state_rows(n):
                return pl.ds(pl.multiple_of(r * class_rows + n * BLOCK, BLOCK), BLOCK)

            def with_prev(part):
                return lambda pair: jnp.concatenate(
                    [prev_ref[0, r, :, cols(part, pair)],
                     cur_ref[0, r, :BLOCK, cols(part, pair)]], axis=0)

            attend(g, lambda pair: cur_ref[0, r, :BLOCK, cols(0, pair)], with_prev(1),
                   with_prev(2), state_rows(0), first=True)

            def block(n, c):
                row0 = pl.multiple_of(n * BLOCK, BLOCK)
                keys = pl.ds(row0 - BLOCK, 2 * BLOCK)
                attend(g, lambda pair: cur_ref[0, r, pl.ds(row0, BLOCK), cols(0, pair)],
                       lambda pair: cur_ref[0, r, keys, cols(1, pair)],
                       lambda pair: cur_ref[0, r, keys, cols(2, pair)],
                       state_rows(n), first=False)
                return c

            if nblk > 1:
                lax.fori_loop(1, nblk, block, 0, unroll=_unroll(nblk - 1))
            return carry

        if dilation == 1:
            residue(0, 0)
        else:
            lax.fori_loop(0, dilation, residue, 0, unroll=_unroll(dilation, nblk))

    mid = DIL_GROUPS[1][1]
    assert [d for _, d in DIL_GROUPS] == [1, mid, mid * mid] and BLOCK % mid == 0
    mid_rows = DIL_SPAN // mid
    wide_rows = DIL_SPAN // (mid * mid)
    run = BLOCK // mid

    def merge(idx, carry):
        r = idx // (mid_rows // BLOCK)
        n = idx % (mid_rows // BLOCK)
        tokens = pl.ds(mid * BLOCK * n + r, BLOCK, stride=mid)
        mid_tokens = pl.ds(pl.multiple_of(r * mid_rows + n * BLOCK, BLOCK), BLOCK)
        for pair in range(pairs):
            def tiles(kind):
                for a in range(mid):
                    wide = pl.ds(pl.multiple_of((mid * a + r) * wide_rows + run * n, run), run)
                    gather_ref[pair, kind, pl.ds(a, run, stride=mid), :] = (
                        state_ref[2, pair, kind, wide, :])
                return (state_ref[0, pair, kind, tokens, :],
                        state_ref[1, pair, kind, mid_tokens, :], gather_ref[pair, kind])

            lses = tiles(1)
            top = jnp.maximum(jnp.maximum(lses[0], lses[1]), lses[2])
            weights = [jnp.exp2(lse - top) for lse in lses]
            num = sum(w * out for w, out in zip(weights, tiles(0)))
            stage_ref[pair, tokens, :] = num / sum(weights)
        return carry

    lax.fori_loop(0, DIL_SPAN // BLOCK, merge, 0)
    for pair in range(HEADS_PER_GROUP // 2):
        o_ref[0, :, pair * LANES:(pair + 1) * LANES] = stage_ref[pair].astype(o_ref.dtype)


def _dilated_mixture(zs):
    batch, _, seq, wg = zs[0].shape
    assert seq % DIL_SPAN == 0
    in_specs, operands = [], []
    for g, (_, dilation) in enumerate(DIL_GROUPS):
        rows = DIL_SPAN // dilation
        assert rows % BLOCK == 0 and zs[g].shape == (batch, dilation, seq // dilation, wg)
        nblk = rows // BLOCK
        in_specs.append(pl.BlockSpec((1, dilation, rows, wg), lambda b, t: (b, 0, t, 0)))
        in_specs.append(pl.BlockSpec(
            (1, dilation, BLOCK, wg),
            lambda b, t, nblk=nblk: (b, 0, jnp.maximum(t * nblk - 1, 0), 0)))
        operands += [zs[g], zs[g]]
    bias = jnp.asarray(_band_bias())
    return pl.pallas_call(
        _dilated_kernel,
        grid=(batch, seq // DIL_SPAN),
        in_specs=in_specs + [_resident(bias.shape)],
        out_specs=pl.BlockSpec((1, DIL_SPAN, WIDTH_G), lambda b, t: (b, t, 0)),
        out_shape=jax.ShapeDtypeStruct((batch, seq, WIDTH_G), BF16),
        scratch_shapes=[pltpu.VMEM((N_GROUPS, HEADS_PER_GROUP // 2, 2, DIL_SPAN, LANES), F32),
                        pltpu.VMEM((HEADS_PER_GROUP // 2, 2, BLOCK, LANES), F32),
                        pltpu.VMEM((HEADS_PER_GROUP // 2, DIL_SPAN, LANES), F32),
                        pltpu.VMEM(bias.shape, F32)],
        compiler_params=_params(2),
        name="dilated_attention",
    )(*operands, bias)


def _log_sigmoid(x):
    return jnp.minimum(x, 0.0) - jnp.log1p(jnp.exp(-jnp.abs(x)))


def _fox_place():
    place = np.zeros((WIDTH_B + DECAY_PIECES * F_COLS, N_HEADS_B * LANES), np.float32)
    for c in range(WIDTH_B):
        place[c, (c // HEAD_DIM) * LANES + c % HEAD_DIM] = 1.0
    for i in range(DECAY_PIECES):
        for h in range(N_HEADS_B):
            place[WIDTH_B + i * F_COLS + h, h * LANES + HEAD_DIM + i] = 1.0
    return place


def _fox_kernel(qt_ref, k_ref, vt_ref, fz_ref, b_ref, tri_ref, place_ref, o_ref,
                kaug_ref, acc_ref, s_ref):
    tq = qt_ref.shape[2]
    seq = k_ref.shape[1]
    qb = pl.program_id(1)
    q0 = qb * tq
    n_full = qb
    key_minus_query = (lax.broadcasted_iota(jnp.int32, (FOX_TKV, tq), 0)
                       - lax.broadcasted_iota(jnp.int32, (FOX_TKV, tq), 1))

    @pl.when(qb == 0)
    def _():
        tri = tri_ref[...]
        carry = jnp.zeros((1, F_COLS), F32)
        for c in range(seq // FOX_TKV):
            rows = slice(c * FOX_TKV, (c + 1) * FOX_TKV)
            sums = []
            for part in range(FOX_TKV // DECAY_CHUNK):
                at = c * FOX_TKV + part * DECAY_CHUNK
                lf = _log_sigmoid(fz_ref[0, at:at + DECAY_CHUNK, :] + b_ref[...])
                sums.append(sum(_dot(tri, piece) for piece in _split_bf16(lf)) + carry)
                carry = sums[-1][DECAY_CHUNK - 1:, :]
            pieces = _split_bf16(jnp.concatenate(sums, axis=0) * (-LOG2E))
            packed = jnp.concatenate([k_ref[0, rows, :]] + pieces, axis=1)
            kaug_ref[rows, :] = _dot(packed, place_ref[...]).astype(BF16)

    acc_ref[...] = jnp.zeros_like(acc_ref)

    def scores(h, k0, masked):
        keys = pl.ds(pl.multiple_of(k0, FOX_TKV), FOX_TKV)
        s = _dot(kaug_ref[keys, h * LANES:(h + 1) * LANES],
                 qt_ref[0, h * Q_ROWS:(h + 1) * Q_ROWS, :])
        if masked:
            s = jnp.where(key_minus_query <= q0 - k0, s, NEG)
        s_ref[h % 2] = s
        return jnp.max(s, axis=0, keepdims=True)

    def kv_block(j, carry, diagonal):
        ms, top_next = carry
        k0 = pl.multiple_of(j * FOX_TKV, FOX_TKV)
        new_ms = []
        for h in range(N_HEADS_B):
            m_new = jnp.maximum(ms[h], top_next)
            if h + 1 < N_HEADS_B:
                top_next = scores(h + 1, k0, masked=diagonal)
            elif not diagonal:
                top_next = scores(0, k0 + FOX_TKV, masked=True)
            alpha = jnp.exp2(ms[h] - m_new)
            p = jnp.exp2(s_ref[h % 2] - m_new).astype(BF16)
            vt = vt_ref[0, h * V_ROWS:(h + 1) * V_ROWS, pl.ds(k0, FOX_TKV)]
            acc_ref[h] = acc_ref[h] * alpha + _dot(vt, p)
            new_ms.append(m_new)
        return tuple(new_ms), top_next

    ms = tuple(jnp.full((1, tq), NEG, F32) for _ in range(N_HEADS_B))
    carry = (ms, scores(0, 0, masked=True))
    carry = lax.fori_loop(0, n_full, functools.partial(kv_block, diagonal=False), carry)
    kv_block(n_full, carry, diagonal=True)
    pad = jnp.zeros((LANES - HEAD_DIM, tq), F32)
    for h in range(N_HEADS_B):
        acc = acc_ref[h]
        out_t = jnp.concatenate([acc[:HEAD_DIM] / acc[HEAD_DIM:HEAD_DIM + 1], pad], axis=0)
        o_ref[0, :, h * HEAD_DIM:(h + 1) * HEAD_DIM] = (
            out_t.T[:, :HEAD_DIM].astype(o_ref.dtype))


def _forgetting_attention(qt, k, vt, fz, f_bias, l):
    batch, seq, _ = k.shape
    assert FOX_TKV % DECAY_CHUNK == 0
    place = jnp.asarray(_fox_place(), dtype=BF16)
    idx = np.arange(DECAY_CHUNK)
    tri = jnp.asarray(idx[None, :] <= idx[:, None], dtype=BF16)
    return pl.pallas_call(
        _fox_kernel,
        grid=(batch, seq // FOX_TQ),
        in_specs=[pl.BlockSpec((1, N_HEADS_B * Q_ROWS, FOX_TQ), lambda b, i: (b, 0, i)),
                  pl.BlockSpec((1, seq, WIDTH_B), lambda b, i: (b, 0, 0)),
                  pl.BlockSpec((1, N_HEADS_B * V_ROWS, seq), lambda b, i: (b, 0, 0)),
                  pl.BlockSpec((1, seq, F_COLS), lambda b, i: (b, 0, 0)),
                  _of_layer(f_bias, l), _resident(tri.shape), _resident(place.shape)],
        out_specs=pl.BlockSpec((1, FOX_TQ, WIDTH_B), lambda b, i: (b, i, 0)),
        out_shape=jax.ShapeDtypeStruct((batch, seq, WIDTH_B), BF16),
        scratch_shapes=[pltpu.VMEM((seq, N_HEADS_B * LANES), BF16),
                        pltpu.VMEM((N_HEADS_B, V_ROWS, FOX_TQ), F32),
                        pltpu.VMEM((2, FOX_TKV, FOX_TQ), F32)],
        compiler_params=_params(2),
        name="forgetting_attention",
    )(qt, k, vt, fz, f_bias, tri, place)


def _outproj_kernel(x_ref, sc_ref, sh_ref, gate_ref, ya_ref, yb_ref,
                    g_ref, wg_ref, wua_ref, wub_ref, wo_ref, o_ref, merged_ref):
    x = x_ref[0]
    d = x.shape[-1]
    h = _modulated_norm(x, g_ref[...], sc_ref[...], sh_ref[...]).astype(BF16)
    ya = ya_ref[0]
    yb = yb_ref[0]
    for c in range(d // OUT_CHUNK):
        cs = slice(c * OUT_CHUNK, (c + 1) * OUT_CHUNK)
        cs_b = slice(d + c * OUT_CHUNK, d + (c + 1) * OUT_CHUNK)
        gate_a = jax.nn.sigmoid(_dot(h, wg_ref[:, cs]))
        gate_b = jax.nn.sigmoid(_dot(h, wg_ref[:, cs_b]))
        merged = gate_a * _dot(ya, wua_ref[:, cs]) + gate_b * _dot(yb, wub_ref[:, cs])
        merged_ref[:, cs] = merged.astype(BF16)
    o_ref[0] = x + gate_ref[...] * _dot(merged_ref[...], wo_ref[...])


def _out_projection(x, mod, ya, yb, params, l):
    batch, seq, d = x.shape
    tok = lambda w: pl.BlockSpec((1, TM, w), lambda b, i: (b, i, 0))
    return pl.pallas_call(
        _outproj_kernel,
        grid=(batch, seq // TM),
        in_specs=[tok(d), _mod_vector(mod, l, MOD_SCALE1), _mod_vector(mod, l, MOD_SHIFT1),
                  _mod_vector(mod, l, MOD_GATE1), tok(WIDTH_G), tok(WIDTH_B)]
        + [_of_layer(p, l) for p in params],
        out_specs=tok(d),
        out_shape=jax.ShapeDtypeStruct((batch, seq, d), F32),
        scratch_shapes=[pltpu.VMEM((TM, d), BF16)],
        compiler_params=_params(2),
        name="out_projection",
    )(x, mod, mod, mod, ya, yb, *params)


def _ffn_kernel(*refs, final_norm):
    x_ref, sc_ref, sh_ref, gate_ref, g_ref, win_ref, wout_ref = refs[:7]
    refs = refs[7:]
    if final_norm:
        gf_ref = refs[0]
        refs = refs[1:]
    o_ref, h_ref, acc_ref = refs
    x = x_ref[0]
    h_ref[...] = _modulated_norm(x, g_ref[...], sc_ref[...], sh_ref[...]).astype(BF16)
    for c in range(D_FF // FF_CHUNK):
        cs = slice(c * FF_CHUNK, (c + 1) * FF_CHUNK)
        cs_up = slice(D_FF + c * FF_CHUNK, D_FF + (c + 1) * FF_CHUNK)
        h = h_ref[...]
        gt = _dot(h, win_ref[:, cs])
        up = _dot(h, win_ref[:, cs_up])
        act = ((gt * jax.nn.sigmoid(gt)) * up).astype(BF16)
        part = _dot(act, wout_ref[cs, :])
        if c == 0:
            acc_ref[...] = part
        else:
            acc_ref[...] += part
    y = x + gate_ref[...] * acc_ref[...]
    if final_norm:
        y = (y * lax.rsqrt(jnp.mean(y * y, axis=-1, keepdims=True) + EPS)) * gf_ref[...]
    o_ref[0] = y


def _ffn(x, mod, params, l, g_final):
    batch, seq, d = x.shape
    final_norm = g_final is not None
    tok = pl.BlockSpec((1, TM, d), lambda b, i: (b, i, 0))
    in_specs = [tok, _mod_vector(mod, l, MOD_SCALE2), _mod_vector(mod, l, MOD_SHIFT2),
                _mod_vector(mod, l, MOD_GATE2)] + [_of_layer(p, l) for p in params]
    operands = [x, mod, mod, mod, *params]
    if final_norm:
        in_specs.append(_resident((1, d)))
        operands.append(g_final)
    return pl.pallas_call(
        functools.partial(_ffn_kernel, final_norm=final_norm),
        grid=(batch, seq // TM),
        in_specs=in_specs,
        out_specs=tok,
        out_shape=jax.ShapeDtypeStruct((batch, seq, d), F32),
        scratch_shapes=[pltpu.VMEM((TM, d), BF16), pltpu.VMEM((TM, d), F32)],
        compiler_params=_params(2),
        name="swiglu_ffn",
    )(*operands)


def _split_w_in(w_in):
    qa, ka, va, qb, kb, vb, fz, gza, gzb = jnp.split(
        w_in, np.cumsum((WIDTH_A, WIDTH_A, WIDTH_A, WIDTH_B, WIDTH_B, WIDTH_B, N_HEADS_B,
                         D_MODEL)).tolist(), axis=-1)
    cols = []
    for g in range(N_GROUPS):
        gs = slice(g * WIDTH_G, (g + 1) * WIDTH_G)
        cols += [qa[..., gs] * (QK_SCALE * LOG2E), ka[..., gs], va[..., gs]]
    wa = jnp.concatenate(cols, axis=-1).astype(BF16)
    pad = jnp.zeros(w_in.shape[:2] + (F_COLS - N_HEADS_B,), w_in.dtype)
    wqt = jnp.swapaxes(qb * (QK_SCALE * LOG2E), 1, 2).astype(BF16)
    wvt = jnp.swapaxes(vb, 1, 2).astype(BF16)
    wkf = jnp.concatenate([kb, fz, pad], axis=-1).astype(BF16)
    wg = jnp.concatenate([gza, gzb], axis=-1).astype(BF16)
    return wa, wqt, wvt, wkf, wg


def kernel(x, c, w_ada, b_ada, norm_mix, w_in, b_forget, w_up_a, w_up_b, w_out,
           norm_ffn, w_ffn_in, w_ffn_out, norm_final):
    depth = w_ada.shape[0]
    batch, seq, d = x.shape
    assert seq % TM == 0 and seq % FOX_TQ == 0 and FOX_TQ == FOX_TKV
    mod = _modulation(c, w_ada, b_ada).reshape(depth, batch, N_MOD, 1, d)
    wa, wqt, wvt, wkf, wg = _split_w_in(w_in)
    in_params = (norm_mix.reshape(depth, 1, d), wa, wqt, wvt, wkf)
    out_params = (norm_mix.reshape(depth, 1, d), wg, w_up_a.astype(BF16), w_up_b.astype(BF16),
                  w_out.astype(BF16))
    ffn_params = (norm_ffn.reshape(depth, 1, d), w_ffn_in.astype(BF16), w_ffn_out.astype(BF16))
    f_bias = jnp.pad(b_forget.astype(F32), ((0, 0), (0, F_COLS - N_HEADS_B)))[:, None, :]
    for l in range(depth):
        za0, za1, za2, qt, k, vt, fz = _in_projection(x, mod, in_params, l)
        ya = _dilated_mixture((za0, za1, za2))
        yb = _forgetting_attention(qt, k, vt, fz, f_bias, l)
        x = _out_projection(x, mod, ya, yb, out_params, l)
        g_final = norm_final.reshape(1, d) if l == depth - 1 else None
        x = _ffn(x, mod, ffn_params, l, g_final)
    return x
```

```python
import functools

import numpy as np
import jax
import jax.numpy as jnp
from jax import lax
from jax.experimental import pallas as pl
from jax.experimental.pallas import tpu as pltpu

F32 = jnp.float32
BF16 = jnp.bfloat16

D_MODEL = 1024
HEAD_DIM = 64
DIL_GROUPS = ((128, 1), (512, 4), (2048, 16))
HEADS_PER_GROUP = 4
N_GROUPS = len(DIL_GROUPS)
N_HEADS_A = HEADS_PER_GROUP * N_GROUPS
N_HEADS_B = 4
WIDTH_G = HEADS_PER_GROUP * HEAD_DIM
WIDTH_A = N_HEADS_A * HEAD_DIM
WIDTH_B = N_HEADS_B * HEAD_DIM
D_FF = -(-8 * D_MODEL // (3 * 256)) * 256
BLOCK = 128
ALIBI_MAX = 8.0
EPS = 1e-6
N_MOD = 6
MOD_SHIFT1, MOD_SCALE1, MOD_GATE1, MOD_SHIFT2, MOD_SCALE2, MOD_GATE2 = range(N_MOD)
QK_SCALE = HEAD_DIM ** -0.5

NEG = -1e30
LANES = 128
F_COLS = LANES
BF16_ROWS = 16
V_ROWS = HEAD_DIM + BF16_ROWS
Q_ROWS = LANES
DECAY_PIECES = 3
LOG2E = 1.4426950408889634

VMEM_LIMIT = 56 * 1024 * 1024

TM = 1024
W_ROWS = 256
FF_CHUNK = 256
OUT_CHUNK = 256
DECAY_CHUNK = 256
FOX_TQ = 512
FOX_TKV = 512
DIL_SPAN = BLOCK * max(d for _, d in DIL_GROUPS)
DIL_BLOCKS_IN_FLIGHT = 8


def _params(n_axes, flags=None):
    return pltpu.CompilerParams(dimension_semantics=("arbitrary",) * n_axes,
                                vmem_limit_bytes=VMEM_LIMIT, flags=flags)


def _resident(shape):
    zeros = (0,) * len(shape)
    return pl.BlockSpec(shape, lambda *_: zeros, pipeline_mode=pl.Buffered(1))


def _of_layer(stacked, l):
    tail = stacked.shape[1:]
    index = (l,) + (0,) * len(tail)
    return pl.BlockSpec((None,) + tail, lambda *_: index, pipeline_mode=pl.Buffered(1))


def _mod_vector(mod, l, j):
    return pl.BlockSpec((None, None, None) + mod.shape[3:], lambda b, i: (l, b, j, 0, 0))


def _alibi_slopes():
    h = np.arange(1, N_HEADS_A + 1, dtype=np.float32)
    return np.asarray(2.0 ** (-ALIBI_MAX * h / N_HEADS_A), dtype=np.float32)


def _modulated_norm(x, g, sc, sh):
    y = x * lax.rsqrt(jnp.mean(x * x, axis=-1, keepdims=True) + EPS)
    return (y * g) * (1.0 + sc) + sh


def _dot(a, b):
    return jnp.dot(a, b, preferred_element_type=F32)


def _dot_nt(a, b):
    return lax.dot_general(a, b, (((1,), (1,)), ((), ())), preferred_element_type=F32)


def _split_bf16(x):
    pieces = []
    for _ in range(DECAY_PIECES):
        piece = x.astype(BF16)
        pieces.append(piece)
        x = x - piece.astype(F32)
    return pieces


def _mod_kernel(c_ref, w_ref, b_ref, o_ref):
    c = c_ref[...]
    c_act = (c * jax.nn.sigmoid(c)).astype(BF16)
    o_ref[0] = _dot(c_act, w_ref[0].astype(BF16)) + b_ref[0]


def _modulation(c, w_ada, b_ada):
    depth, d, n = w_ada.shape
    batch = c.shape[0]
    cols = 2 * d
    return pl.pallas_call(
        _mod_kernel,
        grid=(depth, n // cols),
        in_specs=[pl.BlockSpec((batch, d), lambda l, j: (0, 0)),
                  pl.BlockSpec((1, d, cols), lambda l, j: (l, 0, j)),
                  pl.BlockSpec((1, 1, cols), lambda l, j: (l, 0, j))],
        out_specs=pl.BlockSpec((1, batch, cols), lambda l, j: (l, 0, j)),
        out_shape=jax.ShapeDtypeStruct((depth, batch, n), F32),
        compiler_params=_params(2),
        name="adaln_modulation",
    )(c, w_ada, b_ada.reshape(depth, 1, n))


def _inproj_kernel(x_ref, sc_ref, sh_ref, g_ref, wa_ref, wqt_ref, wvt_ref, wkf_ref,
                   za0_ref, za1_ref, za2_ref, qt_ref, k_ref, vt_ref, fz_ref, stage_ref):
    h = _modulated_norm(x_ref[0], g_ref[...], sc_ref[...], sh_ref[...]).astype(BF16)
    tm = h.shape[0]
    wg = 3 * WIDTH_G
    for g, za_ref in enumerate((za0_ref, za1_ref, za2_ref)):
        dilation = DIL_GROUPS[g][1]
        za = _dot(h, wa_ref[:, g * wg:(g + 1) * wg])
        if dilation == 1:
            za_ref[0, 0] = za.astype(BF16)
            continue
        for c in range(wg // LANES):
            stage_ref[c] = za[:, c * LANES:(c + 1) * LANES]
        for r in range(dilation):
            rows = pl.ds(r, tm // dilation, stride=dilation)
            for c in range(wg // LANES):
                za_ref[0, r, :, c * LANES:(c + 1) * LANES] = stage_ref[c, rows, :].astype(BF16)
    qt = _dot_nt(wqt_ref[...], h).astype(BF16)
    vt = _dot_nt(wvt_ref[...], h).astype(BF16)
    q_tail = (lax.broadcasted_iota(jnp.int32, (Q_ROWS - HEAD_DIM, tm), 0)
              < DECAY_PIECES).astype(BF16)
    v_tail = (lax.broadcasted_iota(jnp.int32, (V_ROWS - HEAD_DIM, tm), 0) == 0).astype(BF16)
    for hd in range(N_HEADS_B):
        heads = slice(hd * HEAD_DIM, (hd + 1) * HEAD_DIM)
        qt_ref[0, hd * Q_ROWS:hd * Q_ROWS + HEAD_DIM, :] = qt[heads]
        qt_ref[0, hd * Q_ROWS + HEAD_DIM:(hd + 1) * Q_ROWS, :] = q_tail
        vt_ref[0, hd * V_ROWS:hd * V_ROWS + HEAD_DIM, :] = vt[heads]
        vt_ref[0, hd * V_ROWS + HEAD_DIM:(hd + 1) * V_ROWS, :] = v_tail
    kf = _dot(h, wkf_ref[...])
    k_ref[0] = kf[:, :WIDTH_B].astype(BF16)
    fz_ref[0] = kf[:, WIDTH_B:]


def _in_projection(x, mod, params, l):
    batch, seq, d = x.shape
    wg = 3 * WIDTH_G
    tok = lambda w: pl.BlockSpec((1, TM, w), lambda b, i: (b, i, 0))
    tok_t = lambda rows: pl.BlockSpec((1, rows, TM), lambda b, i: (b, 0, i))
    dils = [dilation for _, dilation in DIL_GROUPS]
    assert all(TM % (dilation * BF16_ROWS) == 0 for dilation in dils)
    sub_major = [pl.BlockSpec((1, dilation, TM // dilation, wg), lambda b, i: (b, 0, i, 0))
                 for dilation in dils]
    return pl.pallas_call(
        _inproj_kernel,
        grid=(batch, seq // TM),
        in_specs=[tok(d), _mod_vector(mod, l, MOD_SCALE1), _mod_vector(mod, l, MOD_SHIFT1)]
        + [_of_layer(p, l) for p in params],
        out_specs=sub_major + [tok_t(N_HEADS_B * Q_ROWS), tok(WIDTH_B),
                               tok_t(N_HEADS_B * V_ROWS), tok(F_COLS)],
        out_shape=[jax.ShapeDtypeStruct((batch, dilation, seq // dilation, wg), BF16)
                   for dilation in dils]
        + [jax.ShapeDtypeStruct((batch, N_HEADS_B * Q_ROWS, seq), BF16),
           jax.ShapeDtypeStruct((batch, seq, WIDTH_B), BF16),
           jax.ShapeDtypeStruct((batch, N_HEADS_B * V_ROWS, seq), BF16),
           jax.ShapeDtypeStruct((batch, seq, F_COLS), F32)],
        scratch_shapes=[pltpu.VMEM((wg // LANES, TM, LANES), F32)],
        compiler_params=_params(2),
        name="in_projection",
    )(x, mod, mod, *params)


def _band_bias():
    qi = np.arange(BLOCK)[:, None]
    kj = np.arange(2 * BLOCK)[None, :]
    dist = qi + BLOCK - kj
    in_band = (dist >= 0) & (dist <= BLOCK)
    slopes = _alibi_slopes()
    bias = np.empty((N_HEADS_A, BLOCK, 2 * BLOCK), np.float32)
    for g, (window, dilation) in enumerate(DIL_GROUPS):
        assert window // dilation == BLOCK
        for h in range(HEADS_PER_GROUP):
            head = g * HEADS_PER_GROUP + h
            alibi = -slopes[head] * (dist * dilation).astype(np.float32)
            bias[head] = np.where(in_band, alibi * np.float32(LOG2E), np.float32(NEG))
    return bias.reshape(N_HEADS_A // 2, 2 * BLOCK, 2 * BLOCK)


def _unroll(trips, blocks_per_trip=1):
    fits = [u for u in range(1, trips + 1)
            if trips % u == 0 and u * blocks_per_trip <= DIL_BLOCKS_IN_FLIGHT]
    return max(fits, default=1)


def _dilated_kernel(c0_ref, p0_ref, c1_ref, p1_ref, c2_ref, p2_ref, bias_ref, o_ref,
                    state_ref, gather_ref, stage_ref, first_bias_ref):
    span = pl.program_id(1)
    pairs = HEADS_PER_GROUP // 2
    kj = lax.broadcasted_iota(jnp.int32, (2 * BLOCK, 2 * BLOCK), 1)
    no_prev = jnp.where(kj >= jnp.where(span == 0, BLOCK, 0), 0.0, NEG)
    for slab in range(N_GROUPS * pairs):
        first_bias_ref[slab] = bias_ref[slab] + no_prev
    low_half = lax.broadcasted_iota(jnp.int32, (BLOCK, LANES), 1) < HEAD_DIM
    own_half = jnp.concatenate([low_half, jnp.logical_not(low_half)], axis=0)

    def attend(g, q_of, k_of, v_of, rows, first):
        for pair in range(pairs):
            q = q_of(pair)
            q = jnp.where(own_half, jnp.concatenate([q, q], axis=0), jnp.zeros((), q.dtype))
            bias = (first_bias_ref if first else bias_ref)[g * pairs + pair]
            z = _dot_nt(q, k_of(pair)) + bias
            m = jnp.max(z, axis=-1, keepdims=True)
            p = jnp.exp2(z - m)
            l = jnp.sum(p, axis=-1, keepdims=True)
            out = _dot(p.astype(BF16), v_of(pair)) / l
            lse = m + jnp.log2(l)
            state_ref[g, pair, 0, rows, :] = jnp.where(low_half, out[:BLOCK], out[BLOCK:])
            state_ref[g, pair, 1, rows, :] = jnp.where(low_half, lse[:BLOCK], lse[BLOCK:])

    for g, (cur_ref, prev_ref) in enumerate(((c0_ref, p0_ref), (c1_ref, p1_ref),
                                             (c2_ref, p2_ref))):
        dilation = DIL_GROUPS[g][1]
        class_rows = cur_ref.shape[2]
        nblk = class_rows // BLOCK

        def cols(part, pair):
            return slice(part * WIDTH_G + pair * LANES, part * WIDTH_G + (pair + 1) * LANES)

        def residue(r, carry, g=g, cur_ref=cur_ref, prev_ref=prev_ref, nblk=nblk,
                    cols=cols, class_rows=class_rows):
            def state_rows(n):
                return pl.ds(pl.multiple_of(r * class_rows + n * BLOCK, BLOCK), BLOCK)

            def with_prev(part):
                return lambda pair: jnp.concatenate(
                    [prev_ref[0, r, :, cols(part, pair)],
                     cur_ref[0, r, :BLOCK, cols(part, pair)]], axis=0)

            attend(g, lambda pair: cur_ref[0, r, :BLOCK, cols(0, pair)], with_prev(1),
                   with_prev(2), state_rows(0), first=True)

            def block(n, c):
                row0 = pl.multiple_of(n * BLOCK, BLOCK)
                keys = pl.ds(row0 - BLOCK, 2 * BLOCK)
                attend(g, lambda pair: cur_ref[0, r, pl.ds(row0, BLOCK), cols(0, pair)],
                       lambda pair: cur_ref[0, r, keys, cols(1, pair)],
                       lambda pair: cur_ref[0, r, keys, cols(2, pair)],
                       state_rows(n), first=False)
                return c

            if nblk > 1:
                lax.fori_loop(1, nblk, block, 0, unroll=_unroll(nblk - 1))
            return carry

        if dilation == 1:
            residue(0, 0)
        else:
            lax.fori_loop(0, dilation, residue, 0, unroll=_unroll(dilation, nblk))

    mid = DIL_GROUPS[1][1]
    assert [d for _, d in DIL_GROUPS] == [1, mid, mid * mid] and BLOCK % mid == 0
    mid_rows = DIL_SPAN // mid
    wide_rows = DIL_SPAN // (mid * mid)
    run = BLOCK // mid

    def merge(idx, carry):
        r = idx // (mid_rows // BLOCK)
        n = idx % (mid_rows // BLOCK)
        tokens = pl.ds(mid * BLOCK * n + r, BLOCK, stride=mid)
        mid_tokens = pl.ds(pl.multiple_of(r * mid_rows + n * BLOCK, BLOCK), BLOCK)
        for pair in range(pairs):
            def tiles(kind):
                for a in range(mid):
                    wide = pl.ds(pl.multiple_of((mid * a + r) * wide_rows + run * n, run), run)
                    gather_ref[pair, kind, pl.ds(a, run, stride=mid), :] = (
                        state_ref[2, pair, kind, wide, :])
                return (state_ref[0, pair, kind, tokens, :],
                        state_ref[1, pair, kind, mid_tokens, :], gather_ref[pair, kind])

            lses = tiles(1)
            top = jnp.maximum(jnp.maximum(lses[0], lses[1]), lses[2])
            weights = [jnp.exp2(lse - top) for lse in lses]
            num = sum(w * out for w, out in zip(weights, tiles(0)))
            stage_ref[pair, tokens, :] = num / sum(weights)
        return carry

    lax.fori_loop(0, DIL_SPAN // BLOCK, merge, 0)
    for pair in range(HEADS_PER_GROUP // 2):
        o_ref[0, :, pair * LANES:(pair + 1) * LANES] = stage_ref[pair].astype(o_ref.dtype)


def _dilated_mixture(zs):
    batch, _, seq, wg = zs[0].shape
    assert seq % DIL_SPAN == 0
    in_specs, operands = [], []
    for g, (_, dilation) in enumerate(DIL_GROUPS):
        rows = DIL_SPAN // dilation
        assert rows % BLOCK == 0 and zs[g].shape == (batch, dilation, seq // dilation, wg)
        nblk = rows // BLOCK
        in_specs.append(pl.BlockSpec((1, dilation, rows, wg), lambda b, t: (b, 0, t, 0)))
        in_specs.append(pl.BlockSpec(
            (1, dilation, BLOCK, wg),
            lambda b, t, nblk=nblk: (b, 0, jnp.maximum(t * nblk - 1, 0), 0)))
        operands += [zs[g], zs[g]]
    bias = jnp.asarray(_band_bias())
    return pl.pallas_call(
        _dilated_kernel,
        grid=(batch, seq // DIL_SPAN),
        in_specs=in_specs + [_resident(bias.shape)],
        out_specs=pl.BlockSpec((1, DIL_SPAN, WIDTH_G), lambda b, t: (b, t, 0)),
        out_shape=jax.ShapeDtypeStruct((batch, seq, WIDTH_G), BF16),
        scratch_shapes=[pltpu.VMEM((N_GROUPS, HEADS_PER_GROUP // 2, 2, DIL_SPAN, LANES), F32),
                        pltpu.VMEM((HEADS_PER_GROUP // 2, 2, BLOCK, LANES), F32),
                        pltpu.VMEM((HEADS_PER_GROUP // 2, DIL_SPAN, LANES), F32),
                        pltpu.VMEM(bias.shape, F32)],
        compiler_params=_params(2),
        name="dilated_attention",
    )(*operands, bias)


def _log_sigmoid(x):
    return jnp.minimum(x, 0.0) - jnp.log1p(jnp.exp(-jnp.abs(x)))


def _fox_place():
    place = np.zeros((WIDTH_B + DECAY_PIECES * F_COLS, N_HEADS_B * LANES), np.float32)
    for c in range(WIDTH_B):
        place[c, (c // HEAD_DIM) * LANES + c % HEAD_DIM] = 1.0
    for i in range(DECAY_PIECES):
        for h in range(N_HEADS_B):
            place[WIDTH_B + i * F_COLS + h, h * LANES + HEAD_DIM + i] = 1.0
    return place


def _fox_kernel(qt_ref, k_ref, vt_ref, fz_ref, b_ref, tri_ref, place_ref, o_ref,
                kaug_ref, acc_ref, s_ref):
    tq = qt_ref.shape[2]
    seq = k_ref.shape[1]
    qb = pl.program_id(1)
    q0 = qb * tq
    n_full = qb
    key_minus_query = (lax.broadcasted_iota(jnp.int32, (FOX_TKV, tq), 0)
                       - lax.broadcasted_iota(jnp.int32, (FOX_TKV, tq), 1))

    @pl.when(qb == 0)
    def _():
        tri = tri_ref[...]
        carry = jnp.zeros((1, F_COLS), F32)
        for c in range(seq // FOX_TKV):
            rows = slice(c * FOX_TKV, (c + 1) * FOX_TKV)
            sums = []
            for part in range(FOX_TKV // DECAY_CHUNK):
                at = c * FOX_TKV + part * DECAY_CHUNK
                lf = _log_sigmoid(fz_ref[0, at:at + DECAY_CHUNK, :] + b_ref[...])
                sums.append(sum(_dot(tri, piece) for piece in _split_bf16(lf)) + carry)
                carry = sums[-1][DECAY_CHUNK - 1:, :]
            pieces = _split_bf16(jnp.concatenate(sums, axis=0) * (-LOG2E))
            packed = jnp.concatenate([k_ref[0, rows, :]] + pieces, axis=1)
            kaug_ref[rows, :] = _dot(packed, place_ref[...]).astype(BF16)

    acc_ref[...] = jnp.zeros_like(acc_ref)

    def scores(h, k0, masked):
        keys = pl.ds(pl.multiple_of(k0, FOX_TKV), FOX_TKV)
        s = _dot(kaug_ref[keys, h * LANES:(h + 1) * LANES],
                 qt_ref[0, h * Q_ROWS:(h + 1) * Q_ROWS, :])
        if masked:
            s = jnp.where(key_minus_query <= q0 - k0, s, NEG)
        s_ref[h % 2] = s
        return jnp.max(s, axis=0, keepdims=True)

    def kv_block(j, carry, diagonal):
        ms, top_next = carry
        k0 = pl.multiple_of(j * FOX_TKV, FOX_TKV)
        new_ms = []
        for h in range(N_HEADS_B):
            m_new = jnp.maximum(ms[h], top_next)
            if h + 1 < N_HEADS_B:
                top_next = scores(h + 1, k0, masked=diagonal)
            elif not diagonal:
                top_next = scores(0, k0 + FOX_TKV, masked=True)
            alpha = jnp.exp2(ms[h] - m_new)
            p = jnp.exp2(s_ref[h % 2] - m_new).astype(BF16)
            vt = vt_ref[0, h * V_ROWS:(h + 1) * V_ROWS, pl.ds(k0, FOX_TKV)]
            acc_ref[h] = acc_ref[h] * alpha + _dot(vt, p)
            new_ms.append(m_new)
        return tuple(new_ms), top_next

    ms = tuple(jnp.full((1, tq), NEG, F32) for _ in range(N_HEADS_B))
    carry = (ms, scores(0, 0, masked=True))
    carry = lax.fori_loop(0, n_full, functools.partial(kv_block, diagonal=False), carry)
    kv_block(n_full, carry, diagonal=True)
    pad = jnp.zeros((LANES - HEAD_DIM, tq), F32)
    for h in range(N_HEADS_B):
        acc = acc_ref[h]
        out_t = jnp.concatenate([acc[:HEAD_DIM] / acc[HEAD_DIM:HEAD_DIM + 1], pad], axis=0)
        o_ref[0, :, h * HEAD_DIM:(h + 1) * HEAD_DIM] = (
            out_t.T[:, :HEAD_DIM].astype(o_ref.dtype))


def _forgetting_attention(qt, k, vt, fz, f_bias, l):
    batch, seq, _ = k.shape
    assert FOX_TKV % DECAY_CHUNK == 0
    place = jnp.asarray(_fox_place(), dtype=BF16)
    idx = np.arange(DECAY_CHUNK)
    tri = jnp.asarray(idx[None, :] <= idx[:, None], dtype=BF16)
    return pl.pallas_call(
        _fox_kernel,
        grid=(batch, seq // FOX_TQ),
        in_specs=[pl.BlockSpec((1, N_HEADS_B * Q_ROWS, FOX_TQ), lambda b, i: (b, 0, i)),
                  pl.BlockSpec((1, seq, WIDTH_B), lambda b, i: (b, 0, 0)),
                  pl.BlockSpec((1, N_HEADS_B * V_ROWS, seq), lambda b, i: (b, 0, 0)),
                  pl.BlockSpec((1, seq, F_COLS), lambda b, i: (b, 0, 0)),
                  _of_layer(f_bias, l), _resident(tri.shape), _resident(place.shape)],
        out_specs=pl.BlockSpec((1, FOX_TQ, WIDTH_B), lambda b, i: (b, i, 0)),
        out_shape=jax.ShapeDtypeStruct((batch, seq, WIDTH_B), BF16),
        scratch_shapes=[pltpu.VMEM((seq, N_HEADS_B * LANES), BF16),
                        pltpu.VMEM((N_HEADS_B, V_ROWS, FOX_TQ), F32),
                        pltpu.VMEM((2, FOX_TKV, FOX_TQ), F32)],
        compiler_params=_params(2),
        name="forgetting_attention",
    )(qt, k, vt, fz, f_bias, tri, place)


def _outproj_kernel(x_ref, sc_ref, sh_ref, gate_ref, ya_ref, yb_ref,
                    g_ref, wg_ref, wua_ref, wub_ref, wo_ref, o_ref, merged_ref):
    x = x_ref[0]
    d = x.shape[-1]
    h = _modulated_norm(x, g_ref[...], sc_ref[...], sh_ref[...]).astype(BF16)
    ya = ya_ref[0]
    yb = yb_ref[0]
    for c in range(d // OUT_CHUNK):
        cs = slice(c * OUT_CHUNK, (c + 1) * OUT_CHUNK)
        cs_b = slice(d + c * OUT_CHUNK, d + (c + 1) * OUT_CHUNK)
        gate_a = jax.nn.sigmoid(_dot(h, wg_ref[:, cs]))
        gate_b = jax.nn.sigmoid(_dot(h, wg_ref[:, cs_b]))
        merged = gate_a * _dot(ya, wua_ref[:, cs]) + gate_b * _dot(yb, wub_ref[:, cs])
        merged_ref[:, cs] = merged.astype(BF16)
    o_ref[0] = x + gate_ref[...] * _dot(merged_ref[...], wo_ref[...])


def _out_projection(x, mod, ya, yb, params, l):
    batch, seq, d = x.shape
    tok = lambda w: pl.BlockSpec((1, TM, w), lambda b, i: (b, i, 0))
    return pl.pallas_call(
        _outproj_kernel,
        grid=(batch, seq // TM),
        in_specs=[tok(d), _mod_vector(mod, l, MOD_SCALE1), _mod_vector(mod, l, MOD_SHIFT1),
                  _mod_vector(mod, l, MOD_GATE1), tok(WIDTH_G), tok(WIDTH_B)]
        + [_of_layer(p, l) for p in params],
        out_specs=tok(d),
        out_shape=jax.ShapeDtypeStruct((batch, seq, d), F32),
        scratch_shapes=[pltpu.VMEM((TM, d), BF16)],
        compiler_params=_params(2),
        name="out_projection",
    )(x, mod, mod, mod, ya, yb, *params)


def _ffn_kernel(*refs, final_norm):
    x_ref, sc_ref, sh_ref, gate_ref, g_ref, win_ref, wout_ref = refs[:7]
    refs = refs[7:]
    if final_norm:
        gf_ref = refs[0]
        refs = refs[1:]
    o_ref, h_ref, acc_ref = refs
    x = x_ref[0]
    h_ref[...] = _modulated_norm(x, g_ref[...], sc_ref[...], sh_ref[...]).astype(BF16)
    for c in range(D_FF // FF_CHUNK):
        cs = slice(c * FF_CHUNK, (c + 1) * FF_CHUNK)
        cs_up = slice(D_FF + c * FF_CHUNK, D_FF + (c + 1) * FF_CHUNK)
        h = h_ref[...]
        gt = _dot(h, win_ref[:, cs])
        up = _dot(h, win_ref[:, cs_up])
        act = ((gt * jax.nn.sigmoid(gt)) * up).astype(BF16)
        part = _dot(act, wout_ref[cs, :])
        if c == 0:
            acc_ref[...] = part
        else:
            acc_ref[...] += part
    y = x + gate_ref[...] * acc_ref[...]
    if final_norm:
        y = (y * lax.rsqrt(jnp.mean(y * y, axis=-1, keepdims=True) + EPS)) * gf_ref[...]
    o_ref[0] = y


def _ffn(x, mod, params, l, g_final):
    batch, seq, d = x.shape
    final_norm = g_final is not None
    tok = pl.BlockSpec((1, TM, d), lambda b, i: (b, i, 0))
    in_specs = [tok, _mod_vector(mod, l, MOD_SCALE2), _mod_vector(mod, l, MOD_SHIFT2),
                _mod_vector(mod, l, MOD_GATE2)] + [_of_layer(p, l) for p in params]
    operands = [x, mod, mod, mod, *params]
    if final_norm:
        in_specs.append(_resident((1, d)))
        operands.append(g_final)
    return pl.pallas_call(
        functools.partial(_ffn_kernel, final_norm=final_norm),
        grid=(batch, seq // TM),
        in_specs=in_specs,
        out_specs=tok,
        out_shape=jax.ShapeDtypeStruct((batch, seq, d), F32),
        scratch_shapes=[pltpu.VMEM((TM, d), BF16), pltpu.VMEM((TM, d), F32)],
        compiler_params=_params(2),
        name="swiglu_ffn",
    )(*operands)


def _win_kernel(w_ref, wa_ref, wqt_ref, wvt_ref, wkf_ref, wg_ref):
    w = w_ref[0]
    scale = QK_SCALE * LOG2E
    for g in range(N_GROUPS):
        for part in range(3):
            src = part * WIDTH_A + g * WIDTH_G
            piece = w[:, src:src + WIDTH_G]
            dst = (3 * g + part) * WIDTH_G
            wa_ref[0, :, dst:dst + WIDTH_G] = (piece * scale if part == 0 else piece).astype(BF16)
    qb0 = 3 * WIDTH_A
    wqt_ref[0] = (w[:, qb0:qb0 + WIDTH_B] * scale).T.astype(BF16)
    wvt_ref[0] = w[:, qb0 + 2 * WIDTH_B:qb0 + 3 * WIDTH_B].T.astype(BF16)
    f0 = qb0 + 3 * WIDTH_B
    forget = w[:, f0:f0 + F_COLS]
    lane = lax.broadcasted_iota(jnp.int32, forget.shape, 1)
    wkf_ref[0, :, :WIDTH_B] = w[:, qb0 + WIDTH_B:qb0 + 2 * WIDTH_B].astype(BF16)
    wkf_ref[0, :, WIDTH_B:] = jnp.where(lane < N_HEADS_B, forget, 0.0).astype(BF16)
    g0 = f0 + N_HEADS_B
    wg_ref[0] = w[:, g0:g0 + 2 * D_MODEL].astype(BF16)


def _split_w_in(w_in):
    depth, d, d_in = w_in.shape
    assert d_in == 3 * WIDTH_A + 3 * WIDTH_B + N_HEADS_B + 2 * D_MODEL and d % W_ROWS == 0
    row_block = lambda cols: pl.BlockSpec((1, W_ROWS, cols), lambda l, i: (l, i, 0))
    col_block = pl.BlockSpec((1, WIDTH_B, W_ROWS), lambda l, i: (l, 0, i))
    widths = (3 * WIDTH_A, WIDTH_B + F_COLS, 2 * D_MODEL)
    stack = lambda *tail: jax.ShapeDtypeStruct((depth,) + tail, BF16)
    wa, wqt, wvt, wkf, wg = pl.pallas_call(
        _win_kernel,
        grid=(depth, d // W_ROWS),
        in_specs=[row_block(d_in)],
        out_specs=[row_block(widths[0]), col_block, col_block, row_block(widths[1]),
                   row_block(widths[2])],
        out_shape=[stack(d, widths[0]), stack(WIDTH_B, d), stack(WIDTH_B, d),
                   stack(d, widths[1]), stack(d, widths[2])],
        compiler_params=_params(2),
        name="in_projection_weight_layout",
    )(w_in)
    return wa, wqt, wvt, wkf, wg


def kernel(x, c, w_ada, b_ada, norm_mix, w_in, b_forget, w_up_a, w_up_b, w_out,
           norm_ffn, w_ffn_in, w_ffn_out, norm_final):
    depth = w_ada.shape[0]
    batch, seq, d = x.shape
    assert seq % TM == 0 and seq % FOX_TQ == 0 and FOX_TQ == FOX_TKV
    mod = _modulation(c, w_ada, b_ada).reshape(depth, batch, N_MOD, 1, d)
    wa, wqt, wvt, wkf, wg = _split_w_in(w_in)
    in_params = (norm_mix.reshape(depth, 1, d), wa, wqt, wvt, wkf)
    out_params = (norm_mix.reshape(depth, 1, d), wg, w_up_a.astype(BF16), w_up_b.astype(BF16),
                  w_out.astype(BF16))
    ffn_params = (norm_ffn.reshape(depth, 1, d), w_ffn_in.astype(BF16), w_ffn_out.astype(BF16))
    f_bias = jnp.pad(b_forget.astype(F32), ((0, 0), (0, F_COLS - N_HEADS_B)))[:, None, :]
    for l in range(depth):
        za0, za1, za2, qt, k, vt, fz = _in_projection(x, mod, in_params, l)
        ya = _dilated_mixture((za0, za1, za2))
        yb = _forgetting_attention(qt, k, vt, fz, f_bias, l)
        x = _out_projection(x, mod, ya, yb, out_params, l)
        g_final = norm_final.reshape(1, d) if l == depth - 1 else None
        x = _ffn(x, mod, ffn_params, l, g_final)
    return x
```

```python
import functools

import numpy as np
import jax
import jax.numpy as jnp
from jax import lax
from jax.experimental import pallas as pl
from jax.experimental.pallas import tpu as pltpu

F32 = jnp.float32
BF16 = jnp.bfloat16

D_MODEL = 1024
HEAD_DIM = 64
DIL_GROUPS = ((128, 1), (512, 4), (2048, 16))
HEADS_PER_GROUP = 4
N_GROUPS = len(DIL_GROUPS)
N_HEADS_A = HEADS_PER_GROUP * N_GROUPS
N_HEADS_B = 4
WIDTH_G = HEADS_PER_GROUP * HEAD_DIM
WIDTH_A = N_HEADS_A * HEAD_DIM
WIDTH_B = N_HEADS_B * HEAD_DIM
D_FF = -(-8 * D_MODEL // (3 * 256)) * 256
BLOCK = 128
ALIBI_MAX = 8.0
EPS = 1e-6
N_MOD = 6
MOD_SHIFT1, MOD_SCALE1, MOD_GATE1, MOD_SHIFT2, MOD_SCALE2, MOD_GATE2 = range(N_MOD)
QK_SCALE = HEAD_DIM ** -0.5

NEG = -1e30
LANES = 128
F_COLS = LANES
BF16_ROWS = 16
V_ROWS = HEAD_DIM + BF16_ROWS
Q_ROWS = LANES
DECAY_PIECES = 3
LOG2E = 1.4426950408889634

VMEM_LIMIT = 56 * 1024 * 1024

TM = 1024
W_ROWS = 256
FF_CHUNK = 256
OUT_CHUNK = 256
DECAY_CHUNK = 256
FOX_TQ = 512
FOX_TKV = 512
DIL_SPAN = BLOCK * max(d for _, d in DIL_GROUPS)
DIL_BLOCKS_IN_FLIGHT = 16


def _params(n_axes, flags=None):
    return pltpu.CompilerParams(dimension_semantics=("arbitrary",) * n_axes,
                                vmem_limit_bytes=VMEM_LIMIT, flags=flags)


def _resident(shape):
    zeros = (0,) * len(shape)
    return pl.BlockSpec(shape, lambda *_: zeros, pipeline_mode=pl.Buffered(1))


def _of_layer(stacked, l):
    tail = stacked.shape[1:]
    index = (l,) + (0,) * len(tail)
    return pl.BlockSpec((None,) + tail, lambda *_: index, pipeline_mode=pl.Buffered(1))


def _mod_vector(mod, l, j):
    return pl.BlockSpec((None, None, None) + mod.shape[3:], lambda b, i: (l, b, j, 0, 0))


def _alibi_slopes():
    h = np.arange(1, N_HEADS_A + 1, dtype=np.float32)
    return np.asarray(2.0 ** (-ALIBI_MAX * h / N_HEADS_A), dtype=np.float32)


def _modulated_norm(x, g, sc, sh):
    y = x * lax.rsqrt(jnp.mean(x * x, axis=-1, keepdims=True) + EPS)
    return (y * g) * (1.0 + sc) + sh


def _dot(a, b):
    return jnp.dot(a, b, preferred_element_type=F32)


def _dot_nt(a, b):
    return lax.dot_general(a, b, (((1,), (1,)), ((), ())), preferred_element_type=F32)


def _split_bf16(x):
    pieces = []
    for _ in range(DECAY_PIECES):
        piece = x.astype(BF16)
        pieces.append(piece)
        x = x - piece.astype(F32)
    return pieces


def _mod_kernel(c_ref, w_ref, b_ref, o_ref):
    c = c_ref[...]
    c_act = (c * jax.nn.sigmoid(c)).astype(BF16)
    o_ref[0] = _dot(c_act, w_ref[0].astype(BF16)) + b_ref[0]


def _modulation(c, w_ada, b_ada):
    depth, d, n = w_ada.shape
    batch = c.shape[0]
    cols = 2 * d
    return pl.pallas_call(
        _mod_kernel,
        grid=(depth, n // cols),
        in_specs=[pl.BlockSpec((batch, d), lambda l, j: (0, 0)),
                  pl.BlockSpec((1, d, cols), lambda l, j: (l, 0, j)),
                  pl.BlockSpec((1, 1, cols), lambda l, j: (l, 0, j))],
        out_specs=pl.BlockSpec((1, batch, cols), lambda l, j: (l, 0, j)),
        out_shape=jax.ShapeDtypeStruct((depth, batch, n), F32),
        compiler_params=_params(2),
        name="adaln_modulation",
    )(c, w_ada, b_ada.reshape(depth, 1, n))


def _inproj_kernel(x_ref, sc_ref, sh_ref, g_ref, wa_ref, wqt_ref, wvt_ref, wkf_ref,
                   za0_ref, za1_ref, za2_ref, qt_ref, k_ref, vt_ref, fz_ref, stage_ref):
    h = _modulated_norm(x_ref[0], g_ref[...], sc_ref[...], sh_ref[...]).astype(BF16)
    tm = h.shape[0]
    wg = 3 * WIDTH_G
    for g, za_ref in enumerate((za0_ref, za1_ref, za2_ref)):
        dilation = DIL_GROUPS[g][1]
        za = _dot(h, wa_ref[:, g * wg:(g + 1) * wg])
        if dilation == 1:
            za_ref[0, 0] = za.astype(BF16)
            continue
        for c in range(wg // LANES):
            stage_ref[c] = za[:, c * LANES:(c + 1) * LANES]
        for r in range(dilation):
            rows = pl.ds(r, tm // dilation, stride=dilation)
            for c in range(wg // LANES):
                za_ref[0, r, :, c * LANES:(c + 1) * LANES] = stage_ref[c, rows, :].astype(BF16)
    qt = _dot_nt(wqt_ref[...], h).astype(BF16)
    vt = _dot_nt(wvt_ref[...], h).astype(BF16)
    q_tail = (lax.broadcasted_iota(jnp.int32, (Q_ROWS - HEAD_DIM, tm), 0)
              < DECAY_PIECES).astype(BF16)
    v_tail = (lax.broadcasted_iota(jnp.int32, (V_ROWS - HEAD_DIM, tm), 0) == 0).astype(BF16)
    for hd in range(N_HEADS_B):
        heads = slice(hd * HEAD_DIM, (hd + 1) * HEAD_DIM)
        qt_ref[0, hd * Q_ROWS:hd * Q_ROWS + HEAD_DIM, :] = qt[heads]
        qt_ref[0, hd * Q_ROWS + HEAD_DIM:(hd + 1) * Q_ROWS, :] = q_tail
        vt_ref[0, hd * V_ROWS:hd * V_ROWS + HEAD_DIM, :] = vt[heads]
        vt_ref[0, hd * V_ROWS + HEAD_DIM:(hd + 1) * V_ROWS, :] = v_tail
    kf = _dot(h, wkf_ref[...])
    k_ref[0] = kf[:, :WIDTH_B].astype(BF16)
    fz_ref[0] = kf[:, WIDTH_B:]


def _in_projection(x, mod, params, l):
    batch, seq, d = x.shape
    wg = 3 * WIDTH_G
    tok = lambda w: pl.BlockSpec((1, TM, w), lambda b, i: (b, i, 0))
    tok_t = lambda rows: pl.BlockSpec((1, rows, TM), lambda b, i: (b, 0, i))
    dils = [dilation for _, dilation in DIL_GROUPS]
    assert all(TM % (dilation * BF16_ROWS) == 0 for dilation in dils)
    sub_major = [pl.BlockSpec((1, dilation, TM // dilation, wg), lambda b, i: (b, 0, i, 0))
                 for dilation in dils]
    return pl.pallas_call(
        _inproj_kernel,
        grid=(batch, seq // TM),
        in_specs=[tok(d), _mod_vector(mod, l, MOD_SCALE1), _mod_vector(mod, l, MOD_SHIFT1)]
        + [_of_layer(p, l) for p in params],
        out_specs=sub_major + [tok_t(N_HEADS_B * Q_ROWS), tok(WIDTH_B),
                               tok_t(N_HEADS_B * V_ROWS), tok(F_COLS)],
        out_shape=[jax.ShapeDtypeStruct((batch, dilation, seq // dilation, wg), BF16)
                   for dilation in dils]
        + [jax.ShapeDtypeStruct((batch, N_HEADS_B * Q_ROWS, seq), BF16),
           jax.ShapeDtypeStruct((batch, seq, WIDTH_B), BF16),
           jax.ShapeDtypeStruct((batch, N_HEADS_B * V_ROWS, seq), BF16),
           jax.ShapeDtypeStruct((batch, seq, F_COLS), F32)],
        scratch_shapes=[pltpu.VMEM((wg // LANES, TM, LANES), F32)],
        compiler_params=_params(2),
        name="in_projection",
    )(x, mod, mod, *params)


def _band_bias():
    qi = np.arange(BLOCK)[:, None]
    kj = np.arange(2 * BLOCK)[None, :]
    dist = qi + BLOCK - kj
    in_band = (dist >= 0) & (dist <= BLOCK)
    slopes = _alibi_slopes()
    bias = np.empty((N_HEADS_A, BLOCK, 2 * BLOCK), np.float32)
    for g, (window, dilation) in enumerate(DIL_GROUPS):
        assert window // dilation == BLOCK
        for h in range(HEADS_PER_GROUP):
            head = g * HEADS_PER_GROUP + h
            alibi = -slopes[head] * (dist * dilation).astype(np.float32)
            bias[head] = np.where(in_band, alibi * np.float32(LOG2E), np.float32(NEG))
    return bias.reshape(N_HEADS_A // 2, 2 * BLOCK, 2 * BLOCK)


def _unroll(trips, blocks_per_trip=1):
    fits = [u for u in range(1, trips + 1)
            if trips % u == 0 and u * blocks_per_trip <= DIL_BLOCKS_IN_FLIGHT]
    return max(fits, default=1)


def _dilated_kernel(c0_ref, p0_ref, c1_ref, p1_ref, c2_ref, p2_ref, bias_ref, o_ref,
                    state_ref, gather_ref, stage_ref, first_bias_ref):
    span = pl.program_id(1)
    pairs = HEADS_PER_GROUP // 2
    kj = lax.broadcasted_iota(jnp.int32, (2 * BLOCK, 2 * BLOCK), 1)
    no_prev = jnp.where(kj >= jnp.where(span == 0, BLOCK, 0), 0.0, NEG)
    for slab in range(N_GROUPS * pairs):
        first_bias_ref[slab] = bias_ref[slab] + no_prev
    low_half = lax.broadcasted_iota(jnp.int32, (BLOCK, LANES), 1) < HEAD_DIM
    own_half = jnp.concatenate([low_half, jnp.logical_not(low_half)], axis=0)

    def attend(g, q_of, k_of, v_of, rows, first):
        for pair in range(pairs):
            q = q_of(pair)
            q = jnp.where(own_half, jnp.concatenate([q, q], axis=0), jnp.zeros((), q.dtype))
            bias = (first_bias_ref if first else bias_ref)[g * pairs + pair]
            z = _dot_nt(q, k_of(pair)) + bias
            m = jnp.max(z, axis=-1, keepdims=True)
            p = jnp.exp2(z - m)
            l = jnp.sum(p, axis=-1, keepdims=True)
            out = _dot(p.astype(BF16), v_of(pair)) / l
            lse = m + jnp.log2(l)
            state_ref[g, pair, 0, rows, :] = jnp.where(low_half, out[:BLOCK], out[BLOCK:])
            state_ref[g, pair, 1, rows, :] = jnp.where(low_half, lse[:BLOCK], lse[BLOCK:])

    for g, (cur_ref, prev_ref) in enumerate(((c0_ref, p0_ref), (c1_ref, p1_ref),
                                             (c2_ref, p2_ref))):
        dilation = DIL_GROUPS[g][1]
        class_rows = cur_ref.shape[2]
        nblk = class_rows // BLOCK

        def cols(part, pair):
            return slice(part * WIDTH_G + pair * LANES, part * WIDTH_G + (pair + 1) * LANES)

        def residue(r, carry, g=g, cur_ref=cur_ref, prev_ref=prev_ref, nblk=nblk,
                    cols=cols, class_rows=class_rows):
            def state_rows(n):
                return pl.ds(pl.multiple_of(r * class_rows + n * BLOCK, BLOCK), BLOCK)

            def with_prev(part):
                return lambda pair: jnp.concatenate(
                    [prev_ref[0, r, :, cols(part, pair)],
                     cur_ref[0, r, :BLOCK, cols(part, pair)]], axis=0)

            attend(g, lambda pair: cur_ref[0, r, :BLOCK, cols(0, pair)], with_prev(1),
                   with_prev(2), state_rows(0), first=True)

            def block(n, c):
                row0 = pl.multiple_of(n * BLOCK, BLOCK)
                keys = pl.ds(row0 - BLOCK, 2 * BLOCK)
                attend(g, lambda pair: cur_ref[0, r, pl.ds(row0, BLOCK), cols(0, pair)],
                       lambda pair: cur_ref[0, r, keys, cols(1, pair)],
                       lambda pair: cur_ref[0, r, keys, cols(2, pair)],
                       state_rows(n), first=False)
                return c

            if nblk > 1:
                lax.fori_loop(1, nblk, block, 0, unroll=_unroll(nblk - 1))
            return carry

        if dilation == 1:
            residue(0, 0)
        else:
            lax.fori_loop(0, dilation, residue, 0, unroll=_unroll(dilation, nblk))

    mid = DIL_GROUPS[1][1]
    assert [d for _, d in DIL_GROUPS] == [1, mid, mid * mid] and BLOCK % mid == 0
    mid_rows = DIL_SPAN // mid
    wide_rows = DIL_SPAN // (mid * mid)
    run = BLOCK // mid

    def merge(idx, carry):
        r = idx // (mid_rows // BLOCK)
        n = idx % (mid_rows // BLOCK)
        tokens = pl.ds(mid * BLOCK * n + r, BLOCK, stride=mid)
        mid_tokens = pl.ds(pl.multiple_of(r * mid_rows + n * BLOCK, BLOCK), BLOCK)
        for pair in range(pairs):
            def tiles(kind):
                for a in range(mid):
                    wide = pl.ds(pl.multiple_of((mid * a + r) * wide_rows + run * n, run), run)
                    gather_ref[pair, kind, pl.ds(a, run, stride=mid), :] = (
                        state_ref[2, pair, kind, wide, :])
                return (state_ref[0, pair, kind, tokens, :],
                        state_ref[1, pair, kind, mid_tokens, :], gather_ref[pair, kind])

            lses = tiles(1)
            top = jnp.maximum(jnp.maximum(lses[0], lses[1]), lses[2])
            weights = [jnp.exp2(lse - top) for lse in lses]
            num = sum(w * out for w, out in zip(weights, tiles(0)))
            stage_ref[pair, tokens, :] = num / sum(weights)
        return carry

    lax.fori_loop(0, DIL_SPAN // BLOCK, merge, 0)
    for pair in range(HEADS_PER_GROUP // 2):
        o_ref[0, :, pair * LANES:(pair + 1) * LANES] = stage_ref[pair].astype(o_ref.dtype)


def _dilated_mixture(zs):
    batch, _, seq, wg = zs[0].shape
    assert seq % DIL_SPAN == 0
    in_specs, operands = [], []
    for g, (_, dilation) in enumerate(DIL_GROUPS):
        rows = DIL_SPAN // dilation
        assert rows % BLOCK == 0 and zs[g].shape == (batch, dilation, seq // dilation, wg)
        nblk = rows // BLOCK
        in_specs.append(pl.BlockSpec((1, dilation, rows, wg), lambda b, t: (b, 0, t, 0)))
        in_specs.append(pl.BlockSpec(
            (1, dilation, BLOCK, wg),
            lambda b, t, nblk=nblk: (b, 0, jnp.maximum(t * nblk - 1, 0), 0)))
        operands += [zs[g], zs[g]]
    bias = jnp.asarray(_band_bias())
    return pl.pallas_call(
        _dilated_kernel,
        grid=(batch, seq // DIL_SPAN),
        in_specs=in_specs + [_resident(bias.shape)],
        out_specs=pl.BlockSpec((1, DIL_SPAN, WIDTH_G), lambda b, t: (b, t, 0)),
        out_shape=jax.ShapeDtypeStruct((batch, seq, WIDTH_G), BF16),
        scratch_shapes=[pltpu.VMEM((N_GROUPS, HEADS_PER_GROUP // 2, 2, DIL_SPAN, LANES), F32),
                        pltpu.VMEM((HEADS_PER_GROUP // 2, 2, BLOCK, LANES), F32),
                        pltpu.VMEM((HEADS_PER_GROUP // 2, DIL_SPAN, LANES), F32),
                        pltpu.VMEM(bias.shape, F32)],
        compiler_params=_params(2),
        name="dilated_attention",
    )(*operands, bias)


def _log_sigmoid(x):
    return jnp.minimum(x, 0.0) - jnp.log1p(jnp.exp(-jnp.abs(x)))


def _fox_place():
    place = np.zeros((WIDTH_B + DECAY_PIECES * F_COLS, N_HEADS_B * LANES), np.float32)
    for c in range(WIDTH_B):
        place[c, (c // HEAD_DIM) * LANES + c % HEAD_DIM] = 1.0
    for i in range(DECAY_PIECES):
        for h in range(N_HEADS_B):
            place[WIDTH_B + i * F_COLS + h, h * LANES + HEAD_DIM + i] = 1.0
    return place


def _fox_kernel(qt_ref, k_ref, vt_ref, fz_ref, b_ref, tri_ref, place_ref, o_ref,
                kaug_ref, acc_ref, s_ref):
    tq = qt_ref.shape[2]
    seq = k_ref.shape[1]
    qb = pl.program_id(1)
    q0 = qb * tq
    n_full = qb
    key_minus_query = (lax.broadcasted_iota(jnp.int32, (FOX_TKV, tq), 0)
                       - lax.broadcasted_iota(jnp.int32, (FOX_TKV, tq), 1))

    @pl.when(qb == 0)
    def _():
        tri = tri_ref[...]
        carry = jnp.zeros((1, F_COLS), F32)
        for c in range(seq // FOX_TKV):
            rows = slice(c * FOX_TKV, (c + 1) * FOX_TKV)
            sums = []
            for part in range(FOX_TKV // DECAY_CHUNK):
                at = c * FOX_TKV + part * DECAY_CHUNK
                lf = _log_sigmoid(fz_ref[0, at:at + DECAY_CHUNK, :] + b_ref[...])
                sums.append(sum(_dot(tri, piece) for piece in _split_bf16(lf)) + carry)
                carry = sums[-1][DECAY_CHUNK - 1:, :]
            pieces = _split_bf16(jnp.concatenate(sums, axis=0) * (-LOG2E))
            packed = jnp.concatenate([k_ref[0, rows, :]] + pieces, axis=1)
            kaug_ref[rows, :] = _dot(packed, place_ref[...]).astype(BF16)

    acc_ref[...] = jnp.zeros_like(acc_ref)

    def scores(h, k0, masked):
        keys = pl.ds(pl.multiple_of(k0, FOX_TKV), FOX_TKV)
        s = _dot(kaug_ref[keys, h * LANES:(h + 1) * LANES],
                 qt_ref[0, h * Q_ROWS:(h + 1) * Q_ROWS, :])
        if masked:
            s = jnp.where(key_minus_query <= q0 - k0, s, NEG)
        s_ref[h % 2] = s
        return jnp.max(s, axis=0, keepdims=True)

    def kv_block(j, carry, diagonal):
        ms, top_next = carry
        k0 = pl.multiple_of(j * FOX_TKV, FOX_TKV)
        new_ms = []
        for h in range(N_HEADS_B):
            m_new = jnp.maximum(ms[h], top_next)
            if h + 1 < N_HEADS_B:
                top_next = scores(h + 1, k0, masked=diagonal)
            elif not diagonal:
                top_next = scores(0, k0 + FOX_TKV, masked=True)
            alpha = jnp.exp2(ms[h] - m_new)
            p = jnp.exp2(s_ref[h % 2] - m_new).astype(BF16)
            vt = vt_ref[0, h * V_ROWS:(h + 1) * V_ROWS, pl.ds(k0, FOX_TKV)]
            acc_ref[h] = acc_ref[h] * alpha + _dot(vt, p)
            new_ms.append(m_new)
        return tuple(new_ms), top_next

    ms = tuple(jnp.full((1, tq), NEG, F32) for _ in range(N_HEADS_B))
    carry = (ms, scores(0, 0, masked=True))
    carry = lax.fori_loop(0, n_full, functools.partial(kv_block, diagonal=False), carry)
    kv_block(n_full, carry, diagonal=True)
    pad = jnp.zeros((LANES - HEAD_DIM, tq), F32)
    for h in range(N_HEADS_B):
        acc = acc_ref[h]
        out_t = jnp.concatenate([acc[:HEAD_DIM] / acc[HEAD_DIM:HEAD_DIM + 1], pad], axis=0)
        o_ref[0, :, h * HEAD_DIM:(h + 1) * HEAD_DIM] = (
            out_t.T[:, :HEAD_DIM].astype(o_ref.dtype))


def _forgetting_attention(qt, k, vt, fz, f_bias, l):
    batch, seq, _ = k.shape
    assert FOX_TKV % DECAY_CHUNK == 0
    place = jnp.asarray(_fox_place(), dtype=BF16)
    idx = np.arange(DECAY_CHUNK)
    tri = jnp.asarray(idx[None, :] <= idx[:, None], dtype=BF16)
    return pl.pallas_call(
        _fox_kernel,
        grid=(batch, seq // FOX_TQ),
        in_specs=[pl.BlockSpec((1, N_HEADS_B * Q_ROWS, FOX_TQ), lambda b, i: (b, 0, i)),
                  pl.BlockSpec((1, seq, WIDTH_B), lambda b, i: (b, 0, 0)),
                  pl.BlockSpec((1, N_HEADS_B * V_ROWS, seq), lambda b, i: (b, 0, 0)),
                  pl.BlockSpec((1, seq, F_COLS), lambda b, i: (b, 0, 0)),
                  _of_layer(f_bias, l), _resident(tri.shape), _resident(place.shape)],
        out_specs=pl.BlockSpec((1, FOX_TQ, WIDTH_B), lambda b, i: (b, i, 0)),
        out_shape=jax.ShapeDtypeStruct((batch, seq, WIDTH_B), BF16),
        scratch_shapes=[pltpu.VMEM((seq, N_HEADS_B * LANES), BF16),
                        pltpu.VMEM((N_HEADS_B, V_ROWS, FOX_TQ), F32),
                        pltpu.VMEM((2, FOX_TKV, FOX_TQ), F32)],
        compiler_params=_params(2),
        name="forgetting_attention",
    )(qt, k, vt, fz, f_bias, tri, place)


def _outproj_kernel(x_ref, sc_ref, sh_ref, gate_ref, ya_ref, yb_ref,
                    g_ref, wg_ref, wua_ref, wub_ref, wo_ref, o_ref, merged_ref):
    x = x_ref[0]
    d = x.shape[-1]
    h = _modulated_norm(x, g_ref[...], sc_ref[...], sh_ref[...]).astype(BF16)
    ya = ya_ref[0]
    yb = yb_ref[0]
    for c in range(d // OUT_CHUNK):
        cs = slice(c * OUT_CHUNK, (c + 1) * OUT_CHUNK)
        cs_b = slice(d + c * OUT_CHUNK, d + (c + 1) * OUT_CHUNK)
        gate_a = jax.nn.sigmoid(_dot(h, wg_ref[:, cs]))
        gate_b = jax.nn.sigmoid(_dot(h, wg_ref[:, cs_b]))
        merged = gate_a * _dot(ya, wua_ref[:, cs]) + gate_b * _dot(yb, wub_ref[:, cs])
        merged_ref[:, cs] = merged.astype(BF16)
    o_ref[0] = x + gate_ref[...] * _dot(merged_ref[...], wo_ref[...])


def _out_projection(x, mod, ya, yb, params, l):
    batch, seq, d = x.shape
    tok = lambda w: pl.BlockSpec((1, TM, w), lambda b, i: (b, i, 0))
    return pl.pallas_call(
        _outproj_kernel,
        grid=(batch, seq // TM),
        in_specs=[tok(d), _mod_vector(mod, l, MOD_SCALE1), _mod_vector(mod, l, MOD_SHIFT1),
                  _mod_vector(mod, l, MOD_GATE1), tok(WIDTH_G), tok(WIDTH_B)]
        + [_of_layer(p, l) for p in params],
        out_specs=tok(d),
        out_shape=jax.ShapeDtypeStruct((batch, seq, d), F32),
        scratch_shapes=[pltpu.VMEM((TM, d), BF16)],
        compiler_params=_params(2),
        name="out_projection",
    )(x, mod, mod, mod, ya, yb, *params)


def _ffn_kernel(*refs, final_norm):
    x_ref, sc_ref, sh_ref, gate_ref, g_ref, win_ref, wout_ref = refs[:7]
    refs = refs[7:]
    if final_norm:
        gf_ref = refs[0]
        refs = refs[1:]
    o_ref, h_ref, acc_ref = refs
    x = x_ref[0]
    h_ref[...] = _modulated_norm(x, g_ref[...], sc_ref[...], sh_ref[...]).astype(BF16)
    for c in range(D_FF // FF_CHUNK):
        cs = slice(c * FF_CHUNK, (c + 1) * FF_CHUNK)
        cs_up = slice(D_FF + c * FF_CHUNK, D_FF + (c + 1) * FF_CHUNK)
        h = h_ref[...]
        gt = _dot(h, win_ref[:, cs])
        up = _dot(h, win_ref[:, cs_up])
        act = ((gt * jax.nn.sigmoid(gt)) * up).astype(BF16)
        part = _dot(act, wout_ref[cs, :])
        if c == 0:
            acc_ref[...] = part
        else:
            acc_ref[...] += part
    y = x + gate_ref[...] * acc_ref[...]
    if final_norm:
        y = (y * lax.rsqrt(jnp.mean(y * y, axis=-1, keepdims=True) + EPS)) * gf_ref[...]
    o_ref[0] = y


def _ffn(x, mod, params, l, g_final):
    batch, seq, d = x.shape
    final_norm = g_final is not None
    tok = pl.BlockSpec((1, TM, d), lambda b, i: (b, i, 0))
    in_specs = [tok, _mod_vector(mod, l, MOD_SCALE2), _mod_vector(mod, l, MOD_SHIFT2),
                _mod_vector(mod, l, MOD_GATE2)] + [_of_layer(p, l) for p in params]
    operands = [x, mod, mod, mod, *params]
    if final_norm:
        in_specs.append(_resident((1, d)))
        operands.append(g_final)
    return pl.pallas_call(
        functools.partial(_ffn_kernel, final_norm=final_norm),
        grid=(batch, seq // TM),
        in_specs=in_specs,
        out_specs=tok,
        out_shape=jax.ShapeDtypeStruct((batch, seq, d), F32),
        scratch_shapes=[pltpu.VMEM((TM, d), BF16), pltpu.VMEM((TM, d), F32)],
        compiler_params=_params(2),
        name="swiglu_ffn",
    )(*operands)


def _win_kernel(w_ref, wa_ref, wqt_ref, wvt_ref, wkf_ref, wg_ref):
    w = w_ref[0]
    scale = QK_SCALE * LOG2E
    for g in range(N_GROUPS):
        for part in range(3):
            src = part * WIDTH_A + g * WIDTH_G
            piece = w[:, src:src + WIDTH_G]
            dst = (3 * g + part) * WIDTH_G
            wa_ref[0, :, dst:dst + WIDTH_G] = (piece * scale if part == 0 else piece).astype(BF16)
    qb0 = 3 * WIDTH_A
    wqt_ref[0] = (w[:, qb0:qb0 + WIDTH_B] * scale).T.astype(BF16)
    wvt_ref[0] = w[:, qb0 + 2 * WIDTH_B:qb0 + 3 * WIDTH_B].T.astype(BF16)
    f0 = qb0 + 3 * WIDTH_B
    forget = w[:, f0:f0 + F_COLS]
    lane = lax.broadcasted_iota(jnp.int32, forget.shape, 1)
    wkf_ref[0, :, :WIDTH_B] = w[:, qb0 + WIDTH_B:qb0 + 2 * WIDTH_B].astype(BF16)
    wkf_ref[0, :, WIDTH_B:] = jnp.where(lane < N_HEADS_B, forget, 0.0).astype(BF16)
    g0 = f0 + N_HEADS_B
    wg_ref[0] = w[:, g0:g0 + 2 * D_MODEL].astype(BF16)


def _split_w_in(w_in):
    depth, d, d_in = w_in.shape
    assert d_in == 3 * WIDTH_A + 3 * WIDTH_B + N_HEADS_B + 2 * D_MODEL and d % W_ROWS == 0
    row_block = lambda cols: pl.BlockSpec((1, W_ROWS, cols), lambda l, i: (l, i, 0))
    col_block = pl.BlockSpec((1, WIDTH_B, W_ROWS), lambda l, i: (l, 0, i))
    widths = (3 * WIDTH_A, WIDTH_B + F_COLS, 2 * D_MODEL)
    stack = lambda *tail: jax.ShapeDtypeStruct((depth,) + tail, BF16)
    wa, wqt, wvt, wkf, wg = pl.pallas_call(
        _win_kernel,
        grid=(depth, d // W_ROWS),
        in_specs=[row_block(d_in)],
        out_specs=[row_block(widths[0]), col_block, col_block, row_block(widths[1]),
                   row_block(widths[2])],
        out_shape=[stack(d, widths[0]), stack(WIDTH_B, d), stack(WIDTH_B, d),
                   stack(d, widths[1]), stack(d, widths[2])],
        compiler_params=_params(2),
        name="in_projection_weight_layout",
    )(w_in)
    return wa, wqt, wvt, wkf, wg


def kernel(x, c, w_ada, b_ada, norm_mix, w_in, b_forget, w_up_a, w_up_b, w_out,
           norm_ffn, w_ffn_in, w_ffn_out, norm_final):
    depth = w_ada.shape[0]
    batch, seq, d = x.shape
    assert seq % TM == 0 and seq % FOX_TQ == 0 and FOX_TQ == FOX_TKV
    mod = _modulation(c, w_ada, b_ada).reshape(depth, batch, N_MOD, 1, d)
    wa, wqt, wvt, wkf, wg = _split_w_in(w_in)
    in_params = (norm_mix.reshape(depth, 1, d), wa, wqt, wvt, wkf)
    out_params = (norm_mix.reshape(depth, 1, d), wg, w_up_a.astype(BF16), w_up_b.astype(BF16),
                  w_out.astype(BF16))
    ffn_params = (norm_ffn.reshape(depth, 1, d), w_ffn_in.astype(BF16), w_ffn_out.astype(BF16))
    f_bias = jnp.pad(b_forget.astype(F32), ((0, 0), (0, F_COLS - N_HEADS_B)))[:, None, :]
    for l in range(depth):
        za0, za1, za2, qt, k, vt, fz = _in_projection(x, mod, in_params, l)
        ya = _dilated_mixture((za0, za1, za2))
        yb = _forgetting_attention(qt, k, vt, fz, f_bias, l)
        x = _out_projection(x, mod, ya, yb, out_params, l)
        g_final = norm_final.reshape(1, d) if l == depth - 1 else None
        x = _ffn(x, mod, ffn_params, l, g_final)
    return x
```

```python
import functools

import numpy as np
import jax
import jax.numpy as jnp
from jax import lax
from jax.experimental import pallas as pl
from jax.experimental.pallas import tpu as pltpu

F32 = jnp.float32
BF16 = jnp.bfloat16

D_MODEL = 1024
HEAD_DIM = 64
DIL_GROUPS = ((128, 1), (512, 4), (2048, 16))
HEADS_PER_GROUP = 4
N_GROUPS = len(DIL_GROUPS)
N_HEADS_A = HEADS_PER_GROUP * N_GROUPS
N_HEADS_B = 4
WIDTH_G = HEADS_PER_GROUP * HEAD_DIM
WIDTH_A = N_HEADS_A * HEAD_DIM
WIDTH_B = N_HEADS_B * HEAD_DIM
D_FF = -(-8 * D_MODEL // (3 * 256)) * 256
BLOCK = 128
ALIBI_MAX = 8.0
EPS = 1e-6
N_MOD = 6
MOD_SHIFT1, MOD_SCALE1, MOD_GATE1, MOD_SHIFT2, MOD_SCALE2, MOD_GATE2 = range(N_MOD)
QK_SCALE = HEAD_DIM ** -0.5

NEG = -1e30
LANES = 128
F_COLS = LANES
BF16_ROWS = 16
V_ROWS = HEAD_DIM + BF16_ROWS
Q_ROWS = LANES
DECAY_PIECES = 3
LOG2E = 1.4426950408889634

VMEM_LIMIT = 56 * 1024 * 1024

TM = 1024
W_ROWS = 256
FF_CHUNK = 256
OUT_CHUNK = 256
DECAY_CHUNK = 256
FOX_TQ = 512
FOX_TKV = 512
DIL_SPAN = BLOCK * max(d for _, d in DIL_GROUPS)
DIL_BLOCKS_IN_FLIGHT = 16


def _params(n_axes, flags=None):
    return pltpu.CompilerParams(dimension_semantics=("arbitrary",) * n_axes,
                                vmem_limit_bytes=VMEM_LIMIT, flags=flags)


def _resident(shape):
    zeros = (0,) * len(shape)
    return pl.BlockSpec(shape, lambda *_: zeros, pipeline_mode=pl.Buffered(1))


def _of_layer(stacked, l):
    tail = stacked.shape[1:]
    index = (l,) + (0,) * len(tail)
    return pl.BlockSpec((None,) + tail, lambda *_: index, pipeline_mode=pl.Buffered(1))


def _mod_vector(mod, l, j):
    return pl.BlockSpec((None, None, None) + mod.shape[3:], lambda b, i: (l, b, j, 0, 0))


def _alibi_slopes():
    h = np.arange(1, N_HEADS_A + 1, dtype=np.float32)
    return np.asarray(2.0 ** (-ALIBI_MAX * h / N_HEADS_A), dtype=np.float32)


def _modulated_norm(x, g, sc, sh):
    y = x * lax.rsqrt(jnp.mean(x * x, axis=-1, keepdims=True) + EPS)
    return (y * g) * (1.0 + sc) + sh


def _dot(a, b):
    return jnp.dot(a, b, preferred_element_type=F32)


def _dot_nt(a, b):
    return lax.dot_general(a, b, (((1,), (1,)), ((), ())), preferred_element_type=F32)


def _split_bf16(x):
    pieces = []
    for _ in range(DECAY_PIECES):
        piece = x.astype(BF16)
        pieces.append(piece)
        x = x - piece.astype(F32)
    return pieces


def _mod_kernel(c_ref, w_ref, b_ref, o_ref):
    c = c_ref[...]
    c_act = (c * jax.nn.sigmoid(c)).astype(BF16)
    o_ref[0] = _dot(c_act, w_ref[0].astype(BF16)) + b_ref[0]


def _modulation(c, w_ada, b_ada):
    depth, d, n = w_ada.shape
    batch = c.shape[0]
    cols = 2 * d
    return pl.pallas_call(
        _mod_kernel,
        grid=(depth, n // cols),
        in_specs=[pl.BlockSpec((batch, d), lambda l, j: (0, 0)),
                  pl.BlockSpec((1, d, cols), lambda l, j: (l, 0, j)),
                  pl.BlockSpec((1, 1, cols), lambda l, j: (l, 0, j))],
        out_specs=pl.BlockSpec((1, batch, cols), lambda l, j: (l, 0, j)),
        out_shape=jax.ShapeDtypeStruct((depth, batch, n), F32),
        compiler_params=_params(2),
        name="adaln_modulation",
    )(c, w_ada, b_ada.reshape(depth, 1, n))


def _inproj_kernel(x_ref, sc_ref, sh_ref, g_ref, wa_ref, wqt_ref, wvt_ref, wkf_ref,
                   za0_ref, za1_ref, za2_ref, qt_ref, k_ref, vt_ref, fz_ref, stage_ref):
    h = _modulated_norm(x_ref[0], g_ref[...], sc_ref[...], sh_ref[...]).astype(BF16)
    tm = h.shape[0]
    wg = 3 * WIDTH_G
    for g, za_ref in enumerate((za0_ref, za1_ref, za2_ref)):
        dilation = DIL_GROUPS[g][1]
        za = _dot(h, wa_ref[:, g * wg:(g + 1) * wg])
        if dilation == 1:
            za_ref[0, 0] = za.astype(BF16)
            continue
        for c in range(wg // LANES):
            stage_ref[c] = za[:, c * LANES:(c + 1) * LANES]
        for r in range(dilation):
            rows = pl.ds(r, tm // dilation, stride=dilation)
            for c in range(wg // LANES):
                za_ref[0, r, :, c * LANES:(c + 1) * LANES] = stage_ref[c, rows, :].astype(BF16)
    qt = _dot_nt(wqt_ref[...], h).astype(BF16)
    vt = _dot_nt(wvt_ref[...], h).astype(BF16)
    q_tail = (lax.broadcasted_iota(jnp.int32, (Q_ROWS - HEAD_DIM, tm), 0)
              < DECAY_PIECES).astype(BF16)
    v_tail = (lax.broadcasted_iota(jnp.int32, (V_ROWS - HEAD_DIM, tm), 0) == 0).astype(BF16)
    for hd in range(N_HEADS_B):
        heads = slice(hd * HEAD_DIM, (hd + 1) * HEAD_DIM)
        qt_ref[0, hd * Q_ROWS:hd * Q_ROWS + HEAD_DIM, :] = qt[heads]
        qt_ref[0, hd * Q_ROWS + HEAD_DIM:(hd + 1) * Q_ROWS, :] = q_tail
        vt_ref[0, hd * V_ROWS:hd * V_ROWS + HEAD_DIM, :] = vt[heads]
        vt_ref[0, hd * V_ROWS + HEAD_DIM:(hd + 1) * V_ROWS, :] = v_tail
    kf = _dot(h, wkf_ref[...])
    k_ref[0] = kf[:, :WIDTH_B].astype(BF16)
    fz_ref[0] = kf[:, WIDTH_B:]


def _in_projection(x, mod, params, l):
    batch, seq, d = x.shape
    wg = 3 * WIDTH_G
    tok = lambda w: pl.BlockSpec((1, TM, w), lambda b, i: (b, i, 0))
    tok_t = lambda rows: pl.BlockSpec((1, rows, TM), lambda b, i: (b, 0, i))
    dils = [dilation for _, dilation in DIL_GROUPS]
    assert all(TM % (dilation * BF16_ROWS) == 0 for dilation in dils)
    sub_major = [pl.BlockSpec((1, dilation, TM // dilation, wg), lambda b, i: (b, 0, i, 0))
                 for dilation in dils]
    return pl.pallas_call(
        _inproj_kernel,
        grid=(batch, seq // TM),
        in_specs=[tok(d), _mod_vector(mod, l, MOD_SCALE1), _mod_vector(mod, l, MOD_SHIFT1)]
        + [_of_layer(p, l) for p in params],
        out_specs=sub_major + [tok_t(N_HEADS_B * Q_ROWS), tok(WIDTH_B),
                               tok_t(N_HEADS_B * V_ROWS), tok(F_COLS)],
        out_shape=[jax.ShapeDtypeStruct((batch, dilation, seq // dilation, wg), BF16)
                   for dilation in dils]
        + [jax.ShapeDtypeStruct((batch, N_HEADS_B * Q_ROWS, seq), BF16),
           jax.ShapeDtypeStruct((batch, seq, WIDTH_B), BF16),
           jax.ShapeDtypeStruct((batch, N_HEADS_B * V_ROWS, seq), BF16),
           jax.ShapeDtypeStruct((batch, seq, F_COLS), F32)],
        scratch_shapes=[pltpu.VMEM((wg // LANES, TM, LANES), F32)],
        compiler_params=_params(2),
        name="in_projection",
    )(x, mod, mod, *params)


def _band_bias():
    qi = np.arange(BLOCK)[:, None]
    kj = np.arange(2 * BLOCK)[None, :]
    dist = qi + BLOCK - kj
    in_band = (dist >= 0) & (dist <= BLOCK)
    slopes = _alibi_slopes()
    bias = np.empty((N_HEADS_A, BLOCK, 2 * BLOCK), np.float32)
    for g, (window, dilation) in enumerate(DIL_GROUPS):
        assert window // dilation == BLOCK
        for h in range(HEADS_PER_GROUP):
            head = g * HEADS_PER_GROUP + h
            alibi = -slopes[head] * (dist * dilation).astype(np.float32)
            bias[head] = np.where(in_band, alibi * np.float32(LOG2E), np.float32(NEG))
    return bias.reshape(N_HEADS_A // 2, 2 * BLOCK, 2 * BLOCK)


def _unroll(trips, blocks_per_trip=1):
    fits = [u for u in range(1, trips + 1)
            if trips % u == 0 and u * blocks_per_trip <= DIL_BLOCKS_IN_FLIGHT]
    return max(fits, default=1)


def _dilated_kernel(c0_ref, p0_ref, c1_ref, p1_ref, c2_ref, p2_ref, bias_ref, o_ref,
                    state_ref, gather_ref, stage_ref, first_bias_ref):
    span = pl.program_id(1)
    pairs = HEADS_PER_GROUP // 2
    kj = lax.broadcasted_iota(jnp.int32, (2 * BLOCK, 2 * BLOCK), 1)
    no_prev = jnp.where(kj >= jnp.where(span == 0, BLOCK, 0), 0.0, NEG)
    for slab in range(N_GROUPS * pairs):
        first_bias_ref[slab] = bias_ref[slab] + no_prev
    low_half = lax.broadcasted_iota(jnp.int32, (BLOCK, LANES), 1) < HEAD_DIM
    own_half = jnp.concatenate([low_half, jnp.logical_not(low_half)], axis=0)

    def attend(g, q_of, k_of, v_of, rows, first):
        for pair in range(pairs):
            q = q_of(pair)
            q = jnp.where(own_half, jnp.concatenate([q, q], axis=0), jnp.zeros((), q.dtype))
            bias = (first_bias_ref if first else bias_ref)[g * pairs + pair]
            z = _dot_nt(q, k_of(pair)) + bias
            m = jnp.max(z, axis=-1, keepdims=True)
            p = jnp.exp2(z - m)
            l = jnp.sum(p, axis=-1, keepdims=True)
            out = _dot(p.astype(BF16), v_of(pair)) / l
            lse = m + jnp.log2(l)
            state_ref[g, pair, 0, rows, :] = jnp.where(low_half, out[:BLOCK], out[BLOCK:])
            state_ref[g, pair, 1, rows, :] = jnp.where(low_half, lse[:BLOCK], lse[BLOCK:])

    for g, (cur_ref, prev_ref) in enumerate(((c0_ref, p0_ref), (c1_ref, p1_ref),
                                             (c2_ref, p2_ref))):
        dilation = DIL_GROUPS[g][1]
        class_rows = cur_ref.shape[2]
        nblk = class_rows // BLOCK

        def cols(part, pair):
            return slice(part * WIDTH_G + pair * LANES, part * WIDTH_G + (pair + 1) * LANES)

        def residue(r, carry, g=g, cur_ref=cur_ref, prev_ref=prev_ref, nblk=nblk,
                    cols=cols, class_rows=class_rows):
            def state_rows(n):
                return pl.ds(pl.multiple_of(r * class_rows + n * BLOCK, BLOCK), BLOCK)

            def with_prev(part):
                return lambda pair: jnp.concatenate(
                    [prev_ref[0, r, :, cols(part, pair)],
                     cur_ref[0, r, :BLOCK, cols(part, pair)]], axis=0)

            attend(g, lambda pair: cur_ref[0, r, :BLOCK, cols(0, pair)], with_prev(1),
                   with_prev(2), state_rows(0), first=True)

            def block(n, c):
                row0 = pl.multiple_of(n * BLOCK, BLOCK)
                keys = pl.ds(row0 - BLOCK, 2 * BLOCK)
                attend(g, lambda pair: cur_ref[0, r, pl.ds(row0, BLOCK), cols(0, pair)],
                       lambda pair: cur_ref[0, r, keys, cols(1, pair)],
                       lambda pair: cur_ref[0, r, keys, cols(2, pair)],
                       state_rows(n), first=False)
                return c

            if nblk > 1:
                lax.fori_loop(1, nblk, block, 0, unroll=_unroll(nblk - 1))
            return carry

        if dilation == 1:
            residue(0, 0)
        else:
            lax.fori_loop(0, dilation, residue, 0, unroll=_unroll(dilation, nblk))

    mid = DIL_GROUPS[1][1]
    assert [d for _, d in DIL_GROUPS] == [1, mid, mid * mid] and BLOCK % mid == 0
    mid_rows = DIL_SPAN // mid
    wide_rows = DIL_SPAN // (mid * mid)
    run = BLOCK // mid

    def merge(idx, carry):
        r = idx // (mid_rows // BLOCK)
        n = idx % (mid_rows // BLOCK)
        tokens = pl.ds(mid * BLOCK * n + r, BLOCK, stride=mid)
        mid_tokens = pl.ds(pl.multiple_of(r * mid_rows + n * BLOCK, BLOCK), BLOCK)
        for pair in range(pairs):
            def tiles(kind):
                for a in range(mid):
                    wide = pl.ds(pl.multiple_of((mid * a + r) * wide_rows + run * n, run), run)
                    gather_ref[pair, kind, pl.ds(a, run, stride=mid), :] = (
                        state_ref[2, pair, kind, wide, :])
                return (state_ref[0, pair, kind, tokens, :],
                        state_ref[1, pair, kind, mid_tokens, :], gather_ref[pair, kind])

            lses = tiles(1)
            top = jnp.maximum(jnp.maximum(lses[0], lses[1]), lses[2])
            weights = [jnp.exp2(lse - top) for lse in lses]
            num = sum(w * out for w, out in zip(weights, tiles(0)))
            stage_ref[pair, tokens, :] = num / sum(weights)
        return carry

    lax.fori_loop(0, DIL_SPAN // BLOCK, merge, 0)
    for pair in range(HEADS_PER_GROUP // 2):
        o_ref[0, :, pair * LANES:(pair + 1) * LANES] = stage_ref[pair].astype(o_ref.dtype)


def _dilated_mixture(zs):
    batch, _, seq, wg = zs[0].shape
    assert seq % DIL_SPAN == 0
    in_specs, operands = [], []
    for g, (_, dilation) in enumerate(DIL_GROUPS):
        rows = DIL_SPAN // dilation
        assert rows % BLOCK == 0 and zs[g].shape == (batch, dilation, seq // dilation, wg)
        nblk = rows // BLOCK
        in_specs.append(pl.BlockSpec((1, dilation, rows, wg), lambda b, t: (b, 0, t, 0)))
        in_specs.append(pl.BlockSpec(
            (1, dilation, BLOCK, wg),
            lambda b, t, nblk=nblk: (b, 0, jnp.maximum(t * nblk - 1, 0), 0)))
        operands += [zs[g], zs[g]]
    bias = jnp.asarray(_band_bias())
    return pl.pallas_call(
        _dilated_kernel,
        grid=(batch, seq // DIL_SPAN),
        in_specs=in_specs + [_resident(bias.shape)],
        out_specs=pl.BlockSpec((1, DIL_SPAN, WIDTH_G), lambda b, t: (b, t, 0)),
        out_shape=jax.ShapeDtypeStruct((batch, seq, WIDTH_G), BF16),
        scratch_shapes=[pltpu.VMEM((N_GROUPS, HEADS_PER_GROUP // 2, 2, DIL_SPAN, LANES), F32),
                        pltpu.VMEM((HEADS_PER_GROUP // 2, 2, BLOCK, LANES), F32),
                        pltpu.VMEM((HEADS_PER_GROUP // 2, DIL_SPAN, LANES), F32),
                        pltpu.VMEM(bias.shape, F32)],
        compiler_params=_params(2),
        name="dilated_attention",
    )(*operands, bias)


def _log_sigmoid(x):
    return jnp.minimum(x, 0.0) - jnp.log1p(jnp.exp(-jnp.abs(x)))


def _fox_place():
    place = np.zeros((WIDTH_B + DECAY_PIECES * F_COLS, N_HEADS_B * LANES), np.float32)
    for c in range(WIDTH_B):
        place[c, (c // HEAD_DIM) * LANES + c % HEAD_DIM] = 1.0
    for i in range(DECAY_PIECES):
        for h in range(N_HEADS_B):
            place[WIDTH_B + i * F_COLS + h, h * LANES + HEAD_DIM + i] = 1.0
    return place


def _fox_kernel(qt_ref, k_ref, vt_ref, fz_ref, b_ref, tri_ref, place_ref, o_ref,
                kaug_ref, acc_ref, s_ref):
    tq = qt_ref.shape[2]
    seq = k_ref.shape[1]
    qb = pl.program_id(1)
    q0 = qb * tq
    n_full = qb
    key_minus_query = (lax.broadcasted_iota(jnp.int32, (FOX_TKV, tq), 0)
                       - lax.broadcasted_iota(jnp.int32, (FOX_TKV, tq), 1))

    @pl.when(qb == 0)
    def _():
        tri = tri_ref[...]
        carry = jnp.zeros((1, F_COLS), F32)
        for c in range(seq // FOX_TKV):
            rows = slice(c * FOX_TKV, (c + 1) * FOX_TKV)
            sums = []
            for part in range(FOX_TKV // DECAY_CHUNK):
                at = c * FOX_TKV + part * DECAY_CHUNK
                lf = _log_sigmoid(fz_ref[0, at:at + DECAY_CHUNK, :] + b_ref[...])
                sums.append(sum(_dot(tri, piece) for piece in _split_bf16(lf)) + carry)
                carry = sums[-1][DECAY_CHUNK - 1:, :]
            pieces = _split_bf16(jnp.concatenate(sums, axis=0) * (-LOG2E))
            packed = jnp.concatenate([k_ref[0, rows, :]] + pieces, axis=1)
            kaug_ref[rows, :] = _dot(packed, place_ref[...]).astype(BF16)

    acc_ref[...] = jnp.zeros_like(acc_ref)

    def scores(h, k0, masked):
        keys = pl.ds(pl.multiple_of(k0, FOX_TKV), FOX_TKV)
        s = _dot(kaug_ref[keys, h * LANES:(h + 1) * LANES],
                 qt_ref[0, h * Q_ROWS:(h + 1) * Q_ROWS, :])
        if masked:
            s = jnp.where(key_minus_query <= q0 - k0, s, NEG)
        s_ref[h % 2] = s
        return jnp.max(s, axis=0, keepdims=True)

    def kv_block(j, carry, diagonal):
        ms, top_next = carry
        k0 = pl.multiple_of(j * FOX_TKV, FOX_TKV)
        new_ms = []
        for h in range(N_HEADS_B):
            m_new = jnp.maximum(ms[h], top_next)
            if h + 1 < N_HEADS_B:
                top_next = scores(h + 1, k0, masked=diagonal)
            elif not diagonal:
                top_next = scores(0, k0 + FOX_TKV, masked=True)
            alpha = jnp.exp2(ms[h] - m_new)
            p = jnp.exp2(s_ref[h % 2] - m_new).astype(BF16)
            vt = vt_ref[0, h * V_ROWS:(h + 1) * V_ROWS, pl.ds(k0, FOX_TKV)]
            acc_ref[h] = acc_ref[h] * alpha + _dot(vt, p)
            new_ms.append(m_new)
        return tuple(new_ms), top_next

    ms = tuple(jnp.full((1, tq), NEG, F32) for _ in range(N_HEADS_B))
    carry = (ms, scores(0, 0, masked=True))
    below = functools.partial(kv_block, diagonal=False)
    carry = lax.fori_loop(0, n_full // 2, lambda i, c: below(2 * i + 1, below(2 * i, c)), carry)
    carry = lax.cond(n_full % 2 == 1, lambda c: below(n_full - 1, c), lambda c: c, carry)
    kv_block(n_full, carry, diagonal=True)
    outs = []
    for h in range(N_HEADS_B):
        acc = acc_ref[h]
        outs.append(acc[:HEAD_DIM] / acc[HEAD_DIM:HEAD_DIM + 1])
    o_ref[0] = jnp.concatenate(outs, axis=0).T.astype(o_ref.dtype)


def _forgetting_attention(qt, k, vt, fz, f_bias, l):
    batch, seq, _ = k.shape
    assert FOX_TKV % DECAY_CHUNK == 0
    place = jnp.asarray(_fox_place(), dtype=BF16)
    idx = np.arange(DECAY_CHUNK)
    tri = jnp.asarray(idx[None, :] <= idx[:, None], dtype=BF16)
    return pl.pallas_call(
        _fox_kernel,
        grid=(batch, seq // FOX_TQ),
        in_specs=[pl.BlockSpec((1, N_HEADS_B * Q_ROWS, FOX_TQ), lambda b, i: (b, 0, i)),
                  pl.BlockSpec((1, seq, WIDTH_B), lambda b, i: (b, 0, 0)),
                  pl.BlockSpec((1, N_HEADS_B * V_ROWS, seq), lambda b, i: (b, 0, 0)),
                  pl.BlockSpec((1, seq, F_COLS), lambda b, i: (b, 0, 0)),
                  _of_layer(f_bias, l), _resident(tri.shape), _resident(place.shape)],
        out_specs=pl.BlockSpec((1, FOX_TQ, WIDTH_B), lambda b, i: (b, i, 0)),
        out_shape=jax.ShapeDtypeStruct((batch, seq, WIDTH_B), BF16),
        scratch_shapes=[pltpu.VMEM((seq, N_HEADS_B * LANES), BF16),
                        pltpu.VMEM((N_HEADS_B, V_ROWS, FOX_TQ), F32),
                        pltpu.VMEM((2, FOX_TKV, FOX_TQ), F32)],
        compiler_params=_params(2),
        name="forgetting_attention",
    )(qt, k, vt, fz, f_bias, tri, place)


def _outproj_kernel(x_ref, sc_ref, sh_ref, gate_ref, ya_ref, yb_ref,
                    g_ref, wg_ref, wua_ref, wub_ref, wo_ref, o_ref, merged_ref):
    x = x_ref[0]
    d = x.shape[-1]
    h = _modulated_norm(x, g_ref[...], sc_ref[...], sh_ref[...]).astype(BF16)
    ya = ya_ref[0]
    yb = yb_ref[0]
    for c in range(d // OUT_CHUNK):
        cs = slice(c * OUT_CHUNK, (c + 1) * OUT_CHUNK)
        cs_b = slice(d + c * OUT_CHUNK, d + (c + 1) * OUT_CHUNK)
        gate_a = jax.nn.sigmoid(_dot(h, wg_ref[:, cs]))
        gate_b = jax.nn.sigmoid(_dot(h, wg_ref[:, cs_b]))
        merged = gate_a * _dot(ya, wua_ref[:, cs]) + gate_b * _dot(yb, wub_ref[:, cs])
        merged_ref[:, cs] = merged.astype(BF16)
    o_ref[0] = x + gate_ref[...] * _dot(merged_ref[...], wo_ref[...])


def _out_projection(x, mod, ya, yb, params, l):
    batch, seq, d = x.shape
    tok = lambda w: pl.BlockSpec((1, TM, w), lambda b, i: (b, i, 0))
    return pl.pallas_call(
        _outproj_kernel,
        grid=(batch, seq // TM),
        in_specs=[tok(d), _mod_vector(mod, l, MOD_SCALE1), _mod_vector(mod, l, MOD_SHIFT1),
                  _mod_vector(mod, l, MOD_GATE1), tok(WIDTH_G), tok(WIDTH_B)]
        + [_of_layer(p, l) for p in params],
        out_specs=tok(d),
        out_shape=jax.ShapeDtypeStruct((batch, seq, d), F32),
        scratch_shapes=[pltpu.VMEM((TM, d), BF16)],
        compiler_params=_params(2),
        name="out_projection",
    )(x, mod, mod, mod, ya, yb, *params)


def _ffn_kernel(*refs, final_norm):
    x_ref, sc_ref, sh_ref, gate_ref, g_ref, win_ref, wout_ref = refs[:7]
    refs = refs[7:]
    if final_norm:
        gf_ref = refs[0]
        refs = refs[1:]
    o_ref, h_ref, acc_ref = refs
    x = x_ref[0]
    h_ref[...] = _modulated_norm(x, g_ref[...], sc_ref[...], sh_ref[...]).astype(BF16)
    for c in range(D_FF // FF_CHUNK):
        cs = slice(c * FF_CHUNK, (c + 1) * FF_CHUNK)
        cs_up = slice(D_FF + c * FF_CHUNK, D_FF + (c + 1) * FF_CHUNK)
        h = h_ref[...]
        gt = _dot(h, win_ref[:, cs])
        up = _dot(h, win_ref[:, cs_up])
        act = ((gt * jax.nn.sigmoid(gt)) * up).astype(BF16)
        part = _dot(act, wout_ref[cs, :])
        if c == 0:
            acc_ref[...] = part
        else:
            acc_ref[...] += part
    y = x + gate_ref[...] * acc_ref[...]
    if final_norm:
        y = (y * lax.rsqrt(jnp.mean(y * y, axis=-1, keepdims=True) + EPS)) * gf_ref[...]
    o_ref[0] = y


def _ffn(x, mod, params, l, g_final):
    batch, seq, d = x.shape
    final_norm = g_final is not None
    tok = pl.BlockSpec((1, TM, d), lambda b, i: (b, i, 0))
    in_specs = [tok, _mod_vector(mod, l, MOD_SCALE2), _mod_vector(mod, l, MOD_SHIFT2),
                _mod_vector(mod, l, MOD_GATE2)] + [_of_layer(p, l) for p in params]
    operands = [x, mod, mod, mod, *params]
    if final_norm:
        in_specs.append(_resident((1, d)))
        operands.append(g_final)
    return pl.pallas_call(
        functools.partial(_ffn_kernel, final_norm=final_norm),
        grid=(batch, seq // TM),
        in_specs=in_specs,
        out_specs=tok,
        out_shape=jax.ShapeDtypeStruct((batch, seq, d), F32),
        scratch_shapes=[pltpu.VMEM((TM, d), BF16), pltpu.VMEM((TM, d), F32)],
        compiler_params=_params(2),
        name="swiglu_ffn",
    )(*operands)


def _win_kernel(w_ref, wa_ref, wqt_ref, wvt_ref, wkf_ref, wg_ref):
    w = w_ref[0]
    scale = QK_SCALE * LOG2E
    for g in range(N_GROUPS):
        for part in range(3):
            src = part * WIDTH_A + g * WIDTH_G
            piece = w[:, src:src + WIDTH_G]
            dst = (3 * g + part) * WIDTH_G
            wa_ref[0, :, dst:dst + WIDTH_G] = (piece * scale if part == 0 else piece).astype(BF16)
    qb0 = 3 * WIDTH_A
    wqt_ref[0] = (w[:, qb0:qb0 + WIDTH_B] * scale).T.astype(BF16)
    wvt_ref[0] = w[:, qb0 + 2 * WIDTH_B:qb0 + 3 * WIDTH_B].T.astype(BF16)
    f0 = qb0 + 3 * WIDTH_B
    forget = w[:, f0:f0 + F_COLS]
    lane = lax.broadcasted_iota(jnp.int32, forget.shape, 1)
    wkf_ref[0, :, :WIDTH_B] = w[:, qb0 + WIDTH_B:qb0 + 2 * WIDTH_B].astype(BF16)
    wkf_ref[0, :, WIDTH_B:] = jnp.where(lane < N_HEADS_B, forget, 0.0).astype(BF16)
    g0 = f0 + N_HEADS_B
    wg_ref[0] = w[:, g0:g0 + 2 * D_MODEL].astype(BF16)


def _split_w_in(w_in):
    depth, d, d_in = w_in.shape
    assert d_in == 3 * WIDTH_A + 3 * WIDTH_B + N_HEADS_B + 2 * D_MODEL and d % W_ROWS == 0
    row_block = lambda cols: pl.BlockSpec((1, W_ROWS, cols), lambda l, i: (l, i, 0))
    col_block = pl.BlockSpec((1, WIDTH_B, W_ROWS), lambda l, i: (l, 0, i))
    widths = (3 * WIDTH_A, WIDTH_B + F_COLS, 2 * D_MODEL)
    stack = lambda *tail: jax.ShapeDtypeStruct((depth,) + tail, BF16)
    wa, wqt, wvt, wkf, wg = pl.pallas_call(
        _win_kernel,
        grid=(depth, d // W_ROWS),
        in_specs=[row_block(d_in)],
        out_specs=[row_block(widths[0]), col_block, col_block, row_block(widths[1]),
                   row_block(widths[2])],
        out_shape=[stack(d, widths[0]), stack(WIDTH_B, d), stack(WIDTH_B, d),
                   stack(d, widths[1]), stack(d, widths[2])],
        compiler_params=_params(2),
        name="in_projection_weight_layout",
    )(w_in)
    return wa, wqt, wvt, wkf, wg


def kernel(x, c, w_ada, b_ada, norm_mix, w_in, b_forget, w_up_a, w_up_b, w_out,
           norm_ffn, w_ffn_in, w_ffn_out, norm_final):
    depth = w_ada.shape[0]
    batch, seq, d = x.shape
    assert seq % TM == 0 and seq % FOX_TQ == 0 and FOX_TQ == FOX_TKV
    mod = _modulation(c, w_ada, b_ada).reshape(depth, batch, N_MOD, 1, d)
    wa, wqt, wvt, wkf, wg = _split_w_in(w_in)
    in_params = (norm_mix.reshape(depth, 1, d), wa, wqt, wvt, wkf)
    out_params = (norm_mix.reshape(depth, 1, d), wg, w_up_a.astype(BF16), w_up_b.astype(BF16),
                  w_out.astype(BF16))
    ffn_params = (norm_ffn.reshape(depth, 1, d), w_ffn_in.astype(BF16), w_ffn_out.astype(BF16))
    f_bias = jnp.pad(b_forget.astype(F32), ((0, 0), (0, F_COLS - N_HEADS_B)))[:, None, :]
    for l in range(depth):
        za0, za1, za2, qt, k, vt, fz = _in_projection(x, mod, in_params, l)
        ya = _dilated_mixture((za0, za1, za2))
        yb = _forgetting_attention(qt, k, vt, fz, f_bias, l)
        x = _out_projection(x, mod, ya, yb, out_params, l)
        g_final = norm_final.reshape(1, d) if l == depth - 1 else None
        x = _ffn(x, mod, ffn_params, l, g_final)
    return x
```

```python
import functools

import numpy as np
import jax
import jax.numpy as jnp
from jax import lax
from jax.experimental import pallas as pl
from jax.experimental.pallas import tpu as pltpu

F32 = jnp.float32
BF16 = jnp.bfloat16

D_MODEL = 1024
HEAD_DIM = 64
DIL_GROUPS = ((128, 1), (512, 4), (2048, 16))
HEADS_PER_GROUP = 4
N_GROUPS = len(DIL_GROUPS)
N_HEADS_A = HEADS_PER_GROUP * N_GROUPS
N_HEADS_B = 4
WIDTH_G = HEADS_PER_GROUP * HEAD_DIM
WIDTH_A = N_HEADS_A * HEAD_DIM
WIDTH_B = N_HEADS_B * HEAD_DIM
D_FF = -(-8 * D_MODEL // (3 * 256)) * 256
BLOCK = 128
ALIBI_MAX = 8.0
EPS = 1e-6
N_MOD = 6
MOD_SHIFT1, MOD_SCALE1, MOD_GATE1, MOD_SHIFT2, MOD_SCALE2, MOD_GATE2 = range(N_MOD)
QK_SCALE = HEAD_DIM ** -0.5

NEG = -1e30
LANES = 128
F_COLS = LANES
BF16_ROWS = 16
V_ROWS = HEAD_DIM + BF16_ROWS
Q_ROWS = LANES
DECAY_PIECES = 3
LOG2E = 1.4426950408889634

VMEM_LIMIT = 56 * 1024 * 1024

TM = 1024
W_ROWS = 256
FF_CHUNK = 256
OUT_CHUNK = 256
DECAY_CHUNK = 128
FOX_TQ = 512
FOX_TKV = 512
DIL_SPAN = BLOCK * max(d for _, d in DIL_GROUPS)
DIL_BLOCKS_IN_FLIGHT = 16


def _params(n_axes):
    return pltpu.CompilerParams(dimension_semantics=("arbitrary",) * n_axes,
                                vmem_limit_bytes=VMEM_LIMIT)


def _resident(shape):
    zeros = (0,) * len(shape)
    return pl.BlockSpec(shape, lambda *_: zeros, pipeline_mode=pl.Buffered(1))


def _of_layer(stacked, l):
    tail = stacked.shape[1:]
    index = (l,) + (0,) * len(tail)
    return pl.BlockSpec((None,) + tail, lambda *_: index, pipeline_mode=pl.Buffered(1))


def _mod_vector(mod, l, j):
    return pl.BlockSpec((None, None, None) + mod.shape[3:], lambda b, i: (l, b, j, 0, 0))


def _alibi_slopes():
    h = np.arange(1, N_HEADS_A + 1, dtype=np.float32)
    return np.asarray(2.0 ** (-ALIBI_MAX * h / N_HEADS_A), dtype=np.float32)


def _modulated_norm(x, g, sc, sh):
    y = x * lax.rsqrt(jnp.mean(x * x, axis=-1, keepdims=True) + EPS)
    return (y * g) * (1.0 + sc) + sh


def _dot(a, b):
    return jnp.dot(a, b, preferred_element_type=F32)


def _dot_nt(a, b):
    return lax.dot_general(a, b, (((1,), (1,)), ((), ())), preferred_element_type=F32)


def _split_bf16(x):
    pieces = []
    for _ in range(DECAY_PIECES):
        piece = x.astype(BF16)
        pieces.append(piece)
        x = x - piece.astype(F32)
    return pieces


def _mod_kernel(c_ref, w_ref, b_ref, o_ref):
    c = c_ref[...]
    c_act = (c * jax.nn.sigmoid(c)).astype(BF16)
    o_ref[0] = _dot(c_act, w_ref[0].astype(BF16)) + b_ref[0]


def _modulation(c, w_ada, b_ada):
    depth, d, n = w_ada.shape
    batch = c.shape[0]
    cols = 2 * d
    return pl.pallas_call(
        _mod_kernel,
        grid=(depth, n // cols),
        in_specs=[pl.BlockSpec((batch, d), lambda l, j: (0, 0)),
                  pl.BlockSpec((1, d, cols), lambda l, j: (l, 0, j)),
                  pl.BlockSpec((1, 1, cols), lambda l, j: (l, 0, j))],
        out_specs=pl.BlockSpec((1, batch, cols), lambda l, j: (l, 0, j)),
        out_shape=jax.ShapeDtypeStruct((depth, batch, n), F32),
        compiler_params=_params(2),
        name="adaln_modulation",
    )(c, w_ada, b_ada.reshape(depth, 1, n))


def _inproj_kernel(x_ref, sc_ref, sh_ref, g_ref, wa_ref, wqt_ref, wvt_ref, wkf_ref,
                   za0_ref, za1_ref, za2_ref, qt_ref, k_ref, vt_ref, fz_ref, stage_ref):
    h = _modulated_norm(x_ref[0], g_ref[...], sc_ref[...], sh_ref[...]).astype(BF16)
    tm = h.shape[0]
    wg = 3 * WIDTH_G
    for g, za_ref in enumerate((za0_ref, za1_ref, za2_ref)):
        dilation = DIL_GROUPS[g][1]
        za = _dot(h, wa_ref[:, g * wg:(g + 1) * wg])
        if dilation == 1:
            za_ref[0, 0] = za.astype(BF16)
            continue
        for c in range(wg // LANES):
            stage_ref[c] = za[:, c * LANES:(c + 1) * LANES]
        for r in range(dilation):
            rows = pl.ds(r, tm // dilation, stride=dilation)
            for c in range(wg // LANES):
                za_ref[0, r, :, c * LANES:(c + 1) * LANES] = stage_ref[c, rows, :].astype(BF16)
    qt = _dot_nt(wqt_ref[...], h).astype(BF16)
    vt = _dot_nt(wvt_ref[...], h).astype(BF16)
    q_tail = (lax.broadcasted_iota(jnp.int32, (Q_ROWS - HEAD_DIM, tm), 0)
              < DECAY_PIECES).astype(BF16)
    v_tail = (lax.broadcasted_iota(jnp.int32, (V_ROWS - HEAD_DIM, tm), 0) == 0).astype(BF16)
    for hd in range(N_HEADS_B):
        heads = slice(hd * HEAD_DIM, (hd + 1) * HEAD_DIM)
        qt_ref[0, hd * Q_ROWS:hd * Q_ROWS + HEAD_DIM, :] = qt[heads]
        qt_ref[0, hd * Q_ROWS + HEAD_DIM:(hd + 1) * Q_ROWS, :] = q_tail
        vt_ref[0, hd * V_ROWS:hd * V_ROWS + HEAD_DIM, :] = vt[heads]
        vt_ref[0, hd * V_ROWS + HEAD_DIM:(hd + 1) * V_ROWS, :] = v_tail
    kf = _dot(h, wkf_ref[...])
    k_ref[0] = kf[:, :WIDTH_B].astype(BF16)
    fz_ref[0] = kf[:, WIDTH_B:]


def _in_projection(x, mod, params, l):
    batch, seq, d = x.shape
    wg = 3 * WIDTH_G
    tok = lambda w: pl.BlockSpec((1, TM, w), lambda b, i: (b, i, 0))
    tok_t = lambda rows: pl.BlockSpec((1, rows, TM), lambda b, i: (b, 0, i))
    dils = [dilation for _, dilation in DIL_GROUPS]
    assert all(TM % (dilation * BF16_ROWS) == 0 for dilation in dils)
    sub_major = [pl.BlockSpec((1, dilation, TM // dilation, wg), lambda b, i: (b, 0, i, 0))
                 for dilation in dils]
    return pl.pallas_call(
        _inproj_kernel,
        grid=(batch, seq // TM),
        in_specs=[tok(d), _mod_vector(mod, l, MOD_SCALE1), _mod_vector(mod, l, MOD_SHIFT1)]
        + [_of_layer(p, l) for p in params],
        out_specs=sub_major + [tok_t(N_HEADS_B * Q_ROWS), tok(WIDTH_B),
                               tok_t(N_HEADS_B * V_ROWS), tok(F_COLS)],
        out_shape=[jax.ShapeDtypeStruct((batch, dilation, seq // dilation, wg), BF16)
                   for dilation in dils]
        + [jax.ShapeDtypeStruct((batch, N_HEADS_B * Q_ROWS, seq), BF16),
           jax.ShapeDtypeStruct((batch, seq, WIDTH_B), BF16),
           jax.ShapeDtypeStruct((batch, N_HEADS_B * V_ROWS, seq), BF16),
           jax.ShapeDtypeStruct((batch, seq, F_COLS), F32)],
        scratch_shapes=[pltpu.VMEM((wg // LANES, TM, LANES), F32)],
        compiler_params=_params(2),
        name="in_projection",
    )(x, mod, mod, *params)


def _band_bias():
    qi = np.arange(BLOCK)[:, None]
    kj = np.arange(2 * BLOCK)[None, :]
    dist = qi + BLOCK - kj
    in_band = (dist >= 0) & (dist <= BLOCK)
    slopes = _alibi_slopes()
    bias = np.empty((N_HEADS_A, BLOCK, 2 * BLOCK), np.float32)
    for g, (window, dilation) in enumerate(DIL_GROUPS):
        assert window // dilation == BLOCK
        for h in range(HEADS_PER_GROUP):
            head = g * HEADS_PER_GROUP + h
            alibi = -slopes[head] * (dist * dilation).astype(np.float32)
            bias[head] = np.where(in_band, alibi * np.float32(LOG2E), np.float32(NEG))
    return bias.reshape(N_HEADS_A // 2, 2 * BLOCK, 2 * BLOCK)


def _unroll(trips, blocks_per_trip=1):
    fits = [u for u in range(1, trips + 1)
            if trips % u == 0 and u * blocks_per_trip <= DIL_BLOCKS_IN_FLIGHT]
    return max(fits, default=1)


def _dilated_kernel(c0_ref, p0_ref, c1_ref, p1_ref, c2_ref, p2_ref, bias_ref, o_ref,
                    state_ref, gather_ref, stage_ref, first_bias_ref):
    span = pl.program_id(1)
    pairs = HEADS_PER_GROUP // 2
    kj = lax.broadcasted_iota(jnp.int32, (2 * BLOCK, 2 * BLOCK), 1)
    no_prev = jnp.where(kj >= jnp.where(span == 0, BLOCK, 0), 0.0, NEG)
    for slab in range(N_GROUPS * pairs):
        first_bias_ref[slab] = bias_ref[slab] + no_prev
    low_half = lax.broadcasted_iota(jnp.int32, (BLOCK, LANES), 1) < HEAD_DIM
    own_half = jnp.concatenate([low_half, jnp.logical_not(low_half)], axis=0)

    def attend(g, q_of, k_of, v_of, rows, first):
        for pair in range(pairs):
            q = q_of(pair)
            q = jnp.where(own_half, jnp.concatenate([q, q], axis=0), jnp.zeros((), q.dtype))
            bias = (first_bias_ref if first else bias_ref)[g * pairs + pair]
            z = _dot_nt(q, k_of(pair)) + bias
            m = jnp.max(z, axis=-1, keepdims=True)
            p = jnp.exp2(z - m)
            l = jnp.sum(p, axis=-1, keepdims=True)
            out = _dot(p.astype(BF16), v_of(pair)) / l
            lse = m + jnp.log2(l)
            state_ref[g, pair, 0, rows, :] = jnp.where(low_half, out[:BLOCK], out[BLOCK:])
            state_ref[g, pair, 1, rows, :] = jnp.where(low_half, lse[:BLOCK], lse[BLOCK:])

    for g, (cur_ref, prev_ref) in enumerate(((c0_ref, p0_ref), (c1_ref, p1_ref),
                                             (c2_ref, p2_ref))):
        dilation = DIL_GROUPS[g][1]
        class_rows = cur_ref.shape[2]
        nblk = class_rows // BLOCK

        def cols(part, pair):
            return slice(part * WIDTH_G + pair * LANES, part * WIDTH_G + (pair + 1) * LANES)

        def residue(r, carry, g=g, cur_ref=cur_ref, prev_ref=prev_ref, nblk=nblk,
                    cols=cols, class_rows=class_rows):
            def state_rows(n):
                return pl.ds(pl.multiple_of(r * class_rows + n * BLOCK, BLOCK), BLOCK)

            def with_prev(part):
                return lambda pair: jnp.concatenate(
                    [prev_ref[0, r, :, cols(part, pair)],
                     cur_ref[0, r, :BLOCK, cols(part, pair)]], axis=0)

            attend(g, lambda pair: cur_ref[0, r, :BLOCK, cols(0, pair)], with_prev(1),
                   with_prev(2), state_rows(0), first=True)

            def block(n, c):
                row0 = pl.multiple_of(n * BLOCK, BLOCK)
                keys = pl.ds(row0 - BLOCK, 2 * BLOCK)
                attend(g, lambda pair: cur_ref[0, r, pl.ds(row0, BLOCK), cols(0, pair)],
                       lambda pair: cur_ref[0, r, keys, cols(1, pair)],
                       lambda pair: cur_ref[0, r, keys, cols(2, pair)],
                       state_rows(n), first=False)
                return c

            if nblk > 1:
                lax.fori_loop(1, nblk, block, 0, unroll=_unroll(nblk - 1))
            return carry

        if dilation == 1:
            residue(0, 0)
        else:
            lax.fori_loop(0, dilation, residue, 0, unroll=_unroll(dilation, nblk))

    mid = DIL_GROUPS[1][1]
    assert [d for _, d in DIL_GROUPS] == [1, mid, mid * mid] and BLOCK % mid == 0
    mid_rows = DIL_SPAN // mid
    wide_rows = DIL_SPAN // (mid * mid)
    run = BLOCK // mid

    def merge(idx, carry):
        r = idx // (mid_rows // BLOCK)
        n = idx % (mid_rows // BLOCK)
        tokens = pl.ds(mid * BLOCK * n + r, BLOCK, stride=mid)
        mid_tokens = pl.ds(pl.multiple_of(r * mid_rows + n * BLOCK, BLOCK), BLOCK)
        for pair in range(pairs):
            def tiles(kind):
                for a in range(mid):
                    wide = pl.ds(pl.multiple_of((mid * a + r) * wide_rows + run * n, run), run)
                    gather_ref[pair, kind, pl.ds(a, run, stride=mid), :] = (
                        state_ref[2, pair, kind, wide, :])
                return (state_ref[0, pair, kind, tokens, :],
                        state_ref[1, pair, kind, mid_tokens, :], gather_ref[pair, kind])

            lses = tiles(1)
            top = jnp.maximum(jnp.maximum(lses[0], lses[1]), lses[2])
            weights = [jnp.exp2(lse - top) for lse in lses]
            num = sum(w * out for w, out in zip(weights, tiles(0)))
            stage_ref[pair, tokens, :] = num / sum(weights)
        return carry

    lax.fori_loop(0, DIL_SPAN // BLOCK, merge, 0)
    for pair in range(HEADS_PER_GROUP // 2):
        o_ref[0, :, pair * LANES:(pair + 1) * LANES] = stage_ref[pair].astype(o_ref.dtype)


def _dilated_mixture(zs):
    batch, _, seq, wg = zs[0].shape
    assert seq % DIL_SPAN == 0
    in_specs, operands = [], []
    for g, (_, dilation) in enumerate(DIL_GROUPS):
        rows = DIL_SPAN // dilation
        assert rows % BLOCK == 0 and zs[g].shape == (batch, dilation, seq // dilation, wg)
        nblk = rows // BLOCK
        in_specs.append(pl.BlockSpec((1, dilation, rows, wg), lambda b, t: (b, 0, t, 0)))
        in_specs.append(pl.BlockSpec(
            (1, dilation, BLOCK, wg),
            lambda b, t, nblk=nblk: (b, 0, jnp.maximum(t * nblk - 1, 0), 0)))
        operands += [zs[g], zs[g]]
    bias = jnp.asarray(_band_bias())
    return pl.pallas_call(
        _dilated_kernel,
        grid=(batch, seq // DIL_SPAN),
        in_specs=in_specs + [_resident(bias.shape)],
        out_specs=pl.BlockSpec((1, DIL_SPAN, WIDTH_G), lambda b, t: (b, t, 0)),
        out_shape=jax.ShapeDtypeStruct((batch, seq, WIDTH_G), BF16),
        scratch_shapes=[pltpu.VMEM((N_GROUPS, HEADS_PER_GROUP // 2, 2, DIL_SPAN, LANES), F32),
                        pltpu.VMEM((HEADS_PER_GROUP // 2, 2, BLOCK, LANES), F32),
                        pltpu.VMEM((HEADS_PER_GROUP // 2, DIL_SPAN, LANES), F32),
                        pltpu.VMEM(bias.shape, F32)],
        compiler_params=_params(2),
        name="dilated_attention",
    )(*operands, bias)


def _log_sigmoid(x):
    return jnp.minimum(x, 0.0) - jnp.log1p(jnp.exp(-jnp.abs(x)))


def _fox_place():
    place = np.zeros((WIDTH_B + DECAY_PIECES * F_COLS, N_HEADS_B * LANES), np.float32)
    for c in range(WIDTH_B):
        place[c, (c // HEAD_DIM) * LANES + c % HEAD_DIM] = 1.0
    for i in range(DECAY_PIECES):
        for h in range(N_HEADS_B):
            place[WIDTH_B + i * F_COLS + h, h * LANES + HEAD_DIM + i] = 1.0
    return place


def _fox_kernel(qt_ref, k_ref, vt_ref, fz_ref, b_ref, tri_ref, place_ref, o_ref,
                kaug_ref, acc_ref, s_ref):
    tq = qt_ref.shape[2]
    seq = k_ref.shape[1]
    qb = pl.program_id(1)
    q0 = qb * tq
    n_full = qb
    key_minus_query = (lax.broadcasted_iota(jnp.int32, (FOX_TKV, tq), 0)
                       - lax.broadcasted_iota(jnp.int32, (FOX_TKV, tq), 1))

    @pl.when(qb == 0)
    def _():
        tri = tri_ref[...]
        carry = jnp.zeros((1, F_COLS), F32)
        for c in range(seq // FOX_TKV):
            rows = slice(c * FOX_TKV, (c + 1) * FOX_TKV)
            sums = []
            for part in range(FOX_TKV // DECAY_CHUNK):
                at = c * FOX_TKV + part * DECAY_CHUNK
                lf = _log_sigmoid(fz_ref[0, at:at + DECAY_CHUNK, :] + b_ref[...])
                sums.append(sum(_dot(tri, piece) for piece in _split_bf16(lf)) + carry)
                carry = sums[-1][DECAY_CHUNK - 1:, :]
            pieces = _split_bf16(jnp.concatenate(sums, axis=0) * (-LOG2E))
            packed = jnp.concatenate([k_ref[0, rows, :]] + pieces, axis=1)
            kaug_ref[rows, :] = _dot(packed, place_ref[...]).astype(BF16)

    acc_ref[...] = jnp.zeros_like(acc_ref)

    def scores(h, k0, masked):
        keys = pl.ds(pl.multiple_of(k0, FOX_TKV), FOX_TKV)
        s = _dot(kaug_ref[keys, h * LANES:(h + 1) * LANES],
                 qt_ref[0, h * Q_ROWS:(h + 1) * Q_ROWS, :])
        if masked:
            s = jnp.where(key_minus_query <= q0 - k0, s, NEG)
        s_ref[h % 2] = s
        return jnp.max(s, axis=0, keepdims=True)

    def kv_block(j, carry, diagonal):
        ms, top_next = carry
        k0 = pl.multiple_of(j * FOX_TKV, FOX_TKV)
        new_ms = []
        for h in range(N_HEADS_B):
            m_new = jnp.maximum(ms[h], top_next)
            if h + 1 < N_HEADS_B:
                top_next = scores(h + 1, k0, masked=diagonal)
            elif not diagonal:
                top_next = scores(0, k0 + FOX_TKV, masked=True)
            alpha = jnp.exp2(ms[h] - m_new)
            p = jnp.exp2(s_ref[h % 2] - m_new).astype(BF16)
            vt = vt_ref[0, h * V_ROWS:(h + 1) * V_ROWS, pl.ds(k0, FOX_TKV)]
            acc_ref[h] = acc_ref[h] * alpha + _dot(vt, p)
            new_ms.append(m_new)
        return tuple(new_ms), top_next

    ms = tuple(jnp.full((1, tq), NEG, F32) for _ in range(N_HEADS_B))
    carry = (ms, scores(0, 0, masked=True))
    below = functools.partial(kv_block, diagonal=False)
    carry = lax.fori_loop(0, n_full // 2, lambda i, c: below(2 * i + 1, below(2 * i, c)), carry)
    carry = lax.cond(n_full % 2 == 1, lambda c: below(n_full - 1, c), lambda c: c, carry)
    kv_block(n_full, carry, diagonal=True)
    outs = []
    for h in range(N_HEADS_B):
        acc = acc_ref[h]
        outs.append(acc[:HEAD_DIM] / acc[HEAD_DIM:HEAD_DIM + 1])
    o_ref[0] = jnp.concatenate(outs, axis=0).T.astype(o_ref.dtype)


def _forgetting_attention(qt, k, vt, fz, f_bias, l):
    batch, seq, _ = k.shape
    assert FOX_TKV % DECAY_CHUNK == 0
    place = jnp.asarray(_fox_place(), dtype=BF16)
    idx = np.arange(DECAY_CHUNK)
    tri = jnp.asarray(idx[None, :] <= idx[:, None], dtype=BF16)
    return pl.pallas_call(
        _fox_kernel,
        grid=(batch, seq // FOX_TQ),
        in_specs=[pl.BlockSpec((1, N_HEADS_B * Q_ROWS, FOX_TQ), lambda b, i: (b, 0, i)),
                  pl.BlockSpec((1, seq, WIDTH_B), lambda b, i: (b, 0, 0)),
                  pl.BlockSpec((1, N_HEADS_B * V_ROWS, seq), lambda b, i: (b, 0, 0)),
                  pl.BlockSpec((1, seq, F_COLS), lambda b, i: (b, 0, 0)),
                  _of_layer(f_bias, l), _resident(tri.shape), _resident(place.shape)],
        out_specs=pl.BlockSpec((1, FOX_TQ, WIDTH_B), lambda b, i: (b, i, 0)),
        out_shape=jax.ShapeDtypeStruct((batch, seq, WIDTH_B), BF16),
        scratch_shapes=[pltpu.VMEM((seq, N_HEADS_B * LANES), BF16),
                        pltpu.VMEM((N_HEADS_B, V_ROWS, FOX_TQ), F32),
                        pltpu.VMEM((2, FOX_TKV, FOX_TQ), F32)],
        compiler_params=_params(2),
        name="forgetting_attention",
    )(qt, k, vt, fz, f_bias, tri, place)


def _outproj_kernel(x_ref, sc_ref, sh_ref, gate_ref, ya_ref, yb_ref,
                    g_ref, wg_ref, wua_ref, wub_ref, wo_ref, o_ref, merged_ref):
    x = x_ref[0]
    d = x.shape[-1]
    h = _modulated_norm(x, g_ref[...], sc_ref[...], sh_ref[...]).astype(BF16)
    ya = ya_ref[0]
    yb = yb_ref[0]
    for c in range(d // OUT_CHUNK):
        cs = slice(c * OUT_CHUNK, (c + 1) * OUT_CHUNK)
        cs_b = slice(d + c * OUT_CHUNK, d + (c + 1) * OUT_CHUNK)
        gate_a = jax.nn.sigmoid(_dot(h, wg_ref[:, cs]))
        gate_b = jax.nn.sigmoid(_dot(h, wg_ref[:, cs_b]))
        merged = gate_a * _dot(ya, wua_ref[:, cs]) + gate_b * _dot(yb, wub_ref[:, cs])
        merged_ref[:, cs] = merged.astype(BF16)
    o_ref[0] = x + gate_ref[...] * _dot(merged_ref[...], wo_ref[...])


def _out_projection(x, mod, ya, yb, params, l):
    batch, seq, d = x.shape
    tok = lambda w: pl.BlockSpec((1, TM, w), lambda b, i: (b, i, 0))
    return pl.pallas_call(
        _outproj_kernel,
        grid=(batch, seq // TM),
        in_specs=[tok(d), _mod_vector(mod, l, MOD_SCALE1), _mod_vector(mod, l, MOD_SHIFT1),
                  _mod_vector(mod, l, MOD_GATE1), tok(WIDTH_G), tok(WIDTH_B)]
        + [_of_layer(p, l) for p in params],
        out_specs=tok(d),
        out_shape=jax.ShapeDtypeStruct((batch, seq, d), F32),
        scratch_shapes=[pltpu.VMEM((TM, d), BF16)],
        compiler_params=_params(2),
        name="out_projection",
    )(x, mod, mod, mod, ya, yb, *params)


def _ffn_kernel(*refs, final_norm):
    x_ref, sc_ref, sh_ref, gate_ref, g_ref, win_ref, wout_ref = refs[:7]
    refs = refs[7:]
    if final_norm:
        gf_ref = refs[0]
        refs = refs[1:]
    o_ref, h_ref, acc_ref = refs
    x = x_ref[0]
    h_ref[...] = _modulated_norm(x, g_ref[...], sc_ref[...], sh_ref[...]).astype(BF16)
    for c in range(D_FF // FF_CHUNK):
        cs = slice(c * FF_CHUNK, (c + 1) * FF_CHUNK)
        cs_up = slice(D_FF + c * FF_CHUNK, D_FF + (c + 1) * FF_CHUNK)
        h = h_ref[...]
        gt = _dot(h, win_ref[:, cs])
        up = _dot(h, win_ref[:, cs_up])
        act = ((gt * jax.nn.sigmoid(gt)) * up).astype(BF16)
        part = _dot(act, wout_ref[cs, :])
        if c == 0:
            acc_ref[...] = part
        else:
            acc_ref[...] += part
    y = x + gate_ref[...] * acc_ref[...]
    if final_norm:
        y = (y * lax.rsqrt(jnp.mean(y * y, axis=-1, keepdims=True) + EPS)) * gf_ref[...]
    o_ref[0] = y


def _ffn(x, mod, params, l, g_final):
    batch, seq, d = x.shape
    final_norm = g_final is not None
    tok = pl.BlockSpec((1, TM, d), lambda b, i: (b, i, 0))
    in_specs = [tok, _mod_vector(mod, l, MOD_SCALE2), _mod_vector(mod, l, MOD_SHIFT2),
                _mod_vector(mod, l, MOD_GATE2)] + [_of_layer(p, l) for p in params]
    operands = [x, mod, mod, mod, *params]
    if final_norm:
        in_specs.append(_resident((1, d)))
        operands.append(g_final)
    return pl.pallas_call(
        functools.partial(_ffn_kernel, final_norm=final_norm),
        grid=(batch, seq // TM),
        in_specs=in_specs,
        out_specs=tok,
        out_shape=jax.ShapeDtypeStruct((batch, seq, d), F32),
        scratch_shapes=[pltpu.VMEM((TM, d), BF16), pltpu.VMEM((TM, d), F32)],
        compiler_params=_params(2),
        name="swiglu_ffn",
    )(*operands)


def _win_kernel(w_ref, wa_ref, wqt_ref, wvt_ref, wkf_ref, wg_ref):
    w = w_ref[0]
    scale = QK_SCALE * LOG2E
    for g in range(N_GROUPS):
        for part in range(3):
            src = part * WIDTH_A + g * WIDTH_G
            piece = w[:, src:src + WIDTH_G]
            dst = (3 * g + part) * WIDTH_G
            wa_ref[0, :, dst:dst + WIDTH_G] = (piece * scale if part == 0 else piece).astype(BF16)
    qb0 = 3 * WIDTH_A
    wqt_ref[0] = (w[:, qb0:qb0 + WIDTH_B] * scale).T.astype(BF16)
    wvt_ref[0] = w[:, qb0 + 2 * WIDTH_B:qb0 + 3 * WIDTH_B].T.astype(BF16)
    f0 = qb0 + 3 * WIDTH_B
    forget = w[:, f0:f0 + F_COLS]
    lane = lax.broadcasted_iota(jnp.int32, forget.shape, 1)
    wkf_ref[0, :, :WIDTH_B] = w[:, qb0 + WIDTH_B:qb0 + 2 * WIDTH_B].astype(BF16)
    wkf_ref[0, :, WIDTH_B:] = jnp.where(lane < N_HEADS_B, forget, 0.0).astype(BF16)
    g0 = f0 + N_HEADS_B
    wg_ref[0] = w[:, g0:g0 + 2 * D_MODEL].astype(BF16)


def _split_w_in(w_in):
    depth, d, d_in = w_in.shape
    assert d_in == 3 * WIDTH_A + 3 * WIDTH_B + N_HEADS_B + 2 * D_MODEL and d % W_ROWS == 0
    row_block = lambda cols: pl.BlockSpec((1, W_ROWS, cols), lambda l, i: (l, i, 0))
    col_block = pl.BlockSpec((1, WIDTH_B, W_ROWS), lambda l, i: (l, 0, i))
    widths = (3 * WIDTH_A, WIDTH_B + F_COLS, 2 * D_MODEL)
    stack = lambda *tail: jax.ShapeDtypeStruct((depth,) + tail, BF16)
    wa, wqt, wvt, wkf, wg = pl.pallas_call(
        _win_kernel,
        grid=(depth, d // W_ROWS),
        in_specs=[row_block(d_in)],
        out_specs=[row_block(widths[0]), col_block, col_block, row_block(widths[1]),
                   row_block(widths[2])],
        out_shape=[stack(d, widths[0]), stack(WIDTH_B, d), stack(WIDTH_B, d),
                   stack(d, widths[1]), stack(d, widths[2])],
        compiler_params=_params(2),
        name="in_projection_weight_layout",
    )(w_in)
    return wa, wqt, wvt, wkf, wg


def kernel(x, c, w_ada, b_ada, norm_mix, w_in, b_forget, w_up_a, w_up_b, w_out,
           norm_ffn, w_ffn_in, w_ffn_out, norm_final):
    depth = w_ada.shape[0]
    batch, seq, d = x.shape
    assert seq % TM == 0 and seq % FOX_TQ == 0 and FOX_TQ == FOX_TKV
    mod = _modulation(c, w_ada, b_ada).reshape(depth, batch, N_MOD, 1, d)
    wa, wqt, wvt, wkf, wg = _split_w_in(w_in)
    in_params = (norm_mix.reshape(depth, 1, d), wa, wqt, wvt, wkf)
    out_params = (norm_mix.reshape(depth, 1, d), wg, w_up_a.astype(BF16), w_up_b.astype(BF16),
                  w_out.astype(BF16))
    ffn_params = (norm_ffn.reshape(depth, 1, d), w_ffn_in.astype(BF16), w_ffn_out.astype(BF16))
    f_bias = jnp.pad(b_forget.astype(F32), ((0, 0), (0, F_COLS - N_HEADS_B)))[:, None, :]
    for l in range(depth):
        za0, za1, za2, qt, k, vt, fz = _in_projection(x, mod, in_params, l)
        ya = _dilated_mixture((za0, za1, za2))
        yb = _forgetting_attention(qt, k, vt, fz, f_bias, l)
        x = _out_projection(x, mod, ya, yb, out_params, l)
        g_final = norm_final.reshape(1, d) if l == depth - 1 else None
        x = _ffn(x, mod, ffn_params, l, g_final)
    return x
```

```python
import functools

import numpy as np
import jax
import jax.numpy as jnp
from jax import lax
from jax.experimental import pallas as pl
from jax.experimental.pallas import tpu as pltpu

F32 = jnp.float32
BF16 = jnp.bfloat16

D_MODEL = 1024
HEAD_DIM = 64
DIL_GROUPS = ((128, 1), (512, 4), (2048, 16))
HEADS_PER_GROUP = 4
N_GROUPS = len(DIL_GROUPS)
N_HEADS_A = HEADS_PER_GROUP * N_GROUPS
N_HEADS_B = 4
WIDTH_G = HEADS_PER_GROUP * HEAD_DIM
WIDTH_A = N_HEADS_A * HEAD_DIM
WIDTH_B = N_HEADS_B * HEAD_DIM
D_FF = -(-8 * D_MODEL // (3 * 256)) * 256
BLOCK = 128
ALIBI_MAX = 8.0
EPS = 1e-6
N_MOD = 6
MOD_SHIFT1, MOD_SCALE1, MOD_GATE1, MOD_SHIFT2, MOD_SCALE2, MOD_GATE2 = range(N_MOD)
QK_SCALE = HEAD_DIM ** -0.5

NEG = -1e30
LANES = 128
F_COLS = LANES
BF16_ROWS = 16
V_ROWS = HEAD_DIM + BF16_ROWS
Q_ROWS = LANES
DECAY_PIECES = 3
LOG2E = 1.4426950408889634

VMEM_LIMIT = 56 * 1024 * 1024

TM = 1024
TM_TAIL = 512
W_ROWS = 256
FF_CHUNK = 256
OUT_CHUNK = 256
DECAY_CHUNK = 128
FOX_TQ = 512
FOX_TKV = 512
DIL_SPAN = BLOCK * max(d for _, d in DIL_GROUPS)
DIL_BLOCKS_IN_FLIGHT = 16


def _params(n_axes):
    return pltpu.CompilerParams(dimension_semantics=("arbitrary",) * n_axes,
                                vmem_limit_bytes=VMEM_LIMIT)


def _resident(shape):
    zeros = (0,) * len(shape)
    return pl.BlockSpec(shape, lambda *_: zeros, pipeline_mode=pl.Buffered(1))


def _of_layer(stacked, l):
    tail = stacked.shape[1:]
    index = (l,) + (0,) * len(tail)
    return pl.BlockSpec((None,) + tail, lambda *_: index, pipeline_mode=pl.Buffered(1))


def _mod_vector(mod, l, j):
    return pl.BlockSpec((None, None, None) + mod.shape[3:], lambda b, i: (l, b, j, 0, 0))


def _alibi_slopes():
    h = np.arange(1, N_HEADS_A + 1, dtype=np.float32)
    return np.asarray(2.0 ** (-ALIBI_MAX * h / N_HEADS_A), dtype=np.float32)


def _modulated_norm(x, g, sc, sh):
    y = x * lax.rsqrt(jnp.mean(x * x, axis=-1, keepdims=True) + EPS)
    return (y * g) * (1.0 + sc) + sh


def _dot(a, b):
    return jnp.dot(a, b, preferred_element_type=F32)


def _dot_nt(a, b):
    return lax.dot_general(a, b, (((1,), (1,)), ((), ())), preferred_element_type=F32)


def _split_bf16(x):
    pieces = []
    for _ in range(DECAY_PIECES):
        piece = x.astype(BF16)
        pieces.append(piece)
        x = x - piece.astype(F32)
    return pieces


def _mod_kernel(c_ref, w_ref, b_ref, o_ref):
    c = c_ref[...]
    c_act = (c * jax.nn.sigmoid(c)).astype(BF16)
    o_ref[0] = _dot(c_act, w_ref[0].astype(BF16)) + b_ref[0]


def _modulation(c, w_ada, b_ada):
    depth, d, n = w_ada.shape
    batch = c.shape[0]
    cols = 2 * d
    return pl.pallas_call(
        _mod_kernel,
        grid=(depth, n // cols),
        in_specs=[pl.BlockSpec((batch, d), lambda l, j: (0, 0)),
                  pl.BlockSpec((1, d, cols), lambda l, j: (l, 0, j)),
                  pl.BlockSpec((1, 1, cols), lambda l, j: (l, 0, j))],
        out_specs=pl.BlockSpec((1, batch, cols), lambda l, j: (l, 0, j)),
        out_shape=jax.ShapeDtypeStruct((depth, batch, n), F32),
        compiler_params=_params(2),
        name="adaln_modulation",
    )(c, w_ada, b_ada.reshape(depth, 1, n))


def _inproj_kernel(x_ref, sc_ref, sh_ref, g_ref, wa_ref, wqt_ref, wvt_ref, wkf_ref,
                   za0_ref, za1_ref, za2_ref, qt_ref, k_ref, vt_ref, fz_ref, stage_ref):
    h = _modulated_norm(x_ref[0], g_ref[...], sc_ref[...], sh_ref[...]).astype(BF16)
    tm = h.shape[0]
    wg = 3 * WIDTH_G
    for g, za_ref in enumerate((za0_ref, za1_ref, za2_ref)):
        dilation = DIL_GROUPS[g][1]
        za = _dot(h, wa_ref[:, g * wg:(g + 1) * wg])
        if dilation == 1:
            za_ref[0, 0] = za.astype(BF16)
            continue
        for c in range(wg // LANES):
            stage_ref[c] = za[:, c * LANES:(c + 1) * LANES]
        for r in range(dilation):
            rows = pl.ds(r, tm // dilation, stride=dilation)
            for c in range(wg // LANES):
                za_ref[0, r, :, c * LANES:(c + 1) * LANES] = stage_ref[c, rows, :].astype(BF16)
    qt = _dot_nt(wqt_ref[...], h).astype(BF16)
    vt = _dot_nt(wvt_ref[...], h).astype(BF16)
    q_tail = (lax.broadcasted_iota(jnp.int32, (Q_ROWS - HEAD_DIM, tm), 0)
              < DECAY_PIECES).astype(BF16)
    v_tail = (lax.broadcasted_iota(jnp.int32, (V_ROWS - HEAD_DIM, tm), 0) == 0).astype(BF16)
    for hd in range(N_HEADS_B):
        heads = slice(hd * HEAD_DIM, (hd + 1) * HEAD_DIM)
        qt_ref[0, hd * Q_ROWS:hd * Q_ROWS + HEAD_DIM, :] = qt[heads]
        qt_ref[0, hd * Q_ROWS + HEAD_DIM:(hd + 1) * Q_ROWS, :] = q_tail
        vt_ref[0, hd * V_ROWS:hd * V_ROWS + HEAD_DIM, :] = vt[heads]
        vt_ref[0, hd * V_ROWS + HEAD_DIM:(hd + 1) * V_ROWS, :] = v_tail
    kf = _dot(h, wkf_ref[...])
    k_ref[0] = kf[:, :WIDTH_B].astype(BF16)
    fz_ref[0] = kf[:, WIDTH_B:]


def _in_projection(x, mod, params, l):
    batch, seq, d = x.shape
    wg = 3 * WIDTH_G
    tok = lambda w: pl.BlockSpec((1, TM, w), lambda b, i: (b, i, 0))
    tok_t = lambda rows: pl.BlockSpec((1, rows, TM), lambda b, i: (b, 0, i))
    dils = [dilation for _, dilation in DIL_GROUPS]
    assert all(TM % (dilation * BF16_ROWS) == 0 for dilation in dils)
    sub_major = [pl.BlockSpec((1, dilation, TM // dilation, wg), lambda b, i: (b, 0, i, 0))
                 for dilation in dils]
    return pl.pallas_call(
        _inproj_kernel,
        grid=(batch, seq // TM),
        in_specs=[tok(d), _mod_vector(mod, l, MOD_SCALE1), _mod_vector(mod, l, MOD_SHIFT1)]
        + [_of_layer(p, l) for p in params],
        out_specs=sub_major + [tok_t(N_HEADS_B * Q_ROWS), tok(WIDTH_B),
                               tok_t(N_HEADS_B * V_ROWS), tok(F_COLS)],
        out_shape=[jax.ShapeDtypeStruct((batch, dilation, seq // dilation, wg), BF16)
                   for dilation in dils]
        + [jax.ShapeDtypeStruct((batch, N_HEADS_B * Q_ROWS, seq), BF16),
           jax.ShapeDtypeStruct((batch, seq, WIDTH_B), BF16),
           jax.ShapeDtypeStruct((batch, N_HEADS_B * V_ROWS, seq), BF16),
           jax.ShapeDtypeStruct((batch, seq, F_COLS), F32)],
        scratch_shapes=[pltpu.VMEM((wg // LANES, TM, LANES), F32)],
        compiler_params=_params(2),
        name="in_projection",
    )(x, mod, mod, *params)


def _band_bias():
    qi = np.arange(BLOCK)[:, None]
    kj = np.arange(2 * BLOCK)[None, :]
    dist = qi + BLOCK - kj
    in_band = (dist >= 0) & (dist <= BLOCK)
    slopes = _alibi_slopes()
    bias = np.empty((N_HEADS_A, BLOCK, 2 * BLOCK), np.float32)
    for g, (window, dilation) in enumerate(DIL_GROUPS):
        assert window // dilation == BLOCK
        for h in range(HEADS_PER_GROUP):
            head = g * HEADS_PER_GROUP + h
            alibi = -slopes[head] * (dist * dilation).astype(np.float32)
            bias[head] = np.where(in_band, alibi * np.float32(LOG2E), np.float32(NEG))
    return bias.reshape(N_HEADS_A // 2, 2 * BLOCK, 2 * BLOCK)


def _unroll(trips, blocks_per_trip=1):
    fits = [u for u in range(1, trips + 1)
            if trips % u == 0 and u * blocks_per_trip <= DIL_BLOCKS_IN_FLIGHT]
    return max(fits, default=1)


def _dilated_kernel(c0_ref, p0_ref, c1_ref, p1_ref, c2_ref, p2_ref, bias_ref, o_ref,
                    state_ref, gather_ref, stage_ref, first_bias_ref):
    span = pl.program_id(1)
    pairs = HEADS_PER_GROUP // 2
    kj = lax.broadcasted_iota(jnp.int32, (2 * BLOCK, 2 * BLOCK), 1)
    no_prev = jnp.where(kj >= jnp.where(span == 0, BLOCK, 0), 0.0, NEG)
    for slab in range(N_GROUPS * pairs):
        first_bias_ref[slab] = bias_ref[slab] + no_prev
    low_half = lax.broadcasted_iota(jnp.int32, (BLOCK, LANES), 1) < HEAD_DIM
    own_half = jnp.concatenate([low_half, jnp.logical_not(low_half)], axis=0)

    def attend(g, q_of, k_of, v_of, rows, first):
        for pair in range(pairs):
            q = q_of(pair)
            q = jnp.where(own_half, jnp.concatenate([q, q], axis=0), jnp.zeros((), q.dtype))
            bias = (first_bias_ref if first else bias_ref)[g * pairs + pair]
            z = _dot_nt(q, k_of(pair)) + bias
            m = jnp.max(z, axis=-1, keepdims=True)
            p = jnp.exp2(z - m)
            l = jnp.sum(p, axis=-1, keepdims=True)
            out = _dot(p.astype(BF16), v_of(pair)) / l
            lse = m + jnp.log2(l)
            state_ref[g, pair, 0, rows, :] = jnp.where(low_half, out[:BLOCK], out[BLOCK:])
            state_ref[g, pair, 1, rows, :] = jnp.where(low_half, lse[:BLOCK], lse[BLOCK:])

    for g, (cur_ref, prev_ref) in enumerate(((c0_ref, p0_ref), (c1_ref, p1_ref),
                                             (c2_ref, p2_ref))):
        dilation = DIL_GROUPS[g][1]
        class_rows = cur_ref.shape[2]
        nblk = class_rows // BLOCK

        def cols(part, pair):
            return slice(part * WIDTH_G + pair * LANES, part * WIDTH_G + (pair + 1) * LANES)

        def residue(r, carry, g=g, cur_ref=cur_ref, prev_ref=prev_ref, nblk=nblk,
                    cols=cols, class_rows=class_rows):
            def state_rows(n):
                return pl.ds(pl.multiple_of(r * class_rows + n * BLOCK, BLOCK), BLOCK)

            def with_prev(part):
                return lambda pair: jnp.concatenate(
                    [prev_ref[0, r, :, cols(part, pair)],
                     cur_ref[0, r, :BLOCK, cols(part, pair)]], axis=0)

            attend(g, lambda pair: cur_ref[0, r, :BLOCK, cols(0, pair)], with_prev(1),
                   with_prev(2), state_rows(0), first=True)

            def block(n, c):
                row0 = pl.multiple_of(n * BLOCK, BLOCK)
                keys = pl.ds(row0 - BLOCK, 2 * BLOCK)
                attend(g, lambda pair: cur_ref[0, r, pl.ds(row0, BLOCK), cols(0, pair)],
                       lambda pair: cur_ref[0, r, keys, cols(1, pair)],
                       lambda pair: cur_ref[0, r, keys, cols(2, pair)],
                       state_rows(n), first=False)
                return c

            if nblk > 1:
                lax.fori_loop(1, nblk, block, 0, unroll=_unroll(nblk - 1))
            return carry

        if dilation == 1:
            residue(0, 0)
        else:
            lax.fori_loop(0, dilation, residue, 0, unroll=_unroll(dilation, nblk))

    mid = DIL_GROUPS[1][1]
    assert [d for _, d in DIL_GROUPS] == [1, mid, mid * mid] and BLOCK % mid == 0
    mid_rows = DIL_SPAN // mid
    wide_rows = DIL_SPAN // (mid * mid)
    run = BLOCK // mid

    def merge(idx, carry):
        r = idx // (mid_rows // BLOCK)
        n = idx % (mid_rows // BLOCK)
        tokens = pl.ds(mid * BLOCK * n + r, BLOCK, stride=mid)
        mid_tokens = pl.ds(pl.multiple_of(r * mid_rows + n * BLOCK, BLOCK), BLOCK)
        for pair in range(pairs):
            def tiles(kind):
                for a in range(mid):
                    wide = pl.ds(pl.multiple_of((mid * a + r) * wide_rows + run * n, run), run)
                    gather_ref[pair, kind, pl.ds(a, run, stride=mid), :] = (
                        state_ref[2, pair, kind, wide, :])
                return (state_ref[0, pair, kind, tokens, :],
                        state_ref[1, pair, kind, mid_tokens, :], gather_ref[pair, kind])

            lses = tiles(1)
            top = jnp.maximum(jnp.maximum(lses[0], lses[1]), lses[2])
            weights = [jnp.exp2(lse - top) for lse in lses]
            num = sum(w * out for w, out in zip(weights, tiles(0)))
            stage_ref[pair, tokens, :] = num / sum(weights)
        return carry

    lax.fori_loop(0, DIL_SPAN // BLOCK, merge, 0)
    for pair in range(HEADS_PER_GROUP // 2):
        o_ref[0, :, pair * LANES:(pair + 1) * LANES] = stage_ref[pair].astype(o_ref.dtype)


def _dilated_mixture(zs):
    batch, _, seq, wg = zs[0].shape
    assert seq % DIL_SPAN == 0
    in_specs, operands = [], []
    for g, (_, dilation) in enumerate(DIL_GROUPS):
        rows = DIL_SPAN // dilation
        assert rows % BLOCK == 0 and zs[g].shape == (batch, dilation, seq // dilation, wg)
        nblk = rows // BLOCK
        in_specs.append(pl.BlockSpec((1, dilation, rows, wg), lambda b, t: (b, 0, t, 0)))
        in_specs.append(pl.BlockSpec(
            (1, dilation, BLOCK, wg),
            lambda b, t, nblk=nblk: (b, 0, jnp.maximum(t * nblk - 1, 0), 0)))
        operands += [zs[g], zs[g]]
    bias = jnp.asarray(_band_bias())
    return pl.pallas_call(
        _dilated_kernel,
        grid=(batch, seq // DIL_SPAN),
        in_specs=in_specs + [_resident(bias.shape)],
        out_specs=pl.BlockSpec((1, DIL_SPAN, WIDTH_G), lambda b, t: (b, t, 0)),
        out_shape=jax.ShapeDtypeStruct((batch, seq, WIDTH_G), BF16),
        scratch_shapes=[pltpu.VMEM((N_GROUPS, HEADS_PER_GROUP // 2, 2, DIL_SPAN, LANES), F32),
                        pltpu.VMEM((HEADS_PER_GROUP // 2, 2, BLOCK, LANES), F32),
                        pltpu.VMEM((HEADS_PER_GROUP // 2, DIL_SPAN, LANES), F32),
                        pltpu.VMEM(bias.shape, F32)],
        compiler_params=_params(2),
        name="dilated_attention",
    )(*operands, bias)


def _log_sigmoid(x):
    return jnp.minimum(x, 0.0) - jnp.log1p(jnp.exp(-jnp.abs(x)))


def _fox_place():
    place = np.zeros((WIDTH_B + DECAY_PIECES * F_COLS, N_HEADS_B * LANES), np.float32)
    for c in range(WIDTH_B):
        place[c, (c // HEAD_DIM) * LANES + c % HEAD_DIM] = 1.0
    for i in range(DECAY_PIECES):
        for h in range(N_HEADS_B):
            place[WIDTH_B + i * F_COLS + h, h * LANES + HEAD_DIM + i] = 1.0
    return place


def _fox_kernel(qt_ref, k_ref, vt_ref, fz_ref, b_ref, tri_ref, place_ref, o_ref,
                kaug_ref, acc_ref, s_ref):
    tq = qt_ref.shape[2]
    seq = k_ref.shape[1]
    qb = pl.program_id(1)
    q0 = qb * tq
    n_full = qb
    key_minus_query = (lax.broadcasted_iota(jnp.int32, (FOX_TKV, tq), 0)
                       - lax.broadcasted_iota(jnp.int32, (FOX_TKV, tq), 1))

    @pl.when(qb == 0)
    def _():
        tri = tri_ref[...]
        carry = jnp.zeros((1, F_COLS), F32)
        for c in range(seq // FOX_TKV):
            rows = slice(c * FOX_TKV, (c + 1) * FOX_TKV)
            sums = []
            for part in range(FOX_TKV // DECAY_CHUNK):
                at = c * FOX_TKV + part * DECAY_CHUNK
                lf = _log_sigmoid(fz_ref[0, at:at + DECAY_CHUNK, :] + b_ref[...])
                sums.append(sum(_dot(tri, piece) for piece in _split_bf16(lf)) + carry)
                carry = sums[-1][DECAY_CHUNK - 1:, :]
            pieces = _split_bf16(jnp.concatenate(sums, axis=0) * (-LOG2E))
            packed = jnp.concatenate([k_ref[0, rows, :]] + pieces, axis=1)
            kaug_ref[rows, :] = _dot(packed, place_ref[...]).astype(BF16)

    acc_ref[...] = jnp.zeros_like(acc_ref)

    def scores(h, k0, masked):
        keys = pl.ds(pl.multiple_of(k0, FOX_TKV), FOX_TKV)
        s = _dot(kaug_ref[keys, h * LANES:(h + 1) * LANES],
                 qt_ref[0, h * Q_ROWS:(h + 1) * Q_ROWS, :])
        if masked:
            s = jnp.where(key_minus_query <= q0 - k0, s, NEG)
        s_ref[h % 2] = s
        return jnp.max(s, axis=0, keepdims=True)

    def kv_block(j, carry, diagonal):
        ms, top_next = carry
        k0 = pl.multiple_of(j * FOX_TKV, FOX_TKV)
        new_ms = []
        for h in range(N_HEADS_B):
            m_new = jnp.maximum(ms[h], top_next)
            if h + 1 < N_HEADS_B:
                top_next = scores(h + 1, k0, masked=diagonal)
            elif not diagonal:
                top_next = scores(0, k0 + FOX_TKV, masked=True)
            alpha = jnp.exp2(ms[h] - m_new)
            p = jnp.exp2(s_ref[h % 2] - m_new).astype(BF16)
            vt = vt_ref[0, h * V_ROWS:(h + 1) * V_ROWS, pl.ds(k0, FOX_TKV)]
            acc_ref[h] = acc_ref[h] * alpha + _dot(vt, p)
            new_ms.append(m_new)
        return tuple(new_ms), top_next

    ms = tuple(jnp.full((1, tq), NEG, F32) for _ in range(N_HEADS_B))
    carry = (ms, scores(0, 0, masked=True))
    below = functools.partial(kv_block, diagonal=False)
    carry = lax.fori_loop(0, n_full // 2, lambda i, c: below(2 * i + 1, below(2 * i, c)), carry)
    carry = lax.cond(n_full % 2 == 1, lambda c: below(n_full - 1, c), lambda c: c, carry)
    kv_block(n_full, carry, diagonal=True)
    outs = []
    for h in range(N_HEADS_B):
        acc = acc_ref[h]
        outs.append(acc[:HEAD_DIM] / acc[HEAD_DIM:HEAD_DIM + 1])
    o_ref[0] = jnp.concatenate(outs, axis=0).T.astype(o_ref.dtype)


def _forgetting_attention(qt, k, vt, fz, f_bias, l):
    batch, seq, _ = k.shape
    assert FOX_TKV % DECAY_CHUNK == 0
    place = jnp.asarray(_fox_place(), dtype=BF16)
    idx = np.arange(DECAY_CHUNK)
    tri = jnp.asarray(idx[None, :] <= idx[:, None], dtype=BF16)
    return pl.pallas_call(
        _fox_kernel,
        grid=(batch, seq // FOX_TQ),
        in_specs=[pl.BlockSpec((1, N_HEADS_B * Q_ROWS, FOX_TQ), lambda b, i: (b, 0, i)),
                  pl.BlockSpec((1, seq, WIDTH_B), lambda b, i: (b, 0, 0)),
                  pl.BlockSpec((1, N_HEADS_B * V_ROWS, seq), lambda b, i: (b, 0, 0)),
                  pl.BlockSpec((1, seq, F_COLS), lambda b, i: (b, 0, 0)),
                  _of_layer(f_bias, l), _resident(tri.shape), _resident(place.shape)],
        out_specs=pl.BlockSpec((1, FOX_TQ, WIDTH_B), lambda b, i: (b, i, 0)),
        out_shape=jax.ShapeDtypeStruct((batch, seq, WIDTH_B), BF16),
        scratch_shapes=[pltpu.VMEM((seq, N_HEADS_B * LANES), BF16),
                        pltpu.VMEM((N_HEADS_B, V_ROWS, FOX_TQ), F32),
                        pltpu.VMEM((2, FOX_TKV, FOX_TQ), F32)],
        compiler_params=_params(2),
        name="forgetting_attention",
    )(qt, k, vt, fz, f_bias, tri, place)


def _outproj_kernel(x_ref, sc_ref, sh_ref, gate_ref, ya_ref, yb_ref,
                    g_ref, wg_ref, wua_ref, wub_ref, wo_ref, o_ref, merged_ref):
    x = x_ref[0]
    d = x.shape[-1]
    h = _modulated_norm(x, g_ref[...], sc_ref[...], sh_ref[...]).astype(BF16)
    ya = ya_ref[0]
    yb = yb_ref[0]
    for c in range(d // OUT_CHUNK):
        cs = slice(c * OUT_CHUNK, (c + 1) * OUT_CHUNK)
        cs_b = slice(d + c * OUT_CHUNK, d + (c + 1) * OUT_CHUNK)
        gate_a = jax.nn.sigmoid(_dot(h, wg_ref[:, cs]))
        gate_b = jax.nn.sigmoid(_dot(h, wg_ref[:, cs_b]))
        merged = gate_a * _dot(ya, wua_ref[:, cs]) + gate_b * _dot(yb, wub_ref[:, cs])
        merged_ref[:, cs] = merged.astype(BF16)
    o_ref[0] = x + gate_ref[...] * _dot(merged_ref[...], wo_ref[...])


def _out_projection(x, mod, ya, yb, params, l):
    batch, seq, d = x.shape
    tok = lambda w: pl.BlockSpec((1, TM, w), lambda b, i: (b, i, 0))
    return pl.pallas_call(
        _outproj_kernel,
        grid=(batch, seq // TM),
        in_specs=[tok(d), _mod_vector(mod, l, MOD_SCALE1), _mod_vector(mod, l, MOD_SHIFT1),
                  _mod_vector(mod, l, MOD_GATE1), tok(WIDTH_G), tok(WIDTH_B)]
        + [_of_layer(p, l) for p in params],
        out_specs=tok(d),
        out_shape=jax.ShapeDtypeStruct((batch, seq, d), F32),
        scratch_shapes=[pltpu.VMEM((TM, d), BF16)],
        compiler_params=_params(2),
        name="out_projection",
    )(x, mod, mod, mod, ya, yb, *params)


def _ffn_kernel(*refs, final_norm):
    x_ref, sc_ref, sh_ref, gate_ref, g_ref, win_ref, wout_ref = refs[:7]
    refs = refs[7:]
    if final_norm:
        gf_ref = refs[0]
        refs = refs[1:]
    o_ref, h_ref, acc_ref = refs
    x = x_ref[0]
    h_ref[...] = _modulated_norm(x, g_ref[...], sc_ref[...], sh_ref[...]).astype(BF16)
    for c in range(D_FF // FF_CHUNK):
        cs = slice(c * FF_CHUNK, (c + 1) * FF_CHUNK)
        cs_up = slice(D_FF + c * FF_CHUNK, D_FF + (c + 1) * FF_CHUNK)
        h = h_ref[...]
        gt = _dot(h, win_ref[:, cs])
        up = _dot(h, win_ref[:, cs_up])
        act = ((gt * jax.nn.sigmoid(gt)) * up).astype(BF16)
        part = _dot(act, wout_ref[cs, :])
        if c == 0:
            acc_ref[...] = part
        else:
            acc_ref[...] += part
    y = x + gate_ref[...] * acc_ref[...]
    if final_norm:
        y = (y * lax.rsqrt(jnp.mean(y * y, axis=-1, keepdims=True) + EPS)) * gf_ref[...]
    o_ref[0] = y


def _ffn(x, mod, params, l, g_final):
    batch, seq, d = x.shape
    final_norm = g_final is not None
    tok = pl.BlockSpec((1, TM, d), lambda b, i: (b, i, 0))
    in_specs = [tok, _mod_vector(mod, l, MOD_SCALE2), _mod_vector(mod, l, MOD_SHIFT2),
                _mod_vector(mod, l, MOD_GATE2)] + [_of_layer(p, l) for p in params]
    operands = [x, mod, mod, mod, *params]
    if final_norm:
        in_specs.append(_resident((1, d)))
        operands.append(g_final)
    return pl.pallas_call(
        functools.partial(_ffn_kernel, final_norm=final_norm),
        grid=(batch, seq // TM),
        in_specs=in_specs,
        out_specs=tok,
        out_shape=jax.ShapeDtypeStruct((batch, seq, d), F32),
        scratch_shapes=[pltpu.VMEM((TM, d), BF16), pltpu.VMEM((TM, d), F32)],
        compiler_params=_params(2),
        name="swiglu_ffn",
    )(*operands)


def _tail_kernel(*refs, final_norm):
    (x_ref, sc1_ref, sh1_ref, gate1_ref, sc2_ref, sh2_ref, gate2_ref, ya_ref, yb_ref,
     g1_ref, wg_ref, wua_ref, wub_ref, wo_ref, g2_ref, win_ref, wout_ref) = refs[:17]
    refs = refs[17:]
    if final_norm:
        gf_ref = refs[0]
        refs = refs[1:]
    o_ref, merged_ref, h_ref, acc_ref, mid_ref = refs
    x = x_ref[0]
    d = x.shape[-1]
    h = _modulated_norm(x, g1_ref[...], sc1_ref[...], sh1_ref[...]).astype(BF16)
    ya = ya_ref[0]
    yb = yb_ref[0]
    for c in range(d // OUT_CHUNK):
        cs = slice(c * OUT_CHUNK, (c + 1) * OUT_CHUNK)
        cs_b = slice(d + c * OUT_CHUNK, d + (c + 1) * OUT_CHUNK)
        gate_a = jax.nn.sigmoid(_dot(h, wg_ref[:, cs]))
        gate_b = jax.nn.sigmoid(_dot(h, wg_ref[:, cs_b]))
        merged = gate_a * _dot(ya, wua_ref[:, cs]) + gate_b * _dot(yb, wub_ref[:, cs])
        merged_ref[:, cs] = merged.astype(BF16)
    mid_ref[...] = x + gate1_ref[...] * _dot(merged_ref[...], wo_ref[...])
    h_ref[...] = _modulated_norm(mid_ref[...], g2_ref[...], sc2_ref[...],
                                 sh2_ref[...]).astype(BF16)
    for c in range(D_FF // FF_CHUNK):
        cs = slice(c * FF_CHUNK, (c + 1) * FF_CHUNK)
        cs_up = slice(D_FF + c * FF_CHUNK, D_FF + (c + 1) * FF_CHUNK)
        hh = h_ref[...]
        gt = _dot(hh, win_ref[:, cs])
        up = _dot(hh, win_ref[:, cs_up])
        act = ((gt * jax.nn.sigmoid(gt)) * up).astype(BF16)
        part = _dot(act, wout_ref[cs, :])
        if c == 0:
            acc_ref[...] = part
        else:
            acc_ref[...] += part
    y = mid_ref[...] + gate2_ref[...] * acc_ref[...]
    if final_norm:
        y = (y * lax.rsqrt(jnp.mean(y * y, axis=-1, keepdims=True) + EPS)) * gf_ref[...]
    o_ref[0] = y


def _layer_tail(x, mod, ya, yb, out_params, ffn_params, l, g_final):
    batch, seq, d = x.shape
    final_norm = g_final is not None
    tok = lambda w: pl.BlockSpec((1, TM_TAIL, w), lambda b, i: (b, i, 0))
    mods = (MOD_SCALE1, MOD_SHIFT1, MOD_GATE1, MOD_SCALE2, MOD_SHIFT2, MOD_GATE2)
    params = (*out_params, *ffn_params)
    in_specs = ([tok(d)] + [_mod_vector(mod, l, j) for j in mods] + [tok(WIDTH_G), tok(WIDTH_B)]
                + [_of_layer(p, l) for p in params])
    operands = [x] + [mod] * len(mods) + [ya, yb, *params]
    if final_norm:
        in_specs.append(_resident((1, d)))
        operands.append(g_final)
    return pl.pallas_call(
        functools.partial(_tail_kernel, final_norm=final_norm),
        grid=(batch, seq // TM_TAIL),
        in_specs=in_specs,
        out_specs=tok(d),
        out_shape=jax.ShapeDtypeStruct((batch, seq, d), F32),
        scratch_shapes=[pltpu.VMEM((TM_TAIL, d), BF16), pltpu.VMEM((TM_TAIL, d), BF16),
                        pltpu.VMEM((TM_TAIL, d), F32), pltpu.VMEM((TM_TAIL, d), F32)],
        compiler_params=_params(2),
        name="out_projection_ffn",
    )(*operands)


def _win_kernel(w_ref, wa_ref, wqt_ref, wvt_ref, wkf_ref, wg_ref):
    w = w_ref[0]
    scale = QK_SCALE * LOG2E
    for g in range(N_GROUPS):
        for part in range(3):
            src = part * WIDTH_A + g * WIDTH_G
            piece = w[:, src:src + WIDTH_G]
            dst = (3 * g + part) * WIDTH_G
            wa_ref[0, :, dst:dst + WIDTH_G] = (piece * scale if part == 0 else piece).astype(BF16)
    qb0 = 3 * WIDTH_A
    wqt_ref[0] = (w[:, qb0:qb0 + WIDTH_B] * scale).T.astype(BF16)
    wvt_ref[0] = w[:, qb0 + 2 * WIDTH_B:qb0 + 3 * WIDTH_B].T.astype(BF16)
    f0 = qb0 + 3 * WIDTH_B
    forget = w[:, f0:f0 + F_COLS]
    lane = lax.broadcasted_iota(jnp.int32, forget.shape, 1)
    wkf_ref[0, :, :WIDTH_B] = w[:, qb0 + WIDTH_B:qb0 + 2 * WIDTH_B].astype(BF16)
    wkf_ref[0, :, WIDTH_B:] = jnp.where(lane < N_HEADS_B, forget, 0.0).astype(BF16)
    g0 = f0 + N_HEADS_B
    wg_ref[0] = w[:, g0:g0 + 2 * D_MODEL].astype(BF16)


def _split_w_in(w_in):
    depth, d, d_in = w_in.shape
    assert d_in == 3 * WIDTH_A + 3 * WIDTH_B + N_HEADS_B + 2 * D_MODEL and d % W_ROWS == 0
    row_block = lambda cols: pl.BlockSpec((1, W_ROWS, cols), lambda l, i: (l, i, 0))
    col_block = pl.BlockSpec((1, WIDTH_B, W_ROWS), lambda l, i: (l, 0, i))
    widths = (3 * WIDTH_A, WIDTH_B + F_COLS, 2 * D_MODEL)
    stack = lambda *tail: jax.ShapeDtypeStruct((depth,) + tail, BF16)
    wa, wqt, wvt, wkf, wg = pl.pallas_call(
        _win_kernel,
        grid=(depth, d // W_ROWS),
        in_specs=[row_block(d_in)],
        out_specs=[row_block(widths[0]), col_block, col_block, row_block(widths[1]),
                   row_block(widths[2])],
        out_shape=[stack(d, widths[0]), stack(WIDTH_B, d), stack(WIDTH_B, d),
                   stack(d, widths[1]), stack(d, widths[2])],
        compiler_params=_params(2),
        name="in_projection_weight_layout",
    )(w_in)
    return wa, wqt, wvt, wkf, wg


def kernel(x, c, w_ada, b_ada, norm_mix, w_in, b_forget, w_up_a, w_up_b, w_out,
           norm_ffn, w_ffn_in, w_ffn_out, norm_final):
    depth = w_ada.shape[0]
    batch, seq, d = x.shape
    assert seq % TM == 0 and seq % FOX_TQ == 0 and FOX_TQ == FOX_TKV
    mod = _modulation(c, w_ada, b_ada).reshape(depth, batch, N_MOD, 1, d)
    wa, wqt, wvt, wkf, wg = _split_w_in(w_in)
    in_params = (norm_mix.reshape(depth, 1, d), wa, wqt, wvt, wkf)
    out_params = (norm_mix.reshape(depth, 1, d), wg, w_up_a.astype(BF16), w_up_b.astype(BF16),
                  w_out.astype(BF16))
    ffn_params = (norm_ffn.reshape(depth, 1, d), w_ffn_in.astype(BF16), w_ffn_out.astype(BF16))
    f_bias = jnp.pad(b_forget.astype(F32), ((0, 0), (0, F_COLS - N_HEADS_B)))[:, None, :]
    for l in range(depth):
        za0, za1, za2, qt, k, vt, fz = _in_projection(x, mod, in_params, l)
        ya = _dilated_mixture((za0, za1, za2))
        yb = _forgetting_attention(qt, k, vt, fz, f_bias, l)
        g_final = norm_final.reshape(1, d) if l == depth - 1 else None
        x = _layer_tail(x, mod, ya, yb, out_params, ffn_params, l, g_final)
    return x
```
